```python
import math
import jax, jax.numpy as jnp
from jax import lax
import numpy as np

D_MODEL = 2048
BATCH = 32
SEQ = 256
DEPTH = 1
DEC_BATCH = 2
DEC_SEQ = 2048
PAST_LEN = 256

GRID_W = 64
D_A = 1024
HEAD = 64
H_A = D_A // HEAD
LORA_W = 64
LORA_A = 64
LORA_G = 160
D_B = 1024
CHUNK = 128
G_B = 8
C_B = D_B // G_B
D_FF = 5504
N_SHIFT = 3 * D_A + LORA_W + LORA_A + LORA_G
N_IN = N_SHIFT + 2 * D_B + 2 * D_MODEL
RW_SPLITS = (D_A, 2 * D_A, 3 * D_A, 3 * D_A + LORA_W, 3 * D_A + LORA_W + LORA_A)
ALPHA = (2.0 * DEPTH) ** 0.25
BETA = (8.0 * DEPTH) ** -0.25
LN_EPS = 1e-5
GN_EPS = 64e-5
NRM_EPS = 1e-12

kernel_name = 'hybrid_rwkv7_gmlp_diffusion_step'


def _layernorm(x, g, b, eps=LN_EPS):
    xf = x.astype(jnp.float32)
    mu = jnp.mean(xf, -1, keepdims=True)
    var = jnp.mean(jnp.square(xf - mu), -1, keepdims=True)
    return ((xf - mu) * lax.rsqrt(var + eps)).astype(x.dtype) * g + b


def _swiglu(h, w_in, w_out):
    a, b = jnp.split(h @ w_in, 2, axis=-1)
    return (jax.nn.silu(a) * b) @ w_out


def _neighbour_mean(s, on_grid):
    if on_grid:
        B, T, C = s.shape
        rows = T // GRID_W
        p = jnp.pad(s.reshape(B, rows, GRID_W, C), ((0, 0), (1, 1), (1, 1), (0, 0)))
        nb = (p[:, :-2, 1:-1] + p[:, 2:, 1:-1] + p[:, 1:-1, :-2] + p[:, 1:-1, 2:]) * 0.25
        return nb.reshape(B, T, C)
    p = jnp.pad(s, ((0, 0), (1, 1), (0, 0)))
    return (p[:, :-2] + p[:, 2:]) * 0.5


def _rwkv_scan(S0, r, w, k, v, kk, b, reverse):
    def step(S, inp):
        r_t, w_t, k_t, v_t, kk_t, b_t = inp
        Skk = jnp.einsum('bhvk,bhk->bhv', S, kk_t)
        S = S * w_t[:, :, None, :] - Skk[..., None] * b_t[:, :, None, :] + v_t[..., None] * k_t[:, :, None, :]
        return S, jnp.einsum('bhvk,bhk->bhv', S, r_t)
    xs = tuple(jnp.swapaxes(a, 0, 1) for a in (r, w, k, v, kk, b))
    S_T, ys = lax.scan(step, S0, xs, reverse=reverse)
    return S_T, jnp.swapaxes(ys, 0, 1)


def _rwkv_branch(z, S0f, S0b, w0, w2, a0, a2, g2, k_k, k_a, r_k, gn_g, gn_b):
    z = z.astype(jnp.float32)
    B, T, _ = z.shape
    hd = lambda t: t.reshape(B, T, H_A, HEAD)
    r, k, v, xw, xa, xg = jnp.split(z, RW_SPLITS, axis=-1)
    g = jax.nn.sigmoid(xg) @ g2
    kk = hd(k * k_k)
    kk = kk / jnp.maximum(jnp.sqrt(jnp.sum(kk * kk, -1, keepdims=True)), NRM_EPS)
    r4, v4 = hd(r), hd(v)
    inits = (S0f.astype(jnp.float32), S0b.astype(jnp.float32))
    y = jnp.zeros_like(r4)
    bonus = jnp.zeros_like(r4)
    finals = []
    for d in range(2):
        wlog = -jax.nn.softplus(-(w0[d] + jnp.tanh(xw) @ w2[d])) - 0.5
        decay = jnp.exp(-jnp.exp(wlog))
        a = jax.nn.sigmoid(a0[d] + xa @ a2[d])
        kd = hd(k * (1.0 + (a - 1.0) * k_a))
        S_T, yd = _rwkv_scan(inits[d], r4, hd(decay), kd, v4, kk, kk * hd(a), reverse=(d == 1))
        y = y + yd
        bonus = bonus + jnp.sum(r4 * kd * r_k, -1, keepdims=True) * v4
        finals.append(S_T)
    mu = jnp.mean(y, -1, keepdims=True)
    var = jnp.mean(jnp.square(y - mu), -1, keepdims=True)
    yn = ((y - mu) * lax.rsqrt(var + GN_EPS)).reshape(B, T, D_A) * gn_g + gn_b
    out = (yn + bonus.reshape(B, T, D_A)) * g
    return out, finals[0], finals[1]


def _spatial_gating(uv, ln_g, ln_b, w_s, b_s):
    u, v = jnp.split(uv, 2, axis=-1)
    v = _layernorm(v, ln_g, ln_b)
    B, T, _ = v.shape
    vc = v.reshape(B, T // CHUNK, CHUNK, G_B, C_B)
    s = jnp.einsum('gpq,bnqgc->bnpgc', w_s, vc) + jnp.transpose(b_s)[None, None, :, :, None]
    return u * s.reshape(B, T, D_B)


def _layer(x, cond, on_grid, S0f, S0b, p):
    mod = (jax.nn.silu(cond) @ p['w_ada'] + p['b_ada']).reshape(cond.shape[0], 1, 9, D_MODEL)
    shift = lambda i: mod[:, :, 3 * i]
    scale = lambda i: mod[:, :, 3 * i + 1]
    gate = lambda i: mod[:, :, 3 * i + 2]
    modulate = lambda t, i: t * (1.0 + scale(i)) + shift(i)
    f1 = _swiglu(modulate(x, 0), p['ffn_w_in'][0], p['ffn_w_out'][0])
    x = _layernorm(ALPHA * x + 0.5 * gate(0) * f1, p['ln_g'][0], p['ln_b'][0])
    h = modulate(x, 1)
    proj = h @ p['w_in']
    zs = proj[..., :N_SHIFT]
    zs = zs + p['shift_mu'] * (_neighbour_mean(zs, on_grid) - zs)
    yA, Sf, Sb = _rwkv_branch(zs, S0f, S0b, p['rw_w0'], p['rw_w2'], p['rw_a0'], p['rw_a2'], p['rw_g2'],
                              p['rw_k_k'], p['rw_k_a'], p['rw_r_k'], p['rw_gn_g'], p['rw_gn_b'])
    uv = jax.nn.gelu(proj[..., N_SHIFT:N_SHIFT + 2 * D_B], approximate=False)
    yB = _spatial_gating(uv, p['sg_ln_g'], p['sg_ln_b'], p['sg_w'], p['sg_b'])
    gts = jax.nn.sigmoid(proj[..., N_SHIFT + 2 * D_B:])
    merged = gts[..., :D_MODEL] * (yA.astype(x.dtype) @ p['w_pa']) + gts[..., D_MODEL:] * (yB @ p['w_pb'])
    x = _layernorm(ALPHA * x + gate(1) * (merged @ p['w_o']), p['ln_g'][1], p['ln_b'][1])
    f2 = _swiglu(modulate(x, 2), p['ffn_w_in'][1], p['ffn_w_out'][1])
    x = _layernorm(ALPHA * x + 0.5 * gate(2) * f2, p['ln_g'][2], p['ln_b'][2])
    return x, Sf, Sb


def setup_inputs(seed: int = 0) -> dict:
    key = jax.random.key(seed)
    ks = jax.random.split(key, 40)
    nrm = lambda k, shape, s: jax.random.normal(k, shape, jnp.float32) * s
    L = DEPTH
    return {
        'x_prompt': nrm(ks[0], (BATCH, SEQ, D_MODEL), 1.0),
        'x_sample': nrm(ks[1], (DEC_BATCH, DEC_SEQ, D_MODEL), 1.0),
        'c': nrm(ks[2], (DEC_BATCH, D_MODEL), 1.0),
        'state_rwkv_fwd': nrm(ks[3], (DEC_BATCH, L, H_A, HEAD, HEAD), 0.1),
        'state_rwkv_bwd': nrm(ks[4], (DEC_BATCH, L, H_A, HEAD, HEAD), 0.1),
        'c_ctx': nrm(ks[5], (D_MODEL,), 1.0),
        'w_ada': nrm(ks[6], (L, D_MODEL, 9 * D_MODEL), 0.5 * D_MODEL ** -0.5),
        'b_ada': nrm(ks[7], (L, 9 * D_MODEL), 0.01),
        'ln_g': 1.0 + nrm(ks[8], (L, 3, D_MODEL), 0.01),
        'ln_b': nrm(ks[9], (L, 3, D_MODEL), 0.01),
        'ffn_w_in': nrm(ks[10], (L, 2, D_MODEL, 2 * D_FF), D_MODEL ** -0.5),
        'ffn_w_out': nrm(ks[11], (L, 2, D_FF, D_MODEL), BETA * D_FF ** -0.5),
        'w_in': nrm(ks[12], (L, D_MODEL, N_IN), D_MODEL ** -0.5),
        'shift_mu': jax.random.uniform(ks[13], (L, N_SHIFT), jnp.float32),
        'rw_w0': jax.random.uniform(ks[14], (L, 2, D_A), jnp.float32, minval=-3.0, maxval=0.0),
        'rw_w2': nrm(ks[15], (L, 2, LORA_W, D_A), 0.1 * LORA_W ** -0.5),
        'rw_a0': nrm(ks[16], (L, 2, D_A), 0.1),
        'rw_a2': nrm(ks[17], (L, 2, LORA_A, D_A), 0.1 * LORA_A ** -0.5),
        'rw_g2': nrm(ks[18], (L, LORA_G, D_A), LORA_G ** -0.5),
        'rw_k_k': 0.85 + nrm(ks[19], (L, D_A), 0.02),
        'rw_k_a': 1.0 + nrm(ks[20], (L, D_A), 0.02),
        'rw_r_k': nrm(ks[21], (L, H_A, HEAD), 0.1),
        'rw_gn_g': 1.0 + nrm(ks[22], (L, D_A), 0.01),
        'rw_gn_b': nrm(ks[23], (L, D_A), 0.01),
        'sg_ln_g': 1.0 + nrm(ks[24], (L, D_B), 0.01),
        'sg_ln_b': nrm(ks[25], (L, D_B), 0.01),
        'sg_w': nrm(ks[26], (L, G_B, CHUNK, CHUNK), 0.5 * CHUNK ** -0.5),
        'sg_b': 1.0 + nrm(ks[27], (L, G_B, CHUNK), 0.01),
        'w_pa': nrm(ks[28], (L, D_A, D_MODEL), D_A ** -0.5),
        'w_pb': nrm(ks[29], (L, D_B, D_MODEL), D_B ** -0.5),
        'w_o': nrm(ks[30], (L, D_MODEL, D_MODEL), BETA * D_MODEL ** -0.5),
    }


def reference(x_prompt, x_sample, c, state_rwkv_fwd, state_rwkv_bwd, c_ctx, w_ada, b_ada, ln_g, ln_b,
              ffn_w_in, ffn_w_out, w_in, shift_mu, rw_w0, rw_w2, rw_a0, rw_a2, rw_g2, rw_k_k, rw_k_a,
              rw_r_k, rw_gn_g, rw_gn_b, sg_ln_g, sg_ln_b, sg_w, sg_b, w_pa, w_pb, w_o):
    xp, xs = x_prompt, x_sample
    zero_state = jnp.zeros((x_prompt.shape[0], H_A, HEAD, HEAD), jnp.float32)
    cond_ctx = c_ctx[None, :]
    new_f, new_b = [], []
    for l in range(DEPTH):
        p = {'w_ada': w_ada[l], 'b_ada': b_ada[l], 'ln_g': ln_g[l], 'ln_b': ln_b[l],
             'ffn_w_in': ffn_w_in[l], 'ffn_w_out': ffn_w_out[l], 'w_in': w_in[l], 'shift_mu': shift_mu[l],
             'rw_w0': rw_w0[l], 'rw_w2': rw_w2[l], 'rw_a0': rw_a0[l], 'rw_a2': rw_a2[l], 'rw_g2': rw_g2[l],
             'rw_k_k': rw_k_k[l], 'rw_k_a': rw_k_a[l], 'rw_r_k': rw_r_k[l], 'rw_gn_g': rw_gn_g[l],
             'rw_gn_b': rw_gn_b[l], 'sg_ln_g': sg_ln_g[l], 'sg_ln_b': sg_ln_b[l], 'sg_w': sg_w[l],
             'sg_b': sg_b[l], 'w_pa': w_pa[l], 'w_pb': w_pb[l], 'w_o': w_o[l]}
        xp, sf, sb = _layer(xp, cond_ctx, False, zero_state, zero_state, p)
        new_f.append(sf)
        new_b.append(sb)
        xs, _, _ = _layer(xs, c, True, state_rwkv_fwd[:, l], state_rwkv_bwd[:, l], p)
    new_state_fwd = jnp.stack(new_f, axis=1).astype(x_prompt.dtype)
    new_state_bwd = jnp.stack(new_b, axis=1).astype(x_prompt.dtype)
    return (xp, xs, new_state_fwd, new_state_bwd)
```

```python
import functools

import jax
import jax.numpy as jnp
from jax import lax
from jax.experimental import pallas as pl
from jax.experimental.pallas import tpu as pltpu

F32 = jnp.float32
BF16 = jnp.bfloat16

D_MODEL = 2048
DEPTH = 1
D_A = 1024
HEAD = 64
H_A = D_A // HEAD
LORA_W = 64
LORA_A = 64
LORA_G = 160
D_B = 1024
CHUNK = 128
G_B = 8
D_FF = 5504
N_SHIFT = 3 * D_A + LORA_W + LORA_A + LORA_G
GRID_W = 64
ALPHA = (2.0 * DEPTH) ** 0.25
LN_EPS = 1e-5
GN_EPS = 64e-5
NRM_EPS = 1e-12

LANES = 128
VMEM_LIMIT = 56 * 1024 * 1024

LORA_G_PAD = 256
N_SHIFT_PAD = 3 * D_A + LORA_W + LORA_A + LORA_G_PAD
D_FF_TILE = 512
D_FF_PAD = ((D_FF + D_FF_TILE - 1) // D_FF_TILE) * D_FF_TILE
SCAN_C = 64
PAIR = 2 * HEAD
N_PAIR = H_A // 2


def _cparams(sem):
    return pltpu.CompilerParams(dimension_semantics=sem, vmem_limit_bytes=VMEM_LIMIT)


def _dot(a, b):
    return jnp.dot(a, b, preferred_element_type=F32)


def _dot_nt(a, b):
    return lax.dot_general(a, b, (((1,), (1,)), ((), ())), preferred_element_type=F32)


def _dot2(x, m):
    hi = x.astype(BF16)
    lo = (x - hi.astype(F32)).astype(BF16)
    return _dot(hi, m) + _dot(lo, m)


def _ada_kernel(c_ref, w_ref, b_ref, o_ref):
    c = c_ref[...]
    s = (c * jax.nn.sigmoid(c)).astype(BF16)
    o_ref[...] = _dot(s, w_ref[...].astype(BF16)) + b_ref[...]


def _ada(cond8, w_ada, b_ada, tn=1024):
    rows, d = cond8.shape
    n = w_ada.shape[1]
    return pl.pallas_call(
        _ada_kernel,
        grid=(n // tn,),
        in_specs=[pl.BlockSpec((rows, d), lambda j: (0, 0)),
                  pl.BlockSpec((d, tn), lambda j: (0, j)),
                  pl.BlockSpec((1, tn), lambda j: (0, j))],
        out_specs=pl.BlockSpec((rows, tn), lambda j: (0, j)),
        out_shape=jax.ShapeDtypeStruct((rows, n), F32),
        compiler_params=_cparams(("parallel",)),
        name="ada",
    )(cond8, w_ada, b_ada.reshape(1, n))


def _act_none(acc):
    return acc


def _act_swiglu(acc):
    half = acc.shape[1] // 2
    a = acc[:, :half]
    return (a * jax.nn.sigmoid(a)) * acc[:, half:]


def _act_gelu(acc):
    return 0.5 * acc * (1.0 + lax.erf(acc * (2.0 ** -0.5)))


def _act_sigmoid(acc):
    return jax.nn.sigmoid(acc)


def _mm_mod_kernel(x_ref, mod_ref, w_ref, o_ref, h_ref, *, which, act):
    @pl.when(pl.program_id(1) == 0)
    def _():
        shift = mod_ref[0, 3 * which:3 * which + 1, :]
        scale = mod_ref[0, 3 * which + 1:3 * which + 2, :]
        h_ref[...] = (x_ref[...] * (1.0 + scale) + shift).astype(BF16)

    o_ref[...] = act(_dot(h_ref[...], w_ref[...])).astype(o_ref.dtype)


def _mm_mod(x, mod3, w, group_of, *, which, act, tm, tn, tn_out, out_dtype):
    n, d = x.shape
    nj = w.shape[1] // tn
    return pl.pallas_call(
        functools.partial(_mm_mod_kernel, which=which, act=act),
        grid=(n // tm, nj),
        in_specs=[pl.BlockSpec((tm, d), lambda i, j: (i, 0)),
                  pl.BlockSpec((1, 9, d), lambda i, j: (group_of(i), 0, 0)),
                  pl.BlockSpec((d, tn), lambda i, j: (0, j))],
        out_specs=pl.BlockSpec((tm, tn_out), lambda i, j: (i, j)),
        out_shape=jax.ShapeDtypeStruct((n, nj * tn_out), out_dtype),
        scratch_shapes=[pltpu.VMEM((tm, d), BF16)],
        compiler_params=_cparams(("parallel", "arbitrary")),
        name="mm_mod",
    )(x, mod3, w)


def _mm_res_ln_kernel(a_ref, w_ref, x_ref, mod_ref, g_ref, b_ref, o_ref, *, gate_row, coef, nk):
    k = pl.program_id(1)
    part = _dot(a_ref[...], w_ref[...])

    @pl.when(k == 0)
    def _():
        o_ref[...] = part

    @pl.when(k > 0)
    def _():
        o_ref[...] += part

    @pl.when(k == nk - 1)
    def _():
        gate = mod_ref[0, gate_row:gate_row + 1, :]
        y = ALPHA * x_ref[...] + (coef * gate) * o_ref[...]
        mu = jnp.mean(y, axis=-1, keepdims=True)
        yc = y - mu
        var = jnp.mean(yc * yc, axis=-1, keepdims=True)
        o_ref[...] = yc * lax.rsqrt(var + LN_EPS) * g_ref[...] + b_ref[...]


def _mm_res_ln(a, w, x, mod3, ln_g, ln_b, group_of, *, gate_row, coef, tm, tk):
    n, kdim = a.shape
    d = w.shape[1]
    nk = kdim // tk
    return pl.pallas_call(
        functools.partial(_mm_res_ln_kernel, gate_row=gate_row, coef=coef, nk=nk),
        grid=(n // tm, nk),
        in_specs=[pl.BlockSpec((tm, tk), lambda i, k: (i, k)),
                  pl.BlockSpec((tk, d), lambda i, k: (k, 0)),
                  pl.BlockSpec((tm, d), lambda i, k: (i, 0)),
                  pl.BlockSpec((1, 9, d), lambda i, k: (group_of(i), 0, 0)),
                  pl.BlockSpec((1, d), lambda i, k: (0, 0)),
                  pl.BlockSpec((1, d), lambda i, k: (0, 0))],
        out_specs=pl.BlockSpec((tm, d), lambda i, k: (i, 0)),
        out_shape=jax.ShapeDtypeStruct((n, d), F32),
        compiler_params=_cparams(("parallel", "arbitrary")),
        name="mm_res_ln",
    )(a, w, x, mod3, ln_g.reshape(1, d), ln_b.reshape(1, d))


def _shift_kernel(z_ref, mu_ref, o_ref, *, grid_w):
    z = z_ref[...]
    t_len = z.shape[0]
    t = lax.broadcasted_iota(jnp.int32, (t_len, 1), 0)

    def from_offset(off):
        rolled = pltpu.roll(z, (-off) % t_len, 0)
        ok = jnp.logical_and(t + off >= 0, t + off < t_len)
        return rolled, ok

    if grid_w is None:
        up, up_ok = from_offset(-1)
        dn, dn_ok = from_offset(1)
        nb = (jnp.where(up_ok, up, 0.0) + jnp.where(dn_ok, dn, 0.0)) * 0.5
    else:
        col = jnp.bitwise_and(t, grid_w - 1)
        above, above_ok = from_offset(-grid_w)
        below, below_ok = from_offset(grid_w)
        left, left_ok = from_offset(-1)
        right, right_ok = from_offset(1)
        left_ok = jnp.logical_and(left_ok, col > 0)
        right_ok = jnp.logical_and(right_ok, col < grid_w - 1)
        nb = (jnp.where(above_ok, above, 0.0) + jnp.where(below_ok, below, 0.0)
              + jnp.where(left_ok, left, 0.0) + jnp.where(right_ok, right, 0.0)) * 0.25
    o_ref[...] = z + mu_ref[...] * (nb - z)


def _token_shift(zs, mu, *, row0, n_seq, t_len, grid_w, tc=384):
    ncol = zs.shape[1]
    blk0 = row0 // t_len
    return pl.pallas_call(
        functools.partial(_shift_kernel, grid_w=grid_w),
        grid=(n_seq, ncol // tc),
        in_specs=[pl.BlockSpec((t_len, tc), lambda b, j: (blk0 + b, j)),
                  pl.BlockSpec((1, tc), lambda b, j: (0, j))],
        out_specs=pl.BlockSpec((t_len, tc), lambda b, j: (b, j)),
        out_shape=jax.ShapeDtypeStruct((n_seq * t_len, ncol), F32),
        compiler_params=_cparams(("parallel", "parallel")),
        name="token_shift",
    )(zs, mu.reshape(1, ncol))


def _prep_kernel(z_ref, w2_ref, a2_ref, g2_ref, w0_ref, a0_ref, kk_ref_, ka_ref, rk_ref, seg_ref,
                 kk_o, lw0_o, kd0_o, bb0_o, lw1_o, kd1_o, bb1_o, bonus_o, g_o):
    r = z_ref[:, 0:D_A]
    k = z_ref[:, D_A:2 * D_A]
    v = z_ref[:, 2 * D_A:3 * D_A]
    xwa = z_ref[:, 3 * D_A:3 * D_A + LANES]
    xg = z_ref[:, 3 * D_A + LANES:3 * D_A + LANES + LORA_G_PAD]
    seg = seg_ref[...]

    g_o[...] = _dot(jax.nn.sigmoid(xg).astype(BF16), g2_ref[...])

    kk = k * kk_ref_[...]
    ksq = kk * kk
    ss = jnp.concatenate([_dot2(ksq[:, p * LANES:(p + 1) * LANES], seg) for p in range(N_PAIR)], axis=1)
    kk = kk / jnp.maximum(jnp.sqrt(ss), NRM_EPS)
    kk_o[...] = kk

    tw = jnp.tanh(xwa).astype(BF16)
    xa = xwa.astype(BF16)
    k_a = ka_ref[...]
    r_k = rk_ref[...]
    outs = ((lw0_o, kd0_o, bb0_o), (lw1_o, kd1_o, bb1_o))
    rkd = jnp.zeros_like(r)
    for d in range(2):
        lw_o, kd_o, bb_o = outs[d]
        wl = w0_ref[d:d + 1, :] + _dot(tw, w2_ref[d])
        wlog = -(jnp.maximum(-wl, 0.0) + jnp.log1p(jnp.exp(-jnp.abs(wl)))) - 0.5
        lw_o[...] = -jnp.exp(wlog)
        a = jax.nn.sigmoid(a0_ref[d:d + 1, :] + _dot(xa, a2_ref[d]))
        kd = k * (1.0 + (a - 1.0) * k_a)
        kd_o[...] = kd
        bb_o[...] = kk * a
        rkd = rkd + r * kd * r_k
    rs = jnp.concatenate([_dot2(rkd[:, p * LANES:(p + 1) * LANES], seg) for p in range(N_PAIR)], axis=1)
    bonus_o[...] = rs * v


def _rwkv_prep(zs, w2p, a2p, g2p, w0, a0, k_k, k_a, r_k, *, tm=256):
    n = zs.shape[0]
    head_of_lane = jnp.arange(LANES) // HEAD
    seg = (head_of_lane[:, None] == head_of_lane[None, :]).astype(BF16)
    full = lambda shape: pl.BlockSpec(shape, lambda i: (0,) * len(shape))
    row = pl.BlockSpec((tm, D_A), lambda i: (i, 0))
    out = jax.ShapeDtypeStruct((n, D_A), F32)
    return pl.pallas_call(
        _prep_kernel,
        grid=(n // tm,),
        in_specs=[pl.BlockSpec((tm, N_SHIFT_PAD), lambda i: (i, 0)),
                  full((2, LANES, D_A)), full((2, LANES, D_A)), full((LORA_G_PAD, D_A)),
                  full((2, D_A)), full((2, D_A)), full((1, D_A)), full((1, D_A)), full((1, D_A)),
                  full((LANES, LANES))],
        out_specs=[row] * 9,
        out_shape=[out] * 9,
        compiler_params=_cparams(("parallel",)),
        name="rwkv_prep",
    )(zs, w2p, a2p, g2p, w0, a0, k_k.reshape(1, D_A), k_a.reshape(1, D_A), r_k.reshape(1, D_A), seg)


def _blockdiag(x, lane_a):
    return jnp.concatenate([jnp.where(lane_a, x, 0.0), jnp.where(lane_a, 0.0, x)], axis=0)


def _pair_mm(a, b, lane_a):
    return _dot(a.astype(BF16), _blockdiag(b, lane_a).astype(BF16))


def _scan_chunk(c, reverse, refs, s_scr, y_ref, consts):
    r_ref, v_ref, kk_ref, lw_ref, kd_ref, bb_ref = refs
    lane_a, tri2, strict, incl, eye2, level_masks, bd_mask = consts
    cs = SCAN_C
    sl = pl.ds(pl.multiple_of(c * cs, cs), cs)
    r = r_ref[sl, :]
    v = v_ref[sl, :]
    kk = kk_ref[sl, :]
    lw = lw_ref[sl, :]
    kd = kd_ref[sl, :]
    bb = bb_ref[sl, :]

    lw_hi = lw.astype(BF16)
    lw_lo = (lw - lw_hi.astype(F32)).astype(BF16)
    lp = _dot(tri2, jnp.concatenate([lw_hi, lw_lo], axis=0))
    last = cs - 1 if not reverse else 0
    total = lp[last:last + 1, :]
    p_inv = jnp.exp(-lp)
    alpha = kk * jnp.exp(lp - lw)
    beta = bb * p_inv
    kappa = kd * p_inv
    rho = r * jnp.exp(lp)
    to_end = jnp.exp(total - lp)
    p_end = jnp.exp(total)

    lhs = jnp.concatenate([alpha, rho], axis=0).astype(BF16)
    rhs = jnp.concatenate([_blockdiag(beta, lane_a), _blockdiag(kappa, lane_a)], axis=0).astype(BF16)
    gram = _dot_nt(lhs, rhs)
    l_beta = jnp.where(strict, gram[0:cs, 0:PAIR], 0.0)
    l_kappa = jnp.where(strict, gram[0:cs, PAIR:2 * PAIR], 0.0)
    m_beta = jnp.where(incl, gram[cs:2 * cs, 0:PAIR], 0.0)
    m_kappa = jnp.where(incl, gram[cs:2 * cs, PAIR:2 * PAIR], 0.0)

    dinv = eye2 - jnp.where(level_masks[0], l_beta, 0.0)
    for lm in level_masks[1:]:
        l_off = jnp.where(lm, l_beta, 0.0)
        dinv = dinv - _pair_mm(dinv, _pair_mm(l_off, dinv, lane_a), lane_a)

    s0 = s_scr[...]
    from_state = _dot_nt(lhs, s0.astype(BF16))
    v_bd = _blockdiag(v, lane_a).astype(BF16)
    w_rhs = from_state[0:cs] + _dot(l_kappa.astype(BF16), v_bd)
    u = _pair_mm(dinv, w_rhs, lane_a)
    u_bd = _blockdiag(u, lane_a).astype(BF16)
    y = from_state[cs:2 * cs] + _dot(jnp.concatenate([m_kappa, -m_beta], axis=1).astype(BF16),
                                     jnp.concatenate([v_bd, u_bd], axis=0))
    y_ref[sl, :] += y

    zt = jnp.concatenate([v, -u], axis=0).T
    k_end = kd * to_end
    b_end = bb * to_end
    upd = _dot(zt.astype(BF16), jnp.concatenate([k_end, b_end], axis=0).astype(BF16))
    s_scr[...] = s0 * p_end + jnp.where(bd_mask, upd, 0.0)


def _scan_consts():
    cs = SCAN_C
    lane = lax.broadcasted_iota(jnp.int32, (cs, PAIR), 1)
    t = lax.broadcasted_iota(jnp.int32, (cs, PAIR), 0)
    i = jnp.bitwise_and(lane, HEAD - 1)
    lane_a = lane < HEAD
    out = {}
    for reverse in (False, True):
        before = (i > t) if reverse else (i < t)
        strict = before
        incl = jnp.logical_or(before, i == t)
        tri2 = incl.astype(BF16)
        eye2 = (i == t).astype(F32)
        level_masks = []
        s = 1
        while s < cs:
            sh_s = s.bit_length() - 1
            same_2s = lax.shift_right_logical(t, sh_s + 1) == lax.shift_right_logical(i, sh_s + 1)
            diff_s = lax.shift_right_logical(t, sh_s) != lax.shift_right_logical(i, sh_s)
            level_masks.append(jnp.logical_and(jnp.logical_and(same_2s, diff_s), strict))
            s *= 2
        out[reverse] = (tri2, strict, incl, eye2, tuple(level_masks))
    row = lax.broadcasted_iota(jnp.int32, (PAIR, PAIR), 0)
    col = lax.broadcasted_iota(jnp.int32, (PAIR, PAIR), 1)
    bd_mask = (row < HEAD) == (col < HEAD)
    return lane_a, out, bd_mask


def _scan_kernel(r_ref, v_ref, kk_ref, lw0, kd0, bb0, lw1, kd1, bb1, s0f_ref, s0b_ref,
                 y_ref, sf_ref, sb_ref, sf_scr, sb_scr, *, t_len):
    n_chunk = t_len // SCAN_C
    lane_a, per_dir, bd_mask = _scan_consts()
    consts_f = (lane_a,) + per_dir[False] + (bd_mask,)
    consts_b = (lane_a,) + per_dir[True] + (bd_mask,)
    y_ref[...] = jnp.zeros_like(y_ref)
    sf_scr[...] = s0f_ref[0, 0]
    sb_scr[...] = s0b_ref[0, 0]

    def body(c, carry):
        _scan_chunk(c, False, (r_ref, v_ref, kk_ref, lw0, kd0, bb0), sf_scr, y_ref, consts_f)
        _scan_chunk(n_chunk - 1 - c, True, (r_ref, v_ref, kk_ref, lw1, kd1, bb1), sb_scr, y_ref, consts_b)
        return carry

    lax.fori_loop(0, n_chunk, body, 0)
    sf_ref[0, 0] = sf_scr[...]
    sb_ref[0, 0] = sb_scr[...]


def _rwkv_scan(zs, prep, s0f, s0b, *, row0, n_seq, t_len):
    kk, lw0, kd0, bb0, lw1, kd1, bb1 = prep
    blk0 = row0 // t_len
    v_col0 = 2 * D_A // LANES
    tok = lambda off: pl.BlockSpec((t_len, LANES), lambda b, p: (blk0 + b, off + p))
    st = pl.BlockSpec((1, 1, PAIR, PAIR), lambda b, p: (b, p, 0, 0))
    st_shape = jax.ShapeDtypeStruct((n_seq, N_PAIR, PAIR, PAIR), F32)
    return pl.pallas_call(
        functools.partial(_scan_kernel, t_len=t_len),
        grid=(n_seq, N_PAIR),
        in_specs=[tok(0), tok(v_col0)] + [tok(0)] * 7 + [st, st],
        out_specs=[pl.BlockSpec((t_len, LANES), lambda b, p: (b, p)), st, st],
        out_shape=[jax.ShapeDtypeStruct((n_seq * t_len, D_A), F32), st_shape, st_shape],
        scratch_shapes=[pltpu.VMEM((PAIR, PAIR), F32), pltpu.VMEM((PAIR, PAIR), F32)],
        compiler_params=_cparams(("parallel", "parallel")),
        name="rwkv_scan",
    )(zs, zs, kk, lw0, kd0, bb0, lw1, kd1, bb1, s0f, s0b)


def _gmlp_kernel(uv_ref, lg_ref, lb_ref, ws_ref, bs_ref, o_ref, *, n_chunk):
    u = uv_ref[:, 0:D_B].astype(F32)
    v = uv_ref[:, D_B:2 * D_B].astype(F32)
    mu = jnp.mean(v, axis=-1, keepdims=True)
    vc = v - mu
    var = jnp.mean(vc * vc, axis=-1, keepdims=True)
    vn = ((vc * lax.rsqrt(var + LN_EPS)) * lg_ref[...] + lb_ref[...]).astype(BF16)
    for c in range(n_chunk):
        rows = slice(c * CHUNK, (c + 1) * CHUNK)
        for g in range(G_B):
            cols = slice(g * LANES, (g + 1) * LANES)
            s = _dot(ws_ref[g], vn[rows, cols]) + bs_ref[g]
            o_ref[rows, cols] = (u[rows, cols] * s).astype(o_ref.dtype)


def _gmlp(uv, ln_g, ln_b, ws, bs_b, *, n_chunk=4):
    n = uv.shape[0]
    tm = n_chunk * CHUNK
    full = lambda shape: pl.BlockSpec(shape, lambda i: (0,) * len(shape))
    return pl.pallas_call(
        functools.partial(_gmlp_kernel, n_chunk=n_chunk),
        grid=(n // tm,),
        in_specs=[pl.BlockSpec((tm, 2 * D_B), lambda i: (i, 0)),
                  full((1, D_B)), full((1, D_B)), full((G_B, CHUNK, CHUNK)), full((G_B, CHUNK, LANES))],
        out_specs=pl.BlockSpec((tm, D_B), lambda i: (i, 0)),
        out_shape=jax.ShapeDtypeStruct((n, D_B), BF16),
        compiler_params=_cparams(("parallel",)),
        name="gmlp",
    )(uv, ln_g.reshape(1, D_B), ln_b.reshape(1, D_B), ws, bs_b)


def _merge_kernel(y_ref, bonus_ref, g_ref, gng_ref, gnb_ref, avg_ref, yb_ref, gts_ref, wpa_ref, wpb_ref, o_ref):
    avg = avg_ref[...]
    parts = []
    for p in range(N_PAIR):
        cols = slice(p * LANES, (p + 1) * LANES)
        yp = y_ref[:, cols]
        yc = yp - _dot2(yp, avg)
        var = _dot2(yc * yc, avg)
        parts.append(yc * lax.rsqrt(var + GN_EPS))
    yn = jnp.concatenate(parts, axis=1)
    oa = ((yn * gng_ref[...] + gnb_ref[...] + bonus_ref[...]) * g_ref[...]).astype(BF16)
    o_ref[...] = (gts_ref[:, 0:D_MODEL] * _dot(oa, wpa_ref[...])
                  + gts_ref[:, D_MODEL:2 * D_MODEL] * _dot(yb_ref[...], wpb_ref[...])).astype(o_ref.dtype)


def _merge(y, bonus, g, gn_g, gn_b, yb, gts, w_pa, w_pb, *, tm=256):
    n = y.shape[0]
    head_of_lane = jnp.arange(LANES) // HEAD
    avg = ((head_of_lane[:, None] == head_of_lane[None, :]).astype(F32) / HEAD).astype(BF16)
    full = lambda shape: pl.BlockSpec(shape, lambda i: (0,) * len(shape))
    row = lambda w: pl.BlockSpec((tm, w), lambda i: (i, 0))
    return pl.pallas_call(
        _merge_kernel,
        grid=(n // tm,),
        in_specs=[row(D_A), row(D_A), row(D_A), full((1, D_A)), full((1, D_A)), full((LANES, LANES)),
                  row(D_B), row(2 * D_MODEL), full((D_A, D_MODEL)), full((D_B, D_MODEL))],
        out_specs=row(D_MODEL),
        out_shape=jax.ShapeDtypeStruct((n, D_MODEL), BF16),
        compiler_params=_cparams(("parallel",)),
        name="merge",
    )(y, bonus, g, gn_g.reshape(1, D_A), gn_b.reshape(1, D_A), avg, yb, gts, w_pa, w_pb)


def _pair_states(s):
    b = s.shape[0]
    s = s.reshape(b, N_PAIR, 2, HEAD, HEAD)
    z = jnp.zeros_like(s[:, :, 0])
    top = jnp.concatenate([s[:, :, 0], z], axis=-1)
    bot = jnp.concatenate([z, s[:, :, 1]], axis=-1)
    return jnp.concatenate([top, bot], axis=-2)


def _unpair_states(sp):
    b = sp.shape[0]
    return jnp.stack([sp[:, :, :HEAD, :HEAD], sp[:, :, HEAD:, HEAD:]], axis=2).reshape(b, H_A, HEAD, HEAD)


def kernel(x_prompt, x_sample, c, state_rwkv_fwd, state_rwkv_bwd, c_ctx, w_ada, b_ada, ln_g, ln_b,
           ffn_w_in, ffn_w_out, w_in, shift_mu, rw_w0, rw_w2, rw_a0, rw_a2, rw_g2, rw_k_k, rw_k_a,
           rw_r_k, rw_gn_g, rw_gn_b, sg_ln_g, sg_ln_b, sg_w, sg_b, w_pa, w_pb, w_o):
    n_ctx_seq, t_ctx, d = x_prompt.shape
    n_lat_seq, t_lat, _ = x_sample.shape
    n_ctx = n_ctx_seq * t_ctx
    n_lat = n_lat_seq * t_lat
    tm = 1024
    ctx_tiles = n_ctx // tm
    lat_tiles_per_seq = t_lat // tm

    def group_of(i):
        return jnp.where(i < ctx_tiles, 0, 1 + (i - ctx_tiles) // lat_tiles_per_seq)

    x = jnp.concatenate([x_prompt.reshape(n_ctx, d), x_sample.reshape(n_lat, d)], axis=0)
    cond8 = jnp.zeros((16, d), F32).at[0].set(c_ctx).at[1:1 + n_lat_seq].set(c)
    new_f, new_b = [], []
    for l in range(DEPTH):
        wi = ffn_w_in[l]
        pad_ff = lambda w: jnp.pad(w, ((0, 0), (0, D_FF_PAD - D_FF)))
        n_ff_tiles = D_FF_PAD // D_FF_TILE

        def ffn_in_layout(w):
            a = pad_ff(w[:, :D_FF]).reshape(d, n_ff_tiles, D_FF_TILE)
            b = pad_ff(w[:, D_FF:]).reshape(d, n_ff_tiles, D_FF_TILE)
            return jnp.concatenate([a, b], axis=2).reshape(d, 2 * D_FF_PAD).astype(BF16)

        w1 = [ffn_in_layout(wi[s]) for s in range(2)]
        w2 = [jnp.pad(ffn_w_out[l, s], ((0, D_FF_PAD - D_FF), (0, 0))).astype(BF16) for s in range(2)]
        w_in_p = jnp.concatenate([w_in[l][:, :N_SHIFT], jnp.zeros((d, N_SHIFT_PAD - N_SHIFT), F32),
                                  w_in[l][:, N_SHIFT:]], axis=1).astype(BF16)
        mu_p = jnp.pad(shift_mu[l], (0, N_SHIFT_PAD - N_SHIFT))
        zpad = jnp.zeros((2, LANES - LORA_W, D_A), F32)
        w2p = jnp.concatenate([rw_w2[l], zpad], axis=1).astype(BF16)
        a2p = jnp.concatenate([zpad, rw_a2[l]], axis=1).astype(BF16)
        g2p = jnp.pad(rw_g2[l], ((0, LORA_G_PAD - LORA_G), (0, 0))).astype(BF16)
        bs_b = jnp.broadcast_to(sg_b[l][:, :, None], (G_B, CHUNK, LANES))

        mod = _ada(cond8, w_ada[l], b_ada[l]).reshape(16, 9, d)

        hid = _mm_mod(x, mod, w1[0], group_of, which=0, act=_act_swiglu, tm=tm, tn=2 * D_FF_TILE,
                      tn_out=D_FF_TILE, out_dtype=BF16)
        x = _mm_res_ln(hid, w2[0], x, mod, ln_g[l, 0], ln_b[l, 0], group_of, gate_row=2, coef=0.5, tm=tm, tk=D_FF_TILE)

        zs = _mm_mod(x, mod, w_in_p[:, :N_SHIFT_PAD], group_of, which=1, act=_act_none, tm=tm, tn=1152,
                     tn_out=1152, out_dtype=F32)
        uv = _mm_mod(x, mod, w_in_p[:, N_SHIFT_PAD:N_SHIFT_PAD + 2 * D_B], group_of, which=1, act=_act_gelu,
                     tm=tm, tn=1024, tn_out=1024, out_dtype=BF16)
        gts = _mm_mod(x, mod, w_in_p[:, N_SHIFT_PAD + 2 * D_B:], group_of, which=1, act=_act_sigmoid,
                      tm=tm, tn=1024, tn_out=1024, out_dtype=F32)
        zs = jnp.concatenate([
            _token_shift(zs, mu_p, row0=0, n_seq=n_ctx_seq, t_len=t_ctx, grid_w=None),
            _token_shift(zs, mu_p, row0=n_ctx, n_seq=n_lat_seq, t_len=t_lat, grid_w=GRID_W)], axis=0)
        prep = _rwkv_prep(zs, w2p, a2p, g2p, rw_w0[l], rw_a0[l], rw_k_k[l], rw_k_a[l], rw_r_k[l].reshape(D_A))
        scan_in = prep[:7]
        bonus, g_out = prep[7], prep[8]
        zero_state = jnp.zeros((n_ctx_seq, N_PAIR, PAIR, PAIR), F32)
        y_ctx, sf, sb = _rwkv_scan(zs, scan_in, zero_state, zero_state, row0=0, n_seq=n_ctx_seq, t_len=t_ctx)
        y_lat, _, _ = _rwkv_scan(zs, scan_in, _pair_states(state_rwkv_fwd[:, l]), _pair_states(state_rwkv_bwd[:, l]),
                                 row0=n_ctx, n_seq=n_lat_seq, t_len=t_lat)
        new_f.append(_unpair_states(sf))
        new_b.append(_unpair_states(sb))
        y = jnp.concatenate([y_ctx, y_lat], axis=0)
        yb = _gmlp(uv, sg_ln_g[l], sg_ln_b[l], sg_w[l].astype(BF16), bs_b)
        merged = _merge(y, bonus, g_out, rw_gn_g[l], rw_gn_b[l], yb, gts, w_pa[l].astype(BF16), w_pb[l].astype(BF16))
        x = _mm_res_ln(merged, w_o[l].astype(BF16), x, mod, ln_g[l, 1], ln_b[l, 1], group_of, gate_row=5, coef=1.0,
                       tm=tm, tk=512)

        hid = _mm_mod(x, mod, w1[1], group_of, which=2, act=_act_swiglu, tm=tm, tn=2 * D_FF_TILE,
                      tn_out=D_FF_TILE, out_dtype=BF16)
        x = _mm_res_ln(hid, w2[1], x, mod, ln_g[l, 2], ln_b[l, 2], group_of, gate_row=8, coef=0.5, tm=tm, tk=D_FF_TILE)

    y_prompt = x[:n_ctx].reshape(n_ctx_seq, t_ctx, d)
    y_sample = x[n_ctx:].reshape(n_lat_seq, t_lat, d)
    new_state_fwd = jnp.stack(new_f, axis=1).astype(x_prompt.dtype)
    new_state_bwd = jnp.stack(new_b, axis=1).astype(x_prompt.dtype)
    return (y_prompt, y_sample, new_state_fwd, new_state_bwd)
```

```python
import functools

import jax
import jax.numpy as jnp
from jax import lax
from jax.experimental import pallas as pl
from jax.experimental.pallas import tpu as pltpu

F32 = jnp.float32
BF16 = jnp.bfloat16

D_MODEL = 2048
DEPTH = 1
D_A = 1024
HEAD = 64
H_A = D_A // HEAD
LORA_W = 64
LORA_A = 64
LORA_G = 160
D_B = 1024
CHUNK = 128
G_B = 8
D_FF = 5504
N_SHIFT = 3 * D_A + LORA_W + LORA_A + LORA_G
GRID_W = 64
ALPHA = (2.0 * DEPTH) ** 0.25
LN_EPS = 1e-5
GN_EPS = 64e-5
NRM_EPS = 1e-12

LANES = 128
VMEM_LIMIT = 56 * 1024 * 1024

LORA_G_PAD = 256
N_SHIFT_PAD = 3 * D_A + LORA_W + LORA_A + LORA_G_PAD
D_FF_TILE = 512
D_FF_PAD = ((D_FF + D_FF_TILE - 1) // D_FF_TILE) * D_FF_TILE
SCAN_C = 64
SCAN_TB = 256
PAIR = 2 * HEAD
N_PAIR = H_A // 2


def _cparams(sem):
    return pltpu.CompilerParams(dimension_semantics=sem, vmem_limit_bytes=VMEM_LIMIT)


def _dot(a, b):
    return jnp.dot(a, b, preferred_element_type=F32)


def _dot_nt(a, b):
    return lax.dot_general(a, b, (((1,), (1,)), ((), ())), preferred_element_type=F32)


def _dot2(x, m):
    hi = x.astype(BF16)
    lo = (x - hi.astype(F32)).astype(BF16)
    return _dot(hi, m) + _dot(lo, m)


def _ada_kernel(c_ref, w_ref, b_ref, o_ref):
    c = c_ref[...]
    s = (c * jax.nn.sigmoid(c)).astype(BF16)
    o_ref[...] = _dot(s, w_ref[...].astype(BF16)) + b_ref[...]


def _ada(cond8, w_ada, b_ada, tn=1024):
    rows, d = cond8.shape
    n = w_ada.shape[1]
    return pl.pallas_call(
        _ada_kernel,
        grid=(n // tn,),
        in_specs=[pl.BlockSpec((rows, d), lambda j: (0, 0)),
                  pl.BlockSpec((d, tn), lambda j: (0, j)),
                  pl.BlockSpec((1, tn), lambda j: (0, j))],
        out_specs=pl.BlockSpec((rows, tn), lambda j: (0, j)),
        out_shape=jax.ShapeDtypeStruct((rows, n), F32),
        compiler_params=_cparams(("parallel",)),
        name="ada",
    )(cond8, w_ada, b_ada.reshape(1, n))


def _act_none(acc):
    return acc


def _act_swiglu(acc):
    half = acc.shape[1] // 2
    a = acc[:, :half]
    return (a * jax.nn.sigmoid(a)) * acc[:, half:]


def _act_gelu(acc):
    return 0.5 * acc * (1.0 + lax.erf(acc * (2.0 ** -0.5)))


def _act_sigmoid(acc):
    return jax.nn.sigmoid(acc)


def _mm_mod_kernel(x_ref, mod_ref, w_ref, o_ref, h_ref, *, which, act):
    @pl.when(pl.program_id(1) == 0)
    def _():
        shift = mod_ref[0, 3 * which:3 * which + 1, :]
        scale = mod_ref[0, 3 * which + 1:3 * which + 2, :]
        h_ref[...] = (x_ref[...] * (1.0 + scale) + shift).astype(BF16)

    o_ref[...] = act(_dot(h_ref[...], w_ref[...])).astype(o_ref.dtype)


def _mm_mod(x, mod3, w, group_of, *, which, act, tm, tn, tn_out, out_dtype):
    n, d = x.shape
    nj = w.shape[1] // tn
    return pl.pallas_call(
        functools.partial(_mm_mod_kernel, which=which, act=act),
        grid=(n // tm, nj),
        in_specs=[pl.BlockSpec((tm, d), lambda i, j: (i, 0)),
                  pl.BlockSpec((1, 9, d), lambda i, j: (group_of(i), 0, 0)),
                  pl.BlockSpec((d, tn), lambda i, j: (0, j))],
        out_specs=pl.BlockSpec((tm, tn_out), lambda i, j: (i, j)),
        out_shape=jax.ShapeDtypeStruct((n, nj * tn_out), out_dtype),
        scratch_shapes=[pltpu.VMEM((tm, d), BF16)],
        compiler_params=_cparams(("parallel", "arbitrary")),
        name="mm_mod",
    )(x, mod3, w)


def _mm_res_ln_kernel(a_ref, w_ref, x_ref, mod_ref, g_ref, b_ref, o_ref, *, gate_row, coef, nk):
    k = pl.program_id(1)
    part = _dot(a_ref[...], w_ref[...])

    @pl.when(k == 0)
    def _():
        o_ref[...] = part

    @pl.when(k > 0)
    def _():
        o_ref[...] += part

    @pl.when(k == nk - 1)
    def _():
        gate = mod_ref[0, gate_row:gate_row + 1, :]
        y = ALPHA * x_ref[...] + (coef * gate) * o_ref[...]
        mu = jnp.mean(y, axis=-1, keepdims=True)
        yc = y - mu
        var = jnp.mean(yc * yc, axis=-1, keepdims=True)
        o_ref[...] = yc * lax.rsqrt(var + LN_EPS) * g_ref[...] + b_ref[...]


def _mm_res_ln(a, w, x, mod3, ln_g, ln_b, group_of, *, gate_row, coef, tm, tk):
    n, kdim = a.shape
    d = w.shape[1]
    nk = kdim // tk
    return pl.pallas_call(
        functools.partial(_mm_res_ln_kernel, gate_row=gate_row, coef=coef, nk=nk),
        grid=(n // tm, nk),
        in_specs=[pl.BlockSpec((tm, tk), lambda i, k: (i, k)),
                  pl.BlockSpec((tk, d), lambda i, k: (k, 0)),
                  pl.BlockSpec((tm, d), lambda i, k: (i, 0)),
                  pl.BlockSpec((1, 9, d), lambda i, k: (group_of(i), 0, 0)),
                  pl.BlockSpec((1, d), lambda i, k: (0, 0)),
                  pl.BlockSpec((1, d), lambda i, k: (0, 0))],
        out_specs=pl.BlockSpec((tm, d), lambda i, k: (i, 0)),
        out_shape=jax.ShapeDtypeStruct((n, d), F32),
        compiler_params=_cparams(("parallel", "arbitrary")),
        name="mm_res_ln",
    )(a, w, x, mod3, ln_g.reshape(1, d), ln_b.reshape(1, d))


def _shift_kernel(z_ref, mu_ref, o_ref, *, grid_w):
    z = z_ref[...]
    t_len = z.shape[0]
    t = lax.broadcasted_iota(jnp.int32, (t_len, 1), 0)

    def from_offset(off):
        rolled = pltpu.roll(z, (-off) % t_len, 0)
        ok = jnp.logical_and(t + off >= 0, t + off < t_len)
        return rolled, ok

    if grid_w is None:
        up, up_ok = from_offset(-1)
        dn, dn_ok = from_offset(1)
        nb = (jnp.where(up_ok, up, 0.0) + jnp.where(dn_ok, dn, 0.0)) * 0.5
    else:
        col = jnp.bitwise_and(t, grid_w - 1)
        above, above_ok = from_offset(-grid_w)
        below, below_ok = from_offset(grid_w)
        left, left_ok = from_offset(-1)
        right, right_ok = from_offset(1)
        left_ok = jnp.logical_and(left_ok, col > 0)
        right_ok = jnp.logical_and(right_ok, col < grid_w - 1)
        nb = (jnp.where(above_ok, above, 0.0) + jnp.where(below_ok, below, 0.0)
              + jnp.where(left_ok, left, 0.0) + jnp.where(right_ok, right, 0.0)) * 0.25
    o_ref[...] = z + mu_ref[...] * (nb - z)


def _token_shift(zs, mu, *, row0, n_seq, t_len, grid_w, tc=384):
    ncol = zs.shape[1]
    blk0 = row0 // t_len
    return pl.pallas_call(
        functools.partial(_shift_kernel, grid_w=grid_w),
        grid=(n_seq, ncol // tc),
        in_specs=[pl.BlockSpec((t_len, tc), lambda b, j: (blk0 + b, j)),
                  pl.BlockSpec((1, tc), lambda b, j: (0, j))],
        out_specs=pl.BlockSpec((t_len, tc), lambda b, j: (b, j)),
        out_shape=jax.ShapeDtypeStruct((n_seq * t_len, ncol), F32),
        compiler_params=_cparams(("parallel", "parallel")),
        name="token_shift",
    )(zs, mu.reshape(1, ncol))


def _prep_kernel(z_ref, w2_ref, a2_ref, g2_ref, w0_ref, a0_ref, kk_ref_, ka_ref, rk_ref, seg_ref,
                 kk_o, lw0_o, kd0_o, bb0_o, lw1_o, kd1_o, bb1_o, bonus_o, g_o):
    r = z_ref[:, 0:D_A]
    k = z_ref[:, D_A:2 * D_A]
    v = z_ref[:, 2 * D_A:3 * D_A]
    xwa = z_ref[:, 3 * D_A:3 * D_A + LANES]
    xg = z_ref[:, 3 * D_A + LANES:3 * D_A + LANES + LORA_G_PAD]
    seg = seg_ref[...]

    g_o[...] = _dot(jax.nn.sigmoid(xg).astype(BF16), g2_ref[...])

    kk = k * kk_ref_[...]
    ksq = kk * kk
    ss = jnp.concatenate([_dot2(ksq[:, p * LANES:(p + 1) * LANES], seg) for p in range(N_PAIR)], axis=1)
    kk = kk / jnp.maximum(jnp.sqrt(ss), NRM_EPS)
    kk_o[...] = kk

    tw = jnp.tanh(xwa).astype(BF16)
    xa = xwa.astype(BF16)
    k_a = ka_ref[...]
    r_k = rk_ref[...]
    outs = ((lw0_o, kd0_o, bb0_o), (lw1_o, kd1_o, bb1_o))
    rkd = jnp.zeros_like(r)
    for d in range(2):
        lw_o, kd_o, bb_o = outs[d]
        wl = w0_ref[d:d + 1, :] + _dot(tw, w2_ref[d])
        wlog = -(jnp.maximum(-wl, 0.0) + jnp.log1p(jnp.exp(-jnp.abs(wl)))) - 0.5
        lw_o[...] = -jnp.exp(wlog)
        a = jax.nn.sigmoid(a0_ref[d:d + 1, :] + _dot(xa, a2_ref[d]))
        kd = k * (1.0 + (a - 1.0) * k_a)
        kd_o[...] = kd
        bb_o[...] = kk * a
        rkd = rkd + r * kd * r_k
    rs = jnp.concatenate([_dot2(rkd[:, p * LANES:(p + 1) * LANES], seg) for p in range(N_PAIR)], axis=1)
    bonus_o[...] = rs * v


def _rwkv_prep(zs, w2p, a2p, g2p, w0, a0, k_k, k_a, r_k, *, tm=256):
    n = zs.shape[0]
    head_of_lane = jnp.arange(LANES) // HEAD
    seg = (head_of_lane[:, None] == head_of_lane[None, :]).astype(BF16)
    full = lambda shape: pl.BlockSpec(shape, lambda i: (0,) * len(shape))
    row = pl.BlockSpec((tm, D_A), lambda i: (i, 0))
    out = jax.ShapeDtypeStruct((n, D_A), F32)
    return pl.pallas_call(
        _prep_kernel,
        grid=(n // tm,),
        in_specs=[pl.BlockSpec((tm, N_SHIFT_PAD), lambda i: (i, 0)),
                  full((2, LANES, D_A)), full((2, LANES, D_A)), full((LORA_G_PAD, D_A)),
                  full((2, D_A)), full((2, D_A)), full((1, D_A)), full((1, D_A)), full((1, D_A)),
                  full((LANES, LANES))],
        out_specs=[row] * 9,
        out_shape=[out] * 9,
        compiler_params=_cparams(("parallel",)),
        name="rwkv_prep",
    )(zs, w2p, a2p, g2p, w0, a0, k_k.reshape(1, D_A), k_a.reshape(1, D_A), r_k.reshape(1, D_A), seg)


def _blockdiag(x, lane_a):
    return jnp.concatenate([jnp.where(lane_a, x, 0.0), jnp.where(lane_a, 0.0, x)], axis=0)


def _pair_mm(a, b, lane_a):
    return _dot(a.astype(BF16), _blockdiag(b, lane_a).astype(BF16))


def _scan_chunks(chains):
    cs = SCAN_C
    ids = range(len(chains))
    lane_a = chains[0][6][0]
    bd_mask = chains[0][6][6]
    rows = [pl.ds(pl.multiple_of(ch[0] * cs, cs), cs) for ch in chains]
    lanes = [slice(ch[2] * PAIR, (ch[2] + 1) * PAIR) for ch in chains]
    load = lambda k: [chains[i][3][k][rows[i], lanes[i]] for i in ids]
    r, v, kk, lw, kd, bb = (load(k) for k in range(6))
    tri2, strict, incl, eye2, level_masks = ([chains[i][6][k] for i in ids] for k in range(1, 6))

    def split2(x):
        hi = x.astype(BF16)
        return jnp.concatenate([hi, (x - hi.astype(F32)).astype(BF16)], axis=0)

    lp = [_dot(tri2[i], split2(lw[i])) for i in ids]
    total = [lp[i][0:1, :] if chains[i][1] else lp[i][cs - 1:cs, :] for i in ids]
    p_inv = [jnp.exp(-lp[i]) for i in ids]
    lhs = [jnp.concatenate([kk[i] * jnp.exp(lp[i] - lw[i]), r[i] * jnp.exp(lp[i])], axis=0).astype(BF16)
           for i in ids]
    rhs = [jnp.concatenate([_blockdiag(bb[i] * p_inv[i], lane_a), _blockdiag(kd[i] * p_inv[i], lane_a)],
                           axis=0).astype(BF16) for i in ids]
    gram = [_dot_nt(lhs[i], rhs[i]) for i in ids]
    l_beta = [jnp.where(strict[i], gram[i][0:cs, 0:PAIR], 0.0) for i in ids]
    l_kappa = [jnp.where(strict[i], gram[i][0:cs, PAIR:2 * PAIR], 0.0).astype(BF16) for i in ids]
    m_both = [jnp.concatenate([jnp.where(incl[i], gram[i][cs:2 * cs, PAIR:2 * PAIR], 0.0),
                               jnp.where(incl[i], -gram[i][cs:2 * cs, 0:PAIR], 0.0)], axis=1).astype(BF16)
              for i in ids]

    dinv = [eye2[i] - jnp.where(level_masks[i][0], l_beta[i], 0.0) for i in ids]
    for lvl in range(1, len(level_masks[0])):
        x = [_pair_mm(jnp.where(level_masks[i][lvl], l_beta[i], 0.0), dinv[i], lane_a) for i in ids]
        dinv = [dinv[i] - _pair_mm(dinv[i], x[i], lane_a) for i in ids]

    s0 = [chains[i][4][chains[i][2]] for i in ids]
    from_state = [_dot_nt(lhs[i], s0[i].astype(BF16)) for i in ids]
    v_bd = [_blockdiag(v[i], lane_a).astype(BF16) for i in ids]
    w_rhs = [from_state[i][0:cs] + _dot(l_kappa[i], v_bd[i]) for i in ids]
    u = [_pair_mm(dinv[i], w_rhs[i], lane_a) for i in ids]
    y = [from_state[i][cs:2 * cs]
         + _dot(m_both[i], jnp.concatenate([v_bd[i], _blockdiag(u[i], lane_a).astype(BF16)], axis=0)) for i in ids]
    for i in ids:
        chains[i][5][rows[i], lanes[i]] = y[i]

    to_end = [jnp.exp(total[i] - lp[i]) for i in ids]
    zt = [jnp.concatenate([v[i], -u[i]], axis=0).T.astype(BF16) for i in ids]
    kb_end = [jnp.concatenate([kd[i] * to_end[i], bb[i] * to_end[i]], axis=0).astype(BF16) for i in ids]
    upd = [_dot(zt[i], kb_end[i]) for i in ids]
    for i in ids:
        chains[i][4][chains[i][2]] = s0[i] * jnp.exp(total[i]) + jnp.where(bd_mask, upd[i], 0.0)


def _scan_consts():
    cs = SCAN_C
    lane = lax.broadcasted_iota(jnp.int32, (cs, PAIR), 1)
    t = lax.broadcasted_iota(jnp.int32, (cs, PAIR), 0)
    i = jnp.bitwise_and(lane, HEAD - 1)
    lane_a = lane < HEAD
    out = {}
    for reverse in (False, True):
        before = (i > t) if reverse else (i < t)
        strict = before
        incl = jnp.logical_or(before, i == t)
        tri2 = incl.astype(BF16)
        eye2 = (i == t).astype(F32)
        level_masks = []
        s = 1
        while s < cs:
            sh_s = s.bit_length() - 1
            same_2s = lax.shift_right_logical(t, sh_s + 1) == lax.shift_right_logical(i, sh_s + 1)
            diff_s = lax.shift_right_logical(t, sh_s) != lax.shift_right_logical(i, sh_s)
            level_masks.append(jnp.logical_and(jnp.logical_and(same_2s, diff_s), strict))
            s *= 2
        out[reverse] = (tri2, strict, incl, eye2, tuple(level_masks))
    row = lax.broadcasted_iota(jnp.int32, (PAIR, PAIR), 0)
    col = lax.broadcasted_iota(jnp.int32, (PAIR, PAIR), 1)
    bd_mask = (row < HEAD) == (col < HEAD)
    return lane_a, out, bd_mask


def _scan_kernel(rf, vf, kkf, lw0, kd0, bb0, rb, vb, kkb, lw1, kd1, bb1, s0f_ref, s0b_ref,
                 yf_ref, yb_ref, sf_ref, sb_ref, sf_scr, sb_scr, *, n_tb, pairs_per_body):
    j = pl.program_id(1)
    n_chunk = SCAN_TB // SCAN_C
    lane_a, per_dir, bd_mask = _scan_consts()
    consts_f = (lane_a,) + per_dir[False] + (bd_mask,)
    consts_b = (lane_a,) + per_dir[True] + (bd_mask,)

    @pl.when(j == 0)
    def _():
        sf_scr[...] = s0f_ref[0]
        sb_scr[...] = s0b_ref[0]

    for p0 in range(0, N_PAIR, pairs_per_body):
        def body(c, carry, p0=p0):
            chains = []
            for p in range(p0, p0 + pairs_per_body):
                chains.append((c, False, p, (rf, vf, kkf, lw0, kd0, bb0), sf_scr, yf_ref, consts_f))
                chains.append((n_chunk - 1 - c, True, p, (rb, vb, kkb, lw1, kd1, bb1), sb_scr, yb_ref, consts_b))
            _scan_chunks(chains)
            return carry

        lax.fori_loop(0, n_chunk, body, 0)

    @pl.when(j == n_tb - 1)
    def _():
        sf_ref[0] = sf_scr[...]
        sb_ref[0] = sb_scr[...]


def _rwkv_scan(zs, prep, s0f, s0b, *, row0, n_seq, t_len, pairs_per_body=8):
    kk, lw0, kd0, bb0, lw1, kd1, bb1 = prep
    n_tb = t_len // SCAN_TB
    blk0 = row0 // SCAN_TB
    v_col = 2
    fwd = lambda col: pl.BlockSpec((SCAN_TB, D_A), lambda b, j: (blk0 + b * n_tb + j, col))
    bwd = lambda col: pl.BlockSpec((SCAN_TB, D_A), lambda b, j: (blk0 + b * n_tb + (n_tb - 1 - j), col))
    st = pl.BlockSpec((1, N_PAIR, PAIR, PAIR), lambda b, j: (b, 0, 0, 0))
    st_shape = jax.ShapeDtypeStruct((n_seq, N_PAIR, PAIR, PAIR), F32)
    y_shape = jax.ShapeDtypeStruct((n_seq * t_len, D_A), F32)
    return pl.pallas_call(
        functools.partial(_scan_kernel, n_tb=n_tb, pairs_per_body=pairs_per_body),
        grid=(n_seq, n_tb),
        in_specs=[fwd(0), fwd(v_col)] + [fwd(0)] * 4 + [bwd(0), bwd(v_col)] + [bwd(0)] * 4 + [st, st],
        out_specs=[pl.BlockSpec((SCAN_TB, D_A), lambda b, j: (b * n_tb + j, 0)),
                   pl.BlockSpec((SCAN_TB, D_A), lambda b, j: (b * n_tb + (n_tb - 1 - j), 0)), st, st],
        out_shape=[y_shape, y_shape, st_shape, st_shape],
        scratch_shapes=[pltpu.VMEM((N_PAIR, PAIR, PAIR), F32), pltpu.VMEM((N_PAIR, PAIR, PAIR), F32)],
        compiler_params=_cparams(("parallel", "arbitrary")),
        name="rwkv_scan",
    )(zs, zs, kk, lw0, kd0, bb0, zs, zs, kk, lw1, kd1, bb1, s0f, s0b)


def _gmlp_kernel(uv_ref, lg_ref, lb_ref, ws_ref, bs_ref, o_ref, *, n_chunk):
    u = uv_ref[:, 0:D_B].astype(F32)
    v = uv_ref[:, D_B:2 * D_B].astype(F32)
    mu = jnp.mean(v, axis=-1, keepdims=True)
    vc = v - mu
    var = jnp.mean(vc * vc, axis=-1, keepdims=True)
    vn = ((vc * lax.rsqrt(var + LN_EPS)) * lg_ref[...] + lb_ref[...]).astype(BF16)
    for c in range(n_chunk):
        rows = slice(c * CHUNK, (c + 1) * CHUNK)
        for g in range(G_B):
            cols = slice(g * LANES, (g + 1) * LANES)
            s = _dot(ws_ref[g], vn[rows, cols]) + bs_ref[g]
            o_ref[rows, cols] = (u[rows, cols] * s).astype(o_ref.dtype)


def _gmlp(uv, ln_g, ln_b, ws, bs_b, *, n_chunk=4):
    n = uv.shape[0]
    tm = n_chunk * CHUNK
    full = lambda shape: pl.BlockSpec(shape, lambda i: (0,) * len(shape))
    return pl.pallas_call(
        functools.partial(_gmlp_kernel, n_chunk=n_chunk),
        grid=(n // tm,),
        in_specs=[pl.BlockSpec((tm, 2 * D_B), lambda i: (i, 0)),
                  full((1, D_B)), full((1, D_B)), full((G_B, CHUNK, CHUNK)), full((G_B, CHUNK, LANES))],
        out_specs=pl.BlockSpec((tm, D_B), lambda i: (i, 0)),
        out_shape=jax.ShapeDtypeStruct((n, D_B), BF16),
        compiler_params=_cparams(("parallel",)),
        name="gmlp",
    )(uv, ln_g.reshape(1, D_B), ln_b.reshape(1, D_B), ws, bs_b)


def _merge_kernel(yf_ref, yr_ref, bonus_ref, g_ref, gng_ref, gnb_ref, avg_ref, yb_ref, gts_ref, wpa_ref, wpb_ref, o_ref):
    avg = avg_ref[...]
    parts = []
    for p in range(N_PAIR):
        cols = slice(p * LANES, (p + 1) * LANES)
        yp = yf_ref[:, cols] + yr_ref[:, cols]
        yc = yp - _dot2(yp, avg)
        var = _dot2(yc * yc, avg)
        parts.append(yc * lax.rsqrt(var + GN_EPS))
    yn = jnp.concatenate(parts, axis=1)
    oa = ((yn * gng_ref[...] + gnb_ref[...] + bonus_ref[...]) * g_ref[...]).astype(BF16)
    o_ref[...] = (gts_ref[:, 0:D_MODEL] * _dot(oa, wpa_ref[...])
                  + gts_ref[:, D_MODEL:2 * D_MODEL] * _dot(yb_ref[...], wpb_ref[...])).astype(o_ref.dtype)


def _merge(y_fwd, y_bwd, bonus, g, gn_g, gn_b, yb, gts, w_pa, w_pb, *, tm=256):
    n = y_fwd.shape[0]
    head_of_lane = jnp.arange(LANES) // HEAD
    avg = ((head_of_lane[:, None] == head_of_lane[None, :]).astype(F32) / HEAD).astype(BF16)
    full = lambda shape: pl.BlockSpec(shape, lambda i: (0,) * len(shape))
    row = lambda w: pl.BlockSpec((tm, w), lambda i: (i, 0))
    return pl.pallas_call(
        _merge_kernel,
        grid=(n // tm,),
        in_specs=[row(D_A), row(D_A), row(D_A), row(D_A), full((1, D_A)), full((1, D_A)), full((LANES, LANES)),
                  row(D_B), row(2 * D_MODEL), full((D_A, D_MODEL)), full((D_B, D_MODEL))],
        out_specs=row(D_MODEL),
        out_shape=jax.ShapeDtypeStruct((n, D_MODEL), BF16),
        compiler_params=_cparams(("parallel",)),
        name="merge",
    )(y_fwd, y_bwd, bonus, g, gn_g.reshape(1, D_A), gn_b.reshape(1, D_A), avg, yb, gts, w_pa, w_pb)


def _pair_states(s):
    b = s.shape[0]
    s = s.reshape(b, N_PAIR, 2, HEAD, HEAD)
    z = jnp.zeros_like(s[:, :, 0])
    top = jnp.concatenate([s[:, :, 0], z], axis=-1)
    bot = jnp.concatenate([z, s[:, :, 1]], axis=-1)
    return jnp.concatenate([top, bot], axis=-2)


def _unpair_states(sp):
    b = sp.shape[0]
    return jnp.stack([sp[:, :, :HEAD, :HEAD], sp[:, :, HEAD:, HEAD:]], axis=2).reshape(b, H_A, HEAD, HEAD)


def kernel(x_prompt, x_sample, c, state_rwkv_fwd, state_rwkv_bwd, c_ctx, w_ada, b_ada, ln_g, ln_b,
           ffn_w_in, ffn_w_out, w_in, shift_mu, rw_w0, rw_w2, rw_a0, rw_a2, rw_g2, rw_k_k, rw_k_a,
           rw_r_k, rw_gn_g, rw_gn_b, sg_ln_g, sg_ln_b, sg_w, sg_b, w_pa, w_pb, w_o):
    n_ctx_seq, t_ctx, d = x_prompt.shape
    n_lat_seq, t_lat, _ = x_sample.shape
    n_ctx = n_ctx_seq * t_ctx
    n_lat = n_lat_seq * t_lat
    tm = 1024
    ctx_tiles = n_ctx // tm
    lat_tiles_per_seq = t_lat // tm

    def group_of(i):
        return jnp.where(i < ctx_tiles, 0, 1 + (i - ctx_tiles) // lat_tiles_per_seq)

    x = jnp.concatenate([x_prompt.reshape(n_ctx, d), x_sample.reshape(n_lat, d)], axis=0)
    cond8 = jnp.zeros((16, d), F32).at[0].set(c_ctx).at[1:1 + n_lat_seq].set(c)
    new_f, new_b = [], []
    for l in range(DEPTH):
        wi = ffn_w_in[l]
        pad_ff = lambda w: jnp.pad(w, ((0, 0), (0, D_FF_PAD - D_FF)))
        n_ff_tiles = D_FF_PAD // D_FF_TILE

        def ffn_in_layout(w):
            a = pad_ff(w[:, :D_FF]).reshape(d, n_ff_tiles, D_FF_TILE)
            b = pad_ff(w[:, D_FF:]).reshape(d, n_ff_tiles, D_FF_TILE)
            return jnp.concatenate([a, b], axis=2).reshape(d, 2 * D_FF_PAD).astype(BF16)

        w1 = [ffn_in_layout(wi[s]) for s in range(2)]
        w2 = [jnp.pad(ffn_w_out[l, s], ((0, D_FF_PAD - D_FF), (0, 0))).astype(BF16) for s in range(2)]
        w_in_p = jnp.concatenate([w_in[l][:, :N_SHIFT], jnp.zeros((d, N_SHIFT_PAD - N_SHIFT), F32),
                                  w_in[l][:, N_SHIFT:]], axis=1).astype(BF16)
        mu_p = jnp.pad(shift_mu[l], (0, N_SHIFT_PAD - N_SHIFT))
        zpad = jnp.zeros((2, LANES - LORA_W, D_A), F32)
        w2p = jnp.concatenate([rw_w2[l], zpad], axis=1).astype(BF16)
        a2p = jnp.concatenate([zpad, rw_a2[l]], axis=1).astype(BF16)
        g2p = jnp.pad(rw_g2[l], ((0, LORA_G_PAD - LORA_G), (0, 0))).astype(BF16)
        bs_b = jnp.broadcast_to(sg_b[l][:, :, None], (G_B, CHUNK, LANES))

        mod = _ada(cond8, w_ada[l], b_ada[l]).reshape(16, 9, d)

        hid = _mm_mod(x, mod, w1[0], group_of, which=0, act=_act_swiglu, tm=tm, tn=2 * D_FF_TILE,
                      tn_out=D_FF_TILE, out_dtype=BF16)
        x = _mm_res_ln(hid, w2[0], x, mod, ln_g[l, 0], ln_b[l, 0], group_of, gate_row=2, coef=0.5, tm=tm, tk=D_FF_TILE)

        zs = _mm_mod(x, mod, w_in_p[:, :N_SHIFT_PAD], group_of, which=1, act=_act_none, tm=tm, tn=1152,
                     tn_out=1152, out_dtype=F32)
        uv = _mm_mod(x, mod, w_in_p[:, N_SHIFT_PAD:N_SHIFT_PAD + 2 * D_B], group_of, which=1, act=_act_gelu,
                     tm=tm, tn=1024, tn_out=1024, out_dtype=BF16)
        gts = _mm_mod(x, mod, w_in_p[:, N_SHIFT_PAD + 2 * D_B:], group_of, which=1, act=_act_sigmoid,
                      tm=tm, tn=1024, tn_out=1024, out_dtype=F32)
        zs = jnp.concatenate([
            _token_shift(zs, mu_p, row0=0, n_seq=n_ctx_seq, t_len=t_ctx, grid_w=None),
            _token_shift(zs, mu_p, row0=n_ctx, n_seq=n_lat_seq, t_len=t_lat, grid_w=GRID_W)], axis=0)
        prep = _rwkv_prep(zs, w2p, a2p, g2p, rw_w0[l], rw_a0[l], rw_k_k[l], rw_k_a[l], rw_r_k[l].reshape(D_A))
        scan_in = prep[:7]
        bonus, g_out = prep[7], prep[8]
        zero_state = jnp.zeros((n_ctx_seq, N_PAIR, PAIR, PAIR), F32)
        yf_ctx, yb_ctx, sf, sb = _rwkv_scan(zs, scan_in, zero_state, zero_state, row0=0, n_seq=n_ctx_seq, t_len=t_ctx)
        yf_lat, yb_lat, _, _ = _rwkv_scan(zs, scan_in, _pair_states(state_rwkv_fwd[:, l]), _pair_states(state_rwkv_bwd[:, l]),
                                 row0=n_ctx, n_seq=n_lat_seq, t_len=t_lat)
        new_f.append(_unpair_states(sf))
        new_b.append(_unpair_states(sb))
        y_fwd = jnp.concatenate([yf_ctx, yf_lat], axis=0)
        y_bwd = jnp.concatenate([yb_ctx, yb_lat], axis=0)
        yb = _gmlp(uv, sg_ln_g[l], sg_ln_b[l], sg_w[l].astype(BF16), bs_b)
        merged = _merge(y_fwd, y_bwd, bonus, g_out, rw_gn_g[l], rw_gn_b[l], yb, gts, w_pa[l].astype(BF16), w_pb[l].astype(BF16))
        x = _mm_res_ln(merged, w_o[l].astype(BF16), x, mod, ln_g[l, 1], ln_b[l, 1], group_of, gate_row=5, coef=1.0,
                       tm=tm, tk=512)

        hid = _mm_mod(x, mod, w1[1], group_of, which=2, act=_act_swiglu, tm=tm, tn=2 * D_FF_TILE,
                      tn_out=D_FF_TILE, out_dtype=BF16)
        x = _mm_res_ln(hid, w2[1], x, mod, ln_g[l, 2], ln_b[l, 2], group_of, gate_row=8, coef=0.5, tm=tm, tk=D_FF_TILE)

    y_prompt = x[:n_ctx].reshape(n_ctx_seq, t_ctx, d)
    y_sample = x[n_ctx:].reshape(n_lat_seq, t_lat, d)
    new_state_fwd = jnp.stack(new_f, axis=1).astype(x_prompt.dtype)
    new_state_bwd = jnp.stack(new_b, axis=1).astype(x_prompt.dtype)
    return (y_prompt, y_sample, new_state_fwd, new_state_bwd)
```

```python
import functools

import jax
import jax.numpy as jnp
from jax import lax
from jax.experimental import pallas as pl
from jax.experimental.pallas import tpu as pltpu

F32 = jnp.float32
BF16 = jnp.bfloat16

D_MODEL = 2048
DEPTH = 1
D_A = 1024
HEAD = 64
H_A = D_A // HEAD
LORA_W = 64
LORA_A = 64
LORA_G = 160
D_B = 1024
CHUNK = 128
G_B = 8
D_FF = 5504
N_SHIFT = 3 * D_A + LORA_W + LORA_A + LORA_G
GRID_W = 64
ALPHA = (2.0 * DEPTH) ** 0.25
LN_EPS = 1e-5
GN_EPS = 64e-5
NRM_EPS = 1e-12

LANES = 128
VMEM_LIMIT = 56 * 1024 * 1024

LORA_G_PAD = 256
N_SHIFT_PAD = 3 * D_A + LORA_W + LORA_A + LORA_G_PAD
D_FF_TILE = 512
D_FF_PAD = ((D_FF + D_FF_TILE - 1) // D_FF_TILE) * D_FF_TILE
FFN_OUT_CHUNK = 512
LN_ROWS = 256
SCAN_C = 64
SCAN_TB = 256
PAIR = 2 * HEAD
N_PAIR = H_A // 2


def _cparams(sem):
    return pltpu.CompilerParams(dimension_semantics=sem, vmem_limit_bytes=VMEM_LIMIT)


def _dot(a, b):
    return jnp.dot(a, b, preferred_element_type=F32)


def _dot_nt(a, b):
    return lax.dot_general(a, b, (((1,), (1,)), ((), ())), preferred_element_type=F32)


def _dot2(x, m):
    hi = x.astype(BF16)
    lo = (x - hi.astype(F32)).astype(BF16)
    return _dot(hi, m) + _dot(lo, m)


def _ada_kernel(c_ref, w_ref, b_ref, o_ref):
    c = c_ref[...]
    s = (c * jax.nn.sigmoid(c)).astype(BF16)
    o_ref[...] = _dot(s, w_ref[...].astype(BF16)) + b_ref[...]


def _ada(cond8, w_ada, b_ada, tn=1024):
    rows, d = cond8.shape
    n = w_ada.shape[1]
    return pl.pallas_call(
        _ada_kernel,
        grid=(n // tn,),
        in_specs=[pl.BlockSpec((rows, d), lambda j: (0, 0)),
                  pl.BlockSpec((d, tn), lambda j: (0, j)),
                  pl.BlockSpec((1, tn), lambda j: (0, j))],
        out_specs=pl.BlockSpec((rows, tn), lambda j: (0, j)),
        out_shape=jax.ShapeDtypeStruct((rows, n), F32),
        compiler_params=_cparams(("parallel",)),
        name="ada",
    )(cond8, w_ada, b_ada.reshape(1, n))


def _act_none(acc):
    return acc


def _act_swiglu(acc):
    half = acc.shape[1] // 2
    a = acc[:, :half]
    return (a * jax.nn.sigmoid(a)) * acc[:, half:]


def _act_gelu(acc):
    return 0.5 * acc * (1.0 + lax.erf(acc * (2.0 ** -0.5)))


def _act_sigmoid(acc):
    return jax.nn.sigmoid(acc)


def _mm_mod_kernel(x_ref, mod_ref, w_ref, o_ref, h_ref, *, which, act):
    @pl.when(pl.program_id(1) == 0)
    def _():
        shift = mod_ref[0, 3 * which:3 * which + 1, :]
        scale = mod_ref[0, 3 * which + 1:3 * which + 2, :]
        h_ref[...] = (x_ref[...] * (1.0 + scale) + shift).astype(BF16)

    o_ref[...] = act(_dot(h_ref[...], w_ref[...])).astype(o_ref.dtype)


def _mm_mod(x, mod3, w, group_of, *, which, act, tm, tn, tn_out, out_dtype):
    n, d = x.shape
    nj = w.shape[1] // tn
    return pl.pallas_call(
        functools.partial(_mm_mod_kernel, which=which, act=act),
        grid=(n // tm, nj),
        in_specs=[pl.BlockSpec((tm, d), lambda i, j: (i, 0)),
                  pl.BlockSpec((1, 9, d), lambda i, j: (group_of(i), 0, 0)),
                  pl.BlockSpec((d, tn), lambda i, j: (0, j))],
        out_specs=pl.BlockSpec((tm, tn_out), lambda i, j: (i, j)),
        out_shape=jax.ShapeDtypeStruct((n, nj * tn_out), out_dtype),
        scratch_shapes=[pltpu.VMEM((tm, d), BF16)],
        compiler_params=_cparams(("parallel", "arbitrary")),
        name="mm_mod",
    )(x, mod3, w)


def _mm_res_ln_kernel(a_ref, w_ref, x_ref, mod_ref, g_ref, b_ref, o_ref, *, gate_row, coef, nk):
    k = pl.program_id(1)
    part = _dot(a_ref[...], w_ref[...])

    @pl.when(k == 0)
    def _():
        o_ref[...] = part

    @pl.when(k > 0)
    def _():
        o_ref[...] += part

    @pl.when(k == nk - 1)
    def _():
        gate = coef * mod_ref[0, gate_row:gate_row + 1, :]
        for r0 in range(0, o_ref.shape[0], LN_ROWS):
            rows = slice(r0, r0 + LN_ROWS)
            o_ref[rows, :] = _res_ln(x_ref[rows, :], gate * o_ref[rows, :], g_ref[...], b_ref[...])


def _mm_res_ln(a, w, x, mod3, ln_g, ln_b, group_of, *, gate_row, coef, tm, tk):
    n, kdim = a.shape
    d = w.shape[1]
    nk = kdim // tk
    return pl.pallas_call(
        functools.partial(_mm_res_ln_kernel, gate_row=gate_row, coef=coef, nk=nk),
        grid=(n // tm, nk),
        in_specs=[pl.BlockSpec((tm, tk), lambda i, k: (i, k)),
                  pl.BlockSpec((tk, d), lambda i, k: (k, 0)),
                  pl.BlockSpec((tm, d), lambda i, k: (i, 0)),
                  pl.BlockSpec((1, 9, d), lambda i, k: (group_of(i), 0, 0)),
                  pl.BlockSpec((1, d), lambda i, k: (0, 0)),
                  pl.BlockSpec((1, d), lambda i, k: (0, 0))],
        out_specs=pl.BlockSpec((tm, d), lambda i, k: (i, 0)),
        out_shape=jax.ShapeDtypeStruct((n, d), F32),
        compiler_params=_cparams(("parallel", "arbitrary")),
        name="mm_res_ln",
    )(a, w, x, mod3, ln_g.reshape(1, d), ln_b.reshape(1, d))


def _ffn_kernel(x_ref, mod_ref, w1_ref, w2_ref, g_ref, b_ref, o_ref, h_ref, *, which, nj):
    j = pl.program_id(1)

    @pl.when(j == 0)
    def _():
        shift = mod_ref[0, 3 * which:3 * which + 1, :]
        scale = mod_ref[0, 3 * which + 1:3 * which + 2, :]
        h_ref[...] = (x_ref[...] * (1.0 + scale) + shift).astype(BF16)
        o_ref[...] = jnp.zeros_like(o_ref)

    half = h_ref.shape[0] // 2
    for r0 in (0, half):
        h = h_ref[r0:r0 + half, :]
        a = _dot(h, w1_ref[:, 0:D_FF_TILE])
        hid = ((a * jax.nn.sigmoid(a)) * _dot(h, w1_ref[:, D_FF_TILE:2 * D_FF_TILE])).astype(BF16)
        for n0 in range(0, o_ref.shape[1], FFN_OUT_CHUNK):
            o_ref[r0:r0 + half, n0:n0 + FFN_OUT_CHUNK] += _dot(hid, w2_ref[:, n0:n0 + FFN_OUT_CHUNK])

    @pl.when(j == nj - 1)
    def _():
        gate = 0.5 * mod_ref[0, 3 * which + 2:3 * which + 3, :]
        for r0 in range(0, o_ref.shape[0], LN_ROWS):
            rows = slice(r0, r0 + LN_ROWS)
            o_ref[rows, :] = _res_ln(x_ref[rows, :], gate * o_ref[rows, :], g_ref[...], b_ref[...])


def _res_ln(x, f, g, b):
    y = ALPHA * x + f
    mu = jnp.mean(y, axis=-1, keepdims=True)
    yc = y - mu
    var = jnp.mean(yc * yc, axis=-1, keepdims=True)
    return yc * lax.rsqrt(var + LN_EPS) * g + b


def _ffn(x, mod3, w1, w2, ln_g, ln_b, group_of, *, which, tm, tile0=0, n_tiles=None):
    n, d = x.shape
    n_tiles = n // tm if n_tiles is None else n_tiles
    nj = w2.shape[0] // D_FF_TILE
    return pl.pallas_call(
        functools.partial(_ffn_kernel, which=which, nj=nj),
        grid=(n_tiles, nj),
        in_specs=[pl.BlockSpec((tm, d), lambda i, j: (tile0 + i, 0)),
                  pl.BlockSpec((1, 9, d), lambda i, j: (group_of(tile0 + i), 0, 0)),
                  pl.BlockSpec((d, 2 * D_FF_TILE), lambda i, j: (0, j)),
                  pl.BlockSpec((D_FF_TILE, d), lambda i, j: (j, 0)),
                  pl.BlockSpec((1, d), lambda i, j: (0, 0)),
                  pl.BlockSpec((1, d), lambda i, j: (0, 0))],
        out_specs=pl.BlockSpec((tm, d), lambda i, j: (i, 0)),
        out_shape=jax.ShapeDtypeStruct((n_tiles * tm, d), F32),
        scratch_shapes=[pltpu.VMEM((tm, d), BF16)],
        compiler_params=_cparams(("parallel", "arbitrary")),
        name="ffn",
    )(x, mod3, w1, w2, ln_g.reshape(1, d), ln_b.reshape(1, d))


def _prep_kernel(z_ref, za_ref, zb_ref, mu_ref, w2_ref, a2_ref, g2_ref, w0_ref, a0_ref, kk_ref_, ka_ref, rk_ref, seg_ref,
                 r_o, v_o, kk_o, lw0_o, kd0_o, bb0_o, lw1_o, kd1_o, bb1_o, bonus_o, g_o, zsh,
                 *, n_ctx_tiles, lat_tiles, grid_w):
    i = pl.program_id(0)
    tm = z_ref.shape[0]
    t = lax.broadcasted_iota(jnp.int32, (tm, 1), 0)
    z = z_ref[...]
    prev_row = pltpu.roll(z, 1, 0)
    next_row = pltpu.roll(z, tm - 1, 0)

    @pl.when(i < n_ctx_tiles)
    def _():
        nb = (jnp.where(t > 0, prev_row, 0.0) + jnp.where(t < tm - 1, next_row, 0.0)) * 0.5
        zsh[...] = z + mu_ref[...] * (nb - z)

    @pl.when(i >= n_ctx_tiles)
    def _():
        tpos = lax.rem(i - n_ctx_tiles, lat_tiles) * tm + t
        col = jnp.bitwise_and(tpos, grid_w - 1)
        above = jnp.concatenate([za_ref[...], z[0:tm - grid_w]], axis=0)
        below = jnp.concatenate([z[grid_w:tm], zb_ref[...]], axis=0)
        nb = (jnp.where(tpos >= grid_w, above, 0.0) + jnp.where(tpos < lat_tiles * tm - grid_w, below, 0.0)
              + jnp.where(col > 0, prev_row, 0.0) + jnp.where(col < grid_w - 1, next_row, 0.0)) * 0.25
        zsh[...] = z + mu_ref[...] * (nb - z)

    r = zsh[:, 0:D_A]
    k = zsh[:, D_A:2 * D_A]
    v = zsh[:, 2 * D_A:3 * D_A]
    xwa = zsh[:, 3 * D_A:3 * D_A + LANES]
    xg = zsh[:, 3 * D_A + LANES:3 * D_A + LANES + LORA_G_PAD]
    seg = seg_ref[...]
    r_o[...] = r
    v_o[...] = v

    g_o[...] = _dot(jax.nn.sigmoid(xg).astype(BF16), g2_ref[...])

    kk = k * kk_ref_[...]
    ksq = kk * kk
    ss = jnp.concatenate([_dot2(ksq[:, p * LANES:(p + 1) * LANES], seg) for p in range(N_PAIR)], axis=1)
    kk = kk / jnp.maximum(jnp.sqrt(ss), NRM_EPS)
    kk_o[...] = kk

    tw = jnp.tanh(xwa).astype(BF16)
    xa = xwa.astype(BF16)
    k_a = ka_ref[...]
    r_k = rk_ref[...]
    outs = ((lw0_o, kd0_o, bb0_o), (lw1_o, kd1_o, bb1_o))
    rkd = jnp.zeros_like(r)
    for d in range(2):
        lw_o, kd_o, bb_o = outs[d]
        wl = w0_ref[d:d + 1, :] + _dot(tw, w2_ref[d])
        wlog = -(jnp.maximum(-wl, 0.0) + jnp.log1p(jnp.exp(-jnp.abs(wl)))) - 0.5
        lw_o[...] = -jnp.exp(wlog)
        a = jax.nn.sigmoid(a0_ref[d:d + 1, :] + _dot(xa, a2_ref[d]))
        kd = k * (1.0 + (a - 1.0) * k_a)
        kd_o[...] = kd
        bb_o[...] = kk * a
        rkd = rkd + r * kd * r_k
    rs = jnp.concatenate([_dot2(rkd[:, p * LANES:(p + 1) * LANES], seg) for p in range(N_PAIR)], axis=1)
    bonus_o[...] = rs * v


def _rwkv_prep(zs, mu, w2p, a2p, g2p, w0, a0, k_k, k_a, r_k, *, n_ctx, t_ctx, t_lat):
    n, ncol = zs.shape
    tm = t_ctx
    assert t_lat % tm == 0 and tm % GRID_W == 0 and n_ctx % tm == 0
    halo_per_tile = tm // GRID_W
    n_halo = n // GRID_W
    head_of_lane = jnp.arange(LANES) // HEAD
    seg = (head_of_lane[:, None] == head_of_lane[None, :]).astype(BF16)
    full = lambda shape: pl.BlockSpec(shape, lambda i: (0,) * len(shape))
    row = pl.BlockSpec((tm, D_A), lambda i: (i, 0))
    out = jax.ShapeDtypeStruct((n, D_A), F32)
    return pl.pallas_call(
        functools.partial(_prep_kernel, n_ctx_tiles=n_ctx // tm, lat_tiles=t_lat // tm, grid_w=GRID_W),
        grid=(n // tm,),
        in_specs=[pl.BlockSpec((tm, ncol), lambda i: (i, 0)),
                  pl.BlockSpec((GRID_W, ncol), lambda i: (jnp.maximum(i * halo_per_tile - 1, 0), 0)),
                  pl.BlockSpec((GRID_W, ncol), lambda i: (jnp.minimum((i + 1) * halo_per_tile, n_halo - 1), 0)),
                  full((1, ncol)),
                  full((2, LANES, D_A)), full((2, LANES, D_A)), full((LORA_G_PAD, D_A)),
                  full((2, D_A)), full((2, D_A)), full((1, D_A)), full((1, D_A)), full((1, D_A)),
                  full((LANES, LANES))],
        out_specs=[row] * 11,
        out_shape=[out] * 11,
        scratch_shapes=[pltpu.VMEM((tm, ncol), F32)],
        compiler_params=_cparams(("parallel",)),
        name="rwkv_prep",
    )(zs, zs, zs, mu.reshape(1, ncol), w2p, a2p, g2p, w0, a0,
      k_k.reshape(1, D_A), k_a.reshape(1, D_A), r_k.reshape(1, D_A), seg)


def _blockdiag(x, lane_a):
    return jnp.concatenate([jnp.where(lane_a, x, 0.0), jnp.where(lane_a, 0.0, x)], axis=0)


def _pair_mm(a, b, lane_a):
    return _dot(a.astype(BF16), _blockdiag(b, lane_a).astype(BF16))


def _scan_chunks(chains):
    cs = SCAN_C
    ids = range(len(chains))
    lane_a = chains[0][6][0]
    bd_mask = chains[0][6][6]
    rows = [pl.ds(pl.multiple_of(ch[0] * cs, cs), cs) for ch in chains]
    lanes = [slice(ch[2] * PAIR, (ch[2] + 1) * PAIR) for ch in chains]
    load = lambda k: [chains[i][3][k][rows[i], lanes[i]] for i in ids]
    r, v, kk, lw, kd, bb = (load(k) for k in range(6))
    tri2, strict, incl, eye2, level_masks = ([chains[i][6][k] for i in ids] for k in range(1, 6))

    def split2(x):
        hi = x.astype(BF16)
        return jnp.concatenate([hi, (x - hi.astype(F32)).astype(BF16)], axis=0)

    lp = [_dot(tri2[i], split2(lw[i])) for i in ids]
    total = [lp[i][0:1, :] if chains[i][1] else lp[i][cs - 1:cs, :] for i in ids]
    p_inv = [jnp.exp(-lp[i]) for i in ids]
    lhs = [jnp.concatenate([kk[i] * jnp.exp(lp[i] - lw[i]), r[i] * jnp.exp(lp[i])], axis=0).astype(BF16)
           for i in ids]
    rhs = [jnp.concatenate([_blockdiag(bb[i] * p_inv[i], lane_a), _blockdiag(kd[i] * p_inv[i], lane_a)],
                           axis=0).astype(BF16) for i in ids]
    gram = [_dot_nt(lhs[i], rhs[i]) for i in ids]
    l_beta = [jnp.where(strict[i], gram[i][0:cs, 0:PAIR], 0.0) for i in ids]
    l_kappa = [jnp.where(strict[i], gram[i][0:cs, PAIR:2 * PAIR], 0.0).astype(BF16) for i in ids]
    m_both = [jnp.concatenate([jnp.where(incl[i], gram[i][cs:2 * cs, PAIR:2 * PAIR], 0.0),
                               jnp.where(incl[i], -gram[i][cs:2 * cs, 0:PAIR], 0.0)], axis=1).astype(BF16)
              for i in ids]

    dinv = [eye2[i] - jnp.where(level_masks[i][0], l_beta[i], 0.0) for i in ids]
    for lvl in range(1, len(level_masks[0])):
        x = [_pair_mm(jnp.where(level_masks[i][lvl], l_beta[i], 0.0), dinv[i], lane_a) for i in ids]
        dinv = [dinv[i] - _pair_mm(dinv[i], x[i], lane_a) for i in ids]

    s0 = [chains[i][4][chains[i][2]] for i in ids]
    from_state = [_dot_nt(lhs[i], s0[i].astype(BF16)) for i in ids]
    v_bd = [_blockdiag(v[i], lane_a).astype(BF16) for i in ids]
    w_rhs = [from_state[i][0:cs] + _dot(l_kappa[i], v_bd[i]) for i in ids]
    u = [_pair_mm(dinv[i], w_rhs[i], lane_a) for i in ids]
    y = [from_state[i][cs:2 * cs]
         + _dot(m_both[i], jnp.concatenate([v_bd[i], _blockdiag(u[i], lane_a).astype(BF16)], axis=0)) for i in ids]
    for i in ids:
        chains[i][5][rows[i], lanes[i]] = y[i]

    to_end = [jnp.exp(total[i] - lp[i]) for i in ids]
    zt = [jnp.concatenate([v[i], -u[i]], axis=0).T.astype(BF16) for i in ids]
    kb_end = [jnp.concatenate([kd[i] * to_end[i], bb[i] * to_end[i]], axis=0).astype(BF16) for i in ids]
    upd = [_dot(zt[i], kb_end[i]) for i in ids]
    for i in ids:
        chains[i][4][chains[i][2]] = s0[i] * jnp.exp(total[i]) + jnp.where(bd_mask, upd[i], 0.0)


def _scan_consts():
    cs = SCAN_C
    lane = lax.broadcasted_iota(jnp.int32, (cs, PAIR), 1)
    t = lax.broadcasted_iota(jnp.int32, (cs, PAIR), 0)
    i = jnp.bitwise_and(lane, HEAD - 1)
    lane_a = lane < HEAD
    out = {}
    for reverse in (False, True):
        before = (i > t) if reverse else (i < t)
        strict = before
        incl = jnp.logical_or(before, i == t)
        tri2 = incl.astype(BF16)
        eye2 = (i == t).astype(F32)
        level_masks = []
        s = 1
        while s < cs:
            sh_s = s.bit_length() - 1
            same_2s = lax.shift_right_logical(t, sh_s + 1) == lax.shift_right_logical(i, sh_s + 1)
            diff_s = lax.shift_right_logical(t, sh_s) != lax.shift_right_logical(i, sh_s)
            level_masks.append(jnp.logical_and(jnp.logical_and(same_2s, diff_s), strict))
            s *= 2
        out[reverse] = (tri2, strict, incl, eye2, tuple(level_masks))
    row = lax.broadcasted_iota(jnp.int32, (PAIR, PAIR), 0)
    col = lax.broadcasted_iota(jnp.int32, (PAIR, PAIR), 1)
    bd_mask = (row < HEAD) == (col < HEAD)
    return lane_a, out, bd_mask


def _scan_kernel(rf, vf, kkf, lw0, kd0, bb0, rb, vb, kkb, lw1, kd1, bb1, s0f_ref, s0b_ref,
                 yf_ref, yb_ref, sf_ref, sb_ref, sf_scr, sb_scr, *, n_ctx_seq, lat_tb):
    g = pl.program_id(0)
    is_ctx = g < n_ctx_seq
    j = lax.rem(jnp.maximum(g - n_ctx_seq, 0), lat_tb)
    n_chunk = SCAN_TB // SCAN_C
    lane_a, per_dir, bd_mask = _scan_consts()
    consts_f = (lane_a,) + per_dir[False] + (bd_mask,)
    consts_b = (lane_a,) + per_dir[True] + (bd_mask,)

    @pl.when(is_ctx)
    def _():
        sf_scr[...] = jnp.zeros_like(sf_scr)
        sb_scr[...] = jnp.zeros_like(sb_scr)

    @pl.when(jnp.logical_and(jnp.logical_not(is_ctx), j == 0))
    def _():
        sf_scr[...] = s0f_ref[0]
        sb_scr[...] = s0b_ref[0]

    def body(c, carry):
        chains = []
        for p in range(N_PAIR):
            chains.append((c, False, p, (rf, vf, kkf, lw0, kd0, bb0), sf_scr, yf_ref, consts_f))
            chains.append((n_chunk - 1 - c, True, p, (rb, vb, kkb, lw1, kd1, bb1), sb_scr, yb_ref, consts_b))
        _scan_chunks(chains)
        return carry

    lax.fori_loop(0, n_chunk, body, 0)

    @pl.when(jnp.logical_or(is_ctx, j == lat_tb - 1))
    def _():
        sf_ref[0] = sf_scr[...]
        sb_ref[0] = sb_scr[...]


def _rwkv_scan(prep, s0f, s0b, *, n_ctx_seq, n_lat_seq, t_lat):
    lat_tb = t_lat // SCAN_TB
    n_blocks = n_ctx_seq + n_lat_seq * lat_tb
    n_seq = n_ctx_seq + n_lat_seq

    def lat_idx(g):
        q = jnp.maximum(g - n_ctx_seq, 0)
        return q // lat_tb, lax.rem(q, lat_tb)

    def bwd_block(g):
        s, j = lat_idx(g)
        return jnp.where(g < n_ctx_seq, g, n_ctx_seq + s * lat_tb + (lat_tb - 1 - j))

    def seq_of(g):
        return jnp.where(g < n_ctx_seq, g, n_ctx_seq + lat_idx(g)[0])

    fwd = pl.BlockSpec((SCAN_TB, D_A), lambda g: (g, 0))
    bwd = pl.BlockSpec((SCAN_TB, D_A), lambda g: (bwd_block(g), 0))
    st_in = pl.BlockSpec((1, N_PAIR, PAIR, PAIR), lambda g: (lat_idx(g)[0], 0, 0, 0))
    st_out = pl.BlockSpec((1, N_PAIR, PAIR, PAIR), lambda g: (seq_of(g), 0, 0, 0))
    st_shape = jax.ShapeDtypeStruct((n_seq, N_PAIR, PAIR, PAIR), F32)
    y_shape = jax.ShapeDtypeStruct((n_blocks * SCAN_TB, D_A), F32)
    r, v, kk, lw0, kd0, bb0, lw1, kd1, bb1 = prep
    return pl.pallas_call(
        functools.partial(_scan_kernel, n_ctx_seq=n_ctx_seq, lat_tb=lat_tb),
        grid=(n_blocks,),
        in_specs=[fwd] * 6 + [bwd] * 6 + [st_in, st_in],
        out_specs=[fwd, bwd, st_out, st_out],
        out_shape=[y_shape, y_shape, st_shape, st_shape],
        scratch_shapes=[pltpu.VMEM((N_PAIR, PAIR, PAIR), F32), pltpu.VMEM((N_PAIR, PAIR, PAIR), F32)],
        compiler_params=_cparams(("arbitrary",)),
        name="rwkv_scan",
    )(r, v, kk, lw0, kd0, bb0, r, v, kk, lw1, kd1, bb1, s0f, s0b)


def _gmlp_kernel(uv_ref, lg_ref, lb_ref, ws_ref, bs_ref, o_ref, *, n_chunk):
    u = uv_ref[:, 0:D_B].astype(F32)
    v = uv_ref[:, D_B:2 * D_B].astype(F32)
    mu = jnp.mean(v, axis=-1, keepdims=True)
    vc = v - mu
    var = jnp.mean(vc * vc, axis=-1, keepdims=True)
    vn = ((vc * lax.rsqrt(var + LN_EPS)) * lg_ref[...] + lb_ref[...]).astype(BF16)
    for c in range(n_chunk):
        rows = slice(c * CHUNK, (c + 1) * CHUNK)
        for g in range(G_B):
            cols = slice(g * LANES, (g + 1) * LANES)
            s = _dot(ws_ref[g], vn[rows, cols]) + bs_ref[g]
            o_ref[rows, cols] = (u[rows, cols] * s).astype(o_ref.dtype)


def _gmlp(uv, ln_g, ln_b, ws, bs_b, *, n_chunk=4):
    n = uv.shape[0]
    tm = n_chunk * CHUNK
    full = lambda shape: pl.BlockSpec(shape, lambda i: (0,) * len(shape))
    return pl.pallas_call(
        functools.partial(_gmlp_kernel, n_chunk=n_chunk),
        grid=(n // tm,),
        in_specs=[pl.BlockSpec((tm, 2 * D_B), lambda i: (i, 0)),
                  full((1, D_B)), full((1, D_B)), full((G_B, CHUNK, CHUNK)), full((G_B, CHUNK, LANES))],
        out_specs=pl.BlockSpec((tm, D_B), lambda i: (i, 0)),
        out_shape=jax.ShapeDtypeStruct((n, D_B), BF16),
        compiler_params=_cparams(("parallel",)),
        name="gmlp",
    )(uv, ln_g.reshape(1, D_B), ln_b.reshape(1, D_B), ws, bs_b)


def _merge_kernel(yf_ref, yr_ref, bonus_ref, g_ref, gng_ref, gnb_ref, avg_ref, yb_ref, gts_ref, wpa_ref, wpb_ref, o_ref):
    avg = avg_ref[...]
    parts = []
    for p in range(N_PAIR):
        cols = slice(p * LANES, (p + 1) * LANES)
        yp = yf_ref[:, cols] + yr_ref[:, cols]
        yc = yp - _dot2(yp, avg)
        var = _dot2(yc * yc, avg)
        parts.append(yc * lax.rsqrt(var + GN_EPS))
    yn = jnp.concatenate(parts, axis=1)
    oa = ((yn * gng_ref[...] + gnb_ref[...] + bonus_ref[...]) * g_ref[...]).astype(BF16)
    o_ref[...] = (gts_ref[:, 0:D_MODEL] * _dot(oa, wpa_ref[...])
                  + gts_ref[:, D_MODEL:2 * D_MODEL] * _dot(yb_ref[...], wpb_ref[...])).astype(o_ref.dtype)


def _merge(y_fwd, y_bwd, bonus, g, gn_g, gn_b, yb, gts, w_pa, w_pb, *, tm=256):
    n = y_fwd.shape[0]
    head_of_lane = jnp.arange(LANES) // HEAD
    avg = ((head_of_lane[:, None] == head_of_lane[None, :]).astype(F32) / HEAD).astype(BF16)
    full = lambda shape: pl.BlockSpec(shape, lambda i: (0,) * len(shape))
    row = lambda w: pl.BlockSpec((tm, w), lambda i: (i, 0))
    return pl.pallas_call(
        _merge_kernel,
        grid=(n // tm,),
        in_specs=[row(D_A), row(D_A), row(D_A), row(D_A), full((1, D_A)), full((1, D_A)), full((LANES, LANES)),
                  row(D_B), row(2 * D_MODEL), full((D_A, D_MODEL)), full((D_B, D_MODEL))],
        out_specs=row(D_MODEL),
        out_shape=jax.ShapeDtypeStruct((n, D_MODEL), BF16),
        compiler_params=_cparams(("parallel",)),
        name="merge",
    )(y_fwd, y_bwd, bonus, g, gn_g.reshape(1, D_A), gn_b.reshape(1, D_A), avg, yb, gts, w_pa, w_pb)


def _pair_states(s):
    b = s.shape[0]
    s = s.reshape(b, N_PAIR, 2, HEAD, HEAD)
    z = jnp.zeros_like(s[:, :, 0])
    top = jnp.concatenate([s[:, :, 0], z], axis=-1)
    bot = jnp.concatenate([z, s[:, :, 1]], axis=-1)
    return jnp.concatenate([top, bot], axis=-2)


def _unpair_states(sp):
    b = sp.shape[0]
    return jnp.stack([sp[:, :, :HEAD, :HEAD], sp[:, :, HEAD:, HEAD:]], axis=2).reshape(b, H_A, HEAD, HEAD)


def kernel(x_prompt, x_sample, c, state_rwkv_fwd, state_rwkv_bwd, c_ctx, w_ada, b_ada, ln_g, ln_b,
           ffn_w_in, ffn_w_out, w_in, shift_mu, rw_w0, rw_w2, rw_a0, rw_a2, rw_g2, rw_k_k, rw_k_a,
           rw_r_k, rw_gn_g, rw_gn_b, sg_ln_g, sg_ln_b, sg_w, sg_b, w_pa, w_pb, w_o):
    n_ctx_seq, t_ctx, d = x_prompt.shape
    n_lat_seq, t_lat, _ = x_sample.shape
    n_ctx = n_ctx_seq * t_ctx
    n_lat = n_lat_seq * t_lat
    tm = 1024
    ctx_tiles = n_ctx // tm
    lat_tiles_per_seq = t_lat // tm

    def group_of(i):
        return jnp.where(i < ctx_tiles, 0, 1 + (i - ctx_tiles) // lat_tiles_per_seq)

    x = jnp.concatenate([x_prompt.reshape(n_ctx, d), x_sample.reshape(n_lat, d)], axis=0)
    cond8 = jnp.zeros((16, d), F32).at[0].set(c_ctx).at[1:1 + n_lat_seq].set(c)
    new_f, new_b = [], []
    for l in range(DEPTH):
        wi = ffn_w_in[l]
        pad_ff = lambda w: jnp.pad(w, ((0, 0), (0, D_FF_PAD - D_FF)))
        n_ff_tiles = D_FF_PAD // D_FF_TILE

        def ffn_in_layout(w):
            a = pad_ff(w[:, :D_FF]).reshape(d, n_ff_tiles, D_FF_TILE)
            b = pad_ff(w[:, D_FF:]).reshape(d, n_ff_tiles, D_FF_TILE)
            return jnp.concatenate([a, b], axis=2).reshape(d, 2 * D_FF_PAD).astype(BF16)

        w1 = [ffn_in_layout(wi[s]) for s in range(2)]
        w2 = [jnp.pad(ffn_w_out[l, s], ((0, D_FF_PAD - D_FF), (0, 0))).astype(BF16) for s in range(2)]
        w_in_p = jnp.concatenate([w_in[l][:, :N_SHIFT], jnp.zeros((d, N_SHIFT_PAD - N_SHIFT), F32),
                                  w_in[l][:, N_SHIFT:]], axis=1).astype(BF16)
        mu_p = jnp.pad(shift_mu[l], (0, N_SHIFT_PAD - N_SHIFT))
        zpad = jnp.zeros((2, LANES - LORA_W, D_A), F32)
        w2p = jnp.concatenate([rw_w2[l], zpad], axis=1).astype(BF16)
        a2p = jnp.concatenate([zpad, rw_a2[l]], axis=1).astype(BF16)
        g2p = jnp.pad(rw_g2[l], ((0, LORA_G_PAD - LORA_G), (0, 0))).astype(BF16)
        bs_b = jnp.broadcast_to(sg_b[l][:, :, None], (G_B, CHUNK, LANES))

        mod = _ada(cond8, w_ada[l], b_ada[l]).reshape(16, 9, d)

        x = _ffn(x, mod, w1[0], w2[0], ln_g[l, 0], ln_b[l, 0], group_of, which=0, tm=tm)

        zs = _mm_mod(x, mod, w_in_p[:, :N_SHIFT_PAD], group_of, which=1, act=_act_none, tm=tm, tn=1152,
                     tn_out=1152, out_dtype=F32)
        uv = _mm_mod(x, mod, w_in_p[:, N_SHIFT_PAD:N_SHIFT_PAD + 2 * D_B], group_of, which=1, act=_act_gelu,
                     tm=tm, tn=1024, tn_out=1024, out_dtype=BF16)
        gts = _mm_mod(x, mod, w_in_p[:, N_SHIFT_PAD + 2 * D_B:], group_of, which=1, act=_act_sigmoid,
                      tm=tm, tn=1024, tn_out=1024, out_dtype=F32)
        prep = _rwkv_prep(zs, mu_p, w2p, a2p, g2p, rw_w0[l], rw_a0[l], rw_k_k[l], rw_k_a[l], rw_r_k[l].reshape(D_A),
                          n_ctx=n_ctx, t_ctx=t_ctx, t_lat=t_lat)
        bonus, g_out = prep[9], prep[10]
        assert t_ctx == SCAN_TB
        y_fwd, y_bwd, sf, sb = _rwkv_scan(prep[:9], _pair_states(state_rwkv_fwd[:, l]),
                                          _pair_states(state_rwkv_bwd[:, l]),
                                          n_ctx_seq=n_ctx_seq, n_lat_seq=n_lat_seq, t_lat=t_lat)
        new_f.append(_unpair_states(sf[:n_ctx_seq]))
        new_b.append(_unpair_states(sb[:n_ctx_seq]))
        yb = _gmlp(uv, sg_ln_g[l], sg_ln_b[l], sg_w[l].astype(BF16), bs_b)
        merged = _merge(y_fwd, y_bwd, bonus, g_out, rw_gn_g[l], rw_gn_b[l], yb, gts, w_pa[l].astype(BF16), w_pb[l].astype(BF16))
        x = _mm_res_ln(merged, w_o[l].astype(BF16), x, mod, ln_g[l, 1], ln_b[l, 1], group_of, gate_row=5, coef=1.0,
                       tm=tm, tk=512)

        ffn2 = functools.partial(_ffn, x, mod, w1[1], w2[1], ln_g[l, 2], ln_b[l, 2], group_of, which=2, tm=tm)
        x_ctx = ffn2(tile0=0, n_tiles=ctx_tiles)
        x_lat = ffn2(tile0=ctx_tiles, n_tiles=n_lat // tm)
        if l + 1 < DEPTH:
            x = jnp.concatenate([x_ctx, x_lat], axis=0)

    y_prompt = x_ctx.reshape(n_ctx_seq, t_ctx, d)
    y_sample = x_lat.reshape(n_lat_seq, t_lat, d)
    new_state_fwd = jnp.stack(new_f, axis=1).astype(x_prompt.dtype)
    new_state_bwd = jnp.stack(new_b, axis=1).astype(x_prompt.dtype)
    return (y_prompt, y_sample, new_state_fwd, new_state_bwd)
```

```python
import functools

import jax
import jax.numpy as jnp
from jax import lax
from jax.experimental import pallas as pl
from jax.experimental.pallas import tpu as pltpu

F32 = jnp.float32
BF16 = jnp.bfloat16

D_MODEL = 2048
DEPTH = 1
D_A = 1024
HEAD = 64
H_A = D_A // HEAD
LORA_W = 64
LORA_A = 64
LORA_G = 160
D_B = 1024
CHUNK = 128
G_B = 8
D_FF = 5504
N_SHIFT = 3 * D_A + LORA_W + LORA_A + LORA_G
GRID_W = 64
ALPHA = (2.0 * DEPTH) ** 0.25
LN_EPS = 1e-5
GN_EPS = 64e-5
NRM_EPS = 1e-12

LANES = 128
VMEM_LIMIT = 56 * 1024 * 1024

LORA_G_PAD = 256
N_SHIFT_PAD = 3 * D_A + LORA_W + LORA_A + LORA_G_PAD
D_FF_TILE = 512
D_FF_PAD = ((D_FF + D_FF_TILE - 1) // D_FF_TILE) * D_FF_TILE
FFN_OUT_CHUNK = 512
LN_ROWS = 256
SCAN_C = 64
SCAN_TB = 256
PAIR = 2 * HEAD
N_PAIR = H_A // 2


def _cparams(sem):
    return pltpu.CompilerParams(dimension_semantics=sem, vmem_limit_bytes=VMEM_LIMIT)


def _dot(a, b):
    return jnp.dot(a, b, preferred_element_type=F32)


def _dot_nt(a, b):
    return lax.dot_general(a, b, (((1,), (1,)), ((), ())), preferred_element_type=F32)


def _dot2(x, m):
    hi = x.astype(BF16)
    lo = (x - hi.astype(F32)).astype(BF16)
    return _dot(hi, m) + _dot(lo, m)


def _ada_kernel(c_ref, w_ref, b_ref, o_ref):
    c = c_ref[...]
    s = (c * jax.nn.sigmoid(c)).astype(BF16)
    o_ref[...] = _dot(s, w_ref[...].astype(BF16)) + b_ref[...]


def _ada(cond8, w_ada, b_ada, tn=1024):
    rows, d = cond8.shape
    n = w_ada.shape[1]
    return pl.pallas_call(
        _ada_kernel,
        grid=(n // tn,),
        in_specs=[pl.BlockSpec((rows, d), lambda j: (0, 0)),
                  pl.BlockSpec((d, tn), lambda j: (0, j)),
                  pl.BlockSpec((1, tn), lambda j: (0, j))],
        out_specs=pl.BlockSpec((rows, tn), lambda j: (0, j)),
        out_shape=jax.ShapeDtypeStruct((rows, n), F32),
        compiler_params=_cparams(("parallel",)),
        name="ada",
    )(cond8, w_ada, b_ada.reshape(1, n))


def _act_none(acc):
    return acc


def _act_swiglu(acc):
    half = acc.shape[1] // 2
    a = acc[:, :half]
    return (a * jax.nn.sigmoid(a)) * acc[:, half:]


def _act_gelu(acc):
    return 0.5 * acc * (1.0 + lax.erf(acc * (2.0 ** -0.5)))


def _act_sigmoid(acc):
    return jax.nn.sigmoid(acc)


def _mm_mod_kernel(x_ref, mod_ref, w_ref, o_ref, h_ref, *, which, act):
    @pl.when(pl.program_id(1) == 0)
    def _():
        shift = mod_ref[0, 3 * which:3 * which + 1, :]
        scale = mod_ref[0, 3 * which + 1:3 * which + 2, :]
        h_ref[...] = (x_ref[...] * (1.0 + scale) + shift).astype(BF16)

    o_ref[...] = act(_dot(h_ref[...], w_ref[...])).astype(o_ref.dtype)


def _mm_mod(x, mod3, w, group_of, *, which, act, tm, tn, tn_out, out_dtype):
    n, d = x.shape
    nj = w.shape[1] // tn
    return pl.pallas_call(
        functools.partial(_mm_mod_kernel, which=which, act=act),
        grid=(n // tm, nj),
        in_specs=[pl.BlockSpec((tm, d), lambda i, j: (i, 0)),
                  pl.BlockSpec((1, 9, d), lambda i, j: (group_of(i), 0, 0)),
                  pl.BlockSpec((d, tn), lambda i, j: (0, j))],
        out_specs=pl.BlockSpec((tm, tn_out), lambda i, j: (i, j)),
        out_shape=jax.ShapeDtypeStruct((n, nj * tn_out), out_dtype),
        scratch_shapes=[pltpu.VMEM((tm, d), BF16)],
        compiler_params=_cparams(("parallel", "arbitrary")),
        name="mm_mod",
    )(x, mod3, w)


def _mm_res_ln_kernel(a_ref, w_ref, x_ref, mod_ref, g_ref, b_ref, o_ref, *, gate_row, coef, nk):
    k = pl.program_id(1)

    @pl.when(k == 0)
    def _():
        o_ref[...] = jnp.zeros_like(o_ref)

    for n0 in range(0, o_ref.shape[1], FFN_OUT_CHUNK):
        o_ref[:, n0:n0 + FFN_OUT_CHUNK] += _dot(a_ref[...], w_ref[:, n0:n0 + FFN_OUT_CHUNK])

    @pl.when(k == nk - 1)
    def _():
        gate = coef * mod_ref[0, gate_row:gate_row + 1, :]
        for r0 in range(0, o_ref.shape[0], LN_ROWS):
            rows = slice(r0, r0 + LN_ROWS)
            o_ref[rows, :] = _res_ln(x_ref[rows, :], gate * o_ref[rows, :], g_ref[...], b_ref[...])


def _mm_res_ln(a, w, x, mod3, ln_g, ln_b, group_of, *, gate_row, coef, tm, tk):
    n, kdim = a.shape
    d = w.shape[1]
    nk = kdim // tk
    return pl.pallas_call(
        functools.partial(_mm_res_ln_kernel, gate_row=gate_row, coef=coef, nk=nk),
        grid=(n // tm, nk),
        in_specs=[pl.BlockSpec((tm, tk), lambda i, k: (i, k)),
                  pl.BlockSpec((tk, d), lambda i, k: (k, 0)),
                  pl.BlockSpec((tm, d), lambda i, k: (i, 0)),
                  pl.BlockSpec((1, 9, d), lambda i, k: (group_of(i), 0, 0)),
                  pl.BlockSpec((1, d), lambda i, k: (0, 0)),
                  pl.BlockSpec((1, d), lambda i, k: (0, 0))],
        out_specs=pl.BlockSpec((tm, d), lambda i, k: (i, 0)),
        out_shape=jax.ShapeDtypeStruct((n, d), F32),
        compiler_params=_cparams(("parallel", "arbitrary")),
        name="mm_res_ln",
    )(a, w, x, mod3, ln_g.reshape(1, d), ln_b.reshape(1, d))


def _ffn_kernel(x_ref, mod_ref, w1_ref, w2_ref, g_ref, b_ref, o_ref, h_ref, *, which, nj):
    j = pl.program_id(1)

    @pl.when(j == 0)
    def _():
        shift = mod_ref[0, 3 * which:3 * which + 1, :]
        scale = mod_ref[0, 3 * which + 1:3 * which + 2, :]
        h_ref[...] = (x_ref[...] * (1.0 + scale) + shift).astype(BF16)
        o_ref[...] = jnp.zeros_like(o_ref)

    half = h_ref.shape[0] // 2
    for r0 in (0, half):
        h = h_ref[r0:r0 + half, :]
        a = _dot(h, w1_ref[0, :, 0:D_FF_TILE])
        hid = ((a * jax.nn.sigmoid(a)) * _dot(h, w1_ref[0, :, D_FF_TILE:2 * D_FF_TILE])).astype(BF16)
        for n0 in range(0, o_ref.shape[1], FFN_OUT_CHUNK):
            o_ref[r0:r0 + half, n0:n0 + FFN_OUT_CHUNK] += _dot(hid, w2_ref[0, :, n0:n0 + FFN_OUT_CHUNK])

    @pl.when(j == nj - 1)
    def _():
        gate = 0.5 * mod_ref[0, 3 * which + 2:3 * which + 3, :]
        for r0 in range(0, o_ref.shape[0], LN_ROWS):
            rows = slice(r0, r0 + LN_ROWS)
            o_ref[rows, :] = _res_ln(x_ref[rows, :], gate * o_ref[rows, :], g_ref[...], b_ref[...])


def _res_ln(x, f, g, b):
    y = ALPHA * x + f
    mu = jnp.mean(y, axis=-1, keepdims=True)
    yc = y - mu
    var = jnp.mean(yc * yc, axis=-1, keepdims=True)
    return yc * lax.rsqrt(var + LN_EPS) * g + b


def _ffn(x, mod3, w1, w2, ln_g, ln_b, group_of, *, which, tm, tile0=0, n_tiles=None):
    n, d = x.shape
    n_tiles = n // tm if n_tiles is None else n_tiles
    nj = w2.shape[1] // D_FF_TILE
    s = which // 2
    return pl.pallas_call(
        functools.partial(_ffn_kernel, which=which, nj=nj),
        grid=(n_tiles, nj),
        in_specs=[pl.BlockSpec((tm, d), lambda i, j: (tile0 + i, 0)),
                  pl.BlockSpec((1, 9, d), lambda i, j: (group_of(tile0 + i), 0, 0)),
                  pl.BlockSpec((1, d, 2 * D_FF_TILE), lambda i, j: (s, 0, j)),
                  pl.BlockSpec((1, D_FF_TILE, d), lambda i, j: (s, j, 0)),
                  pl.BlockSpec((1, d), lambda i, j: (0, 0)),
                  pl.BlockSpec((1, d), lambda i, j: (0, 0))],
        out_specs=pl.BlockSpec((tm, d), lambda i, j: (i, 0)),
        out_shape=jax.ShapeDtypeStruct((n_tiles * tm, d), F32),
        scratch_shapes=[pltpu.VMEM((tm, d), BF16)],
        compiler_params=_cparams(("parallel", "arbitrary")),
        name="ffn",
    )(x, mod3, w1, w2, ln_g.reshape(1, d), ln_b.reshape(1, d))


def _prep_kernel(z_ref, za_ref, zb_ref, mu_ref, w2_ref, a2_ref, g2_ref, w0_ref, a0_ref, kk_ref_, ka_ref, rk_ref, seg_ref,
                 r_o, v_o, kk_o, lw0_o, kd0_o, bb0_o, lw1_o, kd1_o, bb1_o, bonus_o, g_o, zsh,
                 *, n_ctx_tiles, lat_tiles, grid_w):
    i = pl.program_id(0)
    tm = z_ref.shape[0]
    t = lax.broadcasted_iota(jnp.int32, (tm, 1), 0)
    z = z_ref[...]
    prev_row = pltpu.roll(z, 1, 0)
    next_row = pltpu.roll(z, tm - 1, 0)

    @pl.when(i < n_ctx_tiles)
    def _():
        nb = (jnp.where(t > 0, prev_row, 0.0) + jnp.where(t < tm - 1, next_row, 0.0)) * 0.5
        zsh[...] = z + mu_ref[...] * (nb - z)

    @pl.when(i >= n_ctx_tiles)
    def _():
        tpos = lax.rem(i - n_ctx_tiles, lat_tiles) * tm + t
        col = jnp.bitwise_and(tpos, grid_w - 1)
        above = jnp.concatenate([za_ref[...], z[0:tm - grid_w]], axis=0)
        below = jnp.concatenate([z[grid_w:tm], zb_ref[...]], axis=0)
        nb = (jnp.where(tpos >= grid_w, above, 0.0) + jnp.where(tpos < lat_tiles * tm - grid_w, below, 0.0)
              + jnp.where(col > 0, prev_row, 0.0) + jnp.where(col < grid_w - 1, next_row, 0.0)) * 0.25
        zsh[...] = z + mu_ref[...] * (nb - z)

    r = zsh[:, 0:D_A]
    k = zsh[:, D_A:2 * D_A]
    v = zsh[:, 2 * D_A:3 * D_A]
    xwa = zsh[:, 3 * D_A:3 * D_A + LANES]
    xg = zsh[:, 3 * D_A + LANES:3 * D_A + LANES + LORA_G_PAD]
    seg = seg_ref[...]
    r_o[...] = r
    v_o[...] = v

    g_o[...] = _dot(jax.nn.sigmoid(xg).astype(BF16), g2_ref[...])

    kk = k * kk_ref_[...]
    ksq = kk * kk
    ss = jnp.concatenate([_dot2(ksq[:, p * LANES:(p + 1) * LANES], seg) for p in range(N_PAIR)], axis=1)
    kk = kk / jnp.maximum(jnp.sqrt(ss), NRM_EPS)
    kk_o[...] = kk

    tw = jnp.tanh(xwa).astype(BF16)
    xa = xwa.astype(BF16)
    k_a = ka_ref[...]
    r_k = rk_ref[...]
    outs = ((lw0_o, kd0_o, bb0_o), (lw1_o, kd1_o, bb1_o))
    rkd = jnp.zeros_like(r)
    for d in range(2):
        lw_o, kd_o, bb_o = outs[d]
        wl = w0_ref[d:d + 1, :] + _dot(tw, w2_ref[d])
        wlog = -(jnp.maximum(-wl, 0.0) + jnp.log1p(jnp.exp(-jnp.abs(wl)))) - 0.5
        lw_o[...] = -jnp.exp(wlog)
        a = jax.nn.sigmoid(a0_ref[d:d + 1, :] + _dot(xa, a2_ref[d]))
        kd = k * (1.0 + (a - 1.0) * k_a)
        kd_o[...] = kd
        bb_o[...] = kk * a
        rkd = rkd + r * kd * r_k
    rs = jnp.concatenate([_dot2(rkd[:, p * LANES:(p + 1) * LANES], seg) for p in range(N_PAIR)], axis=1)
    bonus_o[...] = rs * v


def _rwkv_prep(zs, mu, w2p, a2p, g2p, w0, a0, k_k, k_a, r_k, *, n_ctx, t_ctx, t_lat):
    n, ncol = zs.shape
    tm = t_ctx
    assert t_lat % tm == 0 and tm % GRID_W == 0 and n_ctx % tm == 0
    halo_per_tile = tm // GRID_W
    n_halo = n // GRID_W
    head_of_lane = jnp.arange(LANES) // HEAD
    seg = (head_of_lane[:, None] == head_of_lane[None, :]).astype(BF16)
    full = lambda shape: pl.BlockSpec(shape, lambda i: (0,) * len(shape))
    row = pl.BlockSpec((tm, D_A), lambda i: (i, 0))
    out = jax.ShapeDtypeStruct((n, D_A), F32)
    return pl.pallas_call(
        functools.partial(_prep_kernel, n_ctx_tiles=n_ctx // tm, lat_tiles=t_lat // tm, grid_w=GRID_W),
        grid=(n // tm,),
        in_specs=[pl.BlockSpec((tm, ncol), lambda i: (i, 0)),
                  pl.BlockSpec((GRID_W, ncol), lambda i: (jnp.maximum(i * halo_per_tile - 1, 0), 0)),
                  pl.BlockSpec((GRID_W, ncol), lambda i: (jnp.minimum((i + 1) * halo_per_tile, n_halo - 1), 0)),
                  full((1, ncol)),
                  full((2, LANES, D_A)), full((2, LANES, D_A)), full((LORA_G_PAD, D_A)),
                  full((2, D_A)), full((2, D_A)), full((1, D_A)), full((1, D_A)), full((1, D_A)),
                  full((LANES, LANES))],
        out_specs=[row] * 11,
        out_shape=[out] * 11,
        scratch_shapes=[pltpu.VMEM((tm, ncol), F32)],
        compiler_params=_cparams(("parallel",)),
        name="rwkv_prep",
    )(zs, zs, zs, mu.reshape(1, ncol), w2p, a2p, g2p, w0, a0,
      k_k.reshape(1, D_A), k_a.reshape(1, D_A), r_k.reshape(1, D_A), seg)


def _blockdiag(x, lane_a):
    return jnp.concatenate([jnp.where(lane_a, x, 0.0), jnp.where(lane_a, 0.0, x)], axis=0)


def _pair_mm(a, b, lane_a):
    return _dot(a.astype(BF16), _blockdiag(b, lane_a).astype(BF16))


def _scan_chunks(chains):
    cs = SCAN_C
    ids = range(len(chains))
    lane_a = chains[0][6][0]
    bd_mask = chains[0][6][6]
    rows = [pl.ds(pl.multiple_of(ch[0] * cs, cs), cs) for ch in chains]
    lanes = [slice(ch[2] * PAIR, (ch[2] + 1) * PAIR) for ch in chains]
    load = lambda k: [chains[i][3][k][rows[i], lanes[i]] for i in ids]
    r, v, kk, lw, kd, bb = (load(k) for k in range(6))
    tri2, strict, incl, eye2, level_masks = ([chains[i][6][k] for i in ids] for k in range(1, 6))

    def split2(x):
        hi = x.astype(BF16)
        return jnp.concatenate([hi, (x - hi.astype(F32)).astype(BF16)], axis=0)

    lp = [_dot(tri2[i], split2(lw[i])) for i in ids]
    total = [lp[i][0:1, :] if chains[i][1] else lp[i][cs - 1:cs, :] for i in ids]
    p_inv = [jnp.exp(-lp[i]) for i in ids]
    lhs = [jnp.concatenate([kk[i] * jnp.exp(lp[i] - lw[i]), r[i] * jnp.exp(lp[i])], axis=0).astype(BF16)
           for i in ids]
    rhs = [jnp.concatenate([_blockdiag(bb[i] * p_inv[i], lane_a), _blockdiag(kd[i] * p_inv[i], lane_a)],
                           axis=0).astype(BF16) for i in ids]
    gram = [_dot_nt(lhs[i], rhs[i]) for i in ids]
    l_beta = [jnp.where(strict[i], gram[i][0:cs, 0:PAIR], 0.0) for i in ids]
    l_kappa = [jnp.where(strict[i], gram[i][0:cs, PAIR:2 * PAIR], 0.0).astype(BF16) for i in ids]
    m_both = [jnp.concatenate([jnp.where(incl[i], gram[i][cs:2 * cs, PAIR:2 * PAIR], 0.0),
                               jnp.where(incl[i], -gram[i][cs:2 * cs, 0:PAIR], 0.0)], axis=1).astype(BF16)
              for i in ids]

    dinv = [eye2[i] - jnp.where(level_masks[i][0], l_beta[i], 0.0) for i in ids]
    for lvl in range(1, len(level_masks[0])):
        x = [_pair_mm(jnp.where(level_masks[i][lvl], l_beta[i], 0.0), dinv[i], lane_a) for i in ids]
        dinv = [dinv[i] - _pair_mm(dinv[i], x[i], lane_a) for i in ids]

    s0 = [chains[i][4][chains[i][2]] for i in ids]
    from_state = [_dot_nt(lhs[i], s0[i].astype(BF16)) for i in ids]
    v_bd = [_blockdiag(v[i], lane_a).astype(BF16) for i in ids]
    w_rhs = [from_state[i][0:cs] + _dot(l_kappa[i], v_bd[i]) for i in ids]
    u = [_pair_mm(dinv[i], w_rhs[i], lane_a) for i in ids]
    y = [from_state[i][cs:2 * cs]
         + _dot(m_both[i], jnp.concatenate([v_bd[i], _blockdiag(u[i], lane_a).astype(BF16)], axis=0)) for i in ids]
    for i in ids:
        chains[i][5][rows[i], lanes[i]] = y[i]

    to_end = [jnp.exp(total[i] - lp[i]) for i in ids]
    zt = [jnp.concatenate([v[i], -u[i]], axis=0).T.astype(BF16) for i in ids]
    kb_end = [jnp.concatenate([kd[i] * to_end[i], bb[i] * to_end[i]], axis=0).astype(BF16) for i in ids]
    upd = [_dot(zt[i], kb_end[i]) for i in ids]
    for i in ids:
        chains[i][4][chains[i][2]] = s0[i] * jnp.exp(total[i]) + jnp.where(bd_mask, upd[i], 0.0)


def _scan_consts():
    cs = SCAN_C
    lane = lax.broadcasted_iota(jnp.int32, (cs, PAIR), 1)
    t = lax.broadcasted_iota(jnp.int32, (cs, PAIR), 0)
    i = jnp.bitwise_and(lane, HEAD - 1)
    lane_a = lane < HEAD
    out = {}
    for reverse in (False, True):
        before = (i > t) if reverse else (i < t)
        strict = before
        incl = jnp.logical_or(before, i == t)
        tri2 = incl.astype(BF16)
        eye2 = (i == t).astype(F32)
        level_masks = []
        s = 1
        while s < cs:
            sh_s = s.bit_length() - 1
            same_2s = lax.shift_right_logical(t, sh_s + 1) == lax.shift_right_logical(i, sh_s + 1)
            diff_s = lax.shift_right_logical(t, sh_s) != lax.shift_right_logical(i, sh_s)
            level_masks.append(jnp.logical_and(jnp.logical_and(same_2s, diff_s), strict))
            s *= 2
        out[reverse] = (tri2, strict, incl, eye2, tuple(level_masks))
    row = lax.broadcasted_iota(jnp.int32, (PAIR, PAIR), 0)
    col = lax.broadcasted_iota(jnp.int32, (PAIR, PAIR), 1)
    bd_mask = (row < HEAD) == (col < HEAD)
    return lane_a, out, bd_mask


def _scan_kernel(rf, vf, kkf, lw0, kd0, bb0, rb, vb, kkb, lw1, kd1, bb1, s0f_ref, s0b_ref,
                 yf_ref, yb_ref, sf_ref, sb_ref, sf_scr, sb_scr, *, n_ctx_seq, lat_tb):
    g = pl.program_id(0)
    is_ctx = g < n_ctx_seq
    j = lax.rem(jnp.maximum(g - n_ctx_seq, 0), lat_tb)
    n_chunk = SCAN_TB // SCAN_C
    lane_a, per_dir, bd_mask = _scan_consts()
    consts_f = (lane_a,) + per_dir[False] + (bd_mask,)
    consts_b = (lane_a,) + per_dir[True] + (bd_mask,)

    @pl.when(is_ctx)
    def _():
        sf_scr[...] = jnp.zeros_like(sf_scr)
        sb_scr[...] = jnp.zeros_like(sb_scr)

    @pl.when(jnp.logical_and(jnp.logical_not(is_ctx), j == 0))
    def _():
        sf_scr[...] = s0f_ref[0]
        sb_scr[...] = s0b_ref[0]

    def body(c, carry):
        chains = []
        for p in range(N_PAIR):
            chains.append((c, False, p, (rf, vf, kkf, lw0, kd0, bb0), sf_scr, yf_ref, consts_f))
            chains.append((n_chunk - 1 - c, True, p, (rb, vb, kkb, lw1, kd1, bb1), sb_scr, yb_ref, consts_b))
        _scan_chunks(chains)
        return carry

    lax.fori_loop(0, n_chunk, body, 0)

    @pl.when(jnp.logical_or(is_ctx, j == lat_tb - 1))
    def _():
        sf_ref[0] = sf_scr[...]
        sb_ref[0] = sb_scr[...]


def _rwkv_scan(prep, s0f, s0b, *, n_ctx_seq, n_lat_seq, t_lat):
    lat_tb = t_lat // SCAN_TB
    n_blocks = n_ctx_seq + n_lat_seq * lat_tb
    n_seq = n_ctx_seq + n_lat_seq

    def lat_idx(g):
        q = jnp.maximum(g - n_ctx_seq, 0)
        return q // lat_tb, lax.rem(q, lat_tb)

    def bwd_block(g):
        s, j = lat_idx(g)
        return jnp.where(g < n_ctx_seq, g, n_ctx_seq + s * lat_tb + (lat_tb - 1 - j))

    def seq_of(g):
        return jnp.where(g < n_ctx_seq, g, n_ctx_seq + lat_idx(g)[0])

    fwd = pl.BlockSpec((SCAN_TB, D_A), lambda g: (g, 0))
    bwd = pl.BlockSpec((SCAN_TB, D_A), lambda g: (bwd_block(g), 0))
    st_in = pl.BlockSpec((1, N_PAIR, PAIR, PAIR), lambda g: (lat_idx(g)[0], 0, 0, 0))
    st_out = pl.BlockSpec((1, N_PAIR, PAIR, PAIR), lambda g: (seq_of(g), 0, 0, 0))
    st_shape = jax.ShapeDtypeStruct((n_seq, N_PAIR, PAIR, PAIR), F32)
    y_shape = jax.ShapeDtypeStruct((n_blocks * SCAN_TB, D_A), F32)
    r, v, kk, lw0, kd0, bb0, lw1, kd1, bb1 = prep
    return pl.pallas_call(
        functools.partial(_scan_kernel, n_ctx_seq=n_ctx_seq, lat_tb=lat_tb),
        grid=(n_blocks,),
        in_specs=[fwd] * 6 + [bwd] * 6 + [st_in, st_in],
        out_specs=[fwd, bwd, st_out, st_out],
        out_shape=[y_shape, y_shape, st_shape, st_shape],
        scratch_shapes=[pltpu.VMEM((N_PAIR, PAIR, PAIR), F32), pltpu.VMEM((N_PAIR, PAIR, PAIR), F32)],
        compiler_params=_cparams(("arbitrary",)),
        name="rwkv_scan",
    )(r, v, kk, lw0, kd0, bb0, r, v, kk, lw1, kd1, bb1, s0f, s0b)


def _gmlp_kernel(uv_ref, lg_ref, lb_ref, ws_ref, bs_ref, o_ref, *, n_chunk):
    u = uv_ref[:, 0:D_B].astype(F32)
    v = uv_ref[:, D_B:2 * D_B].astype(F32)
    mu = jnp.mean(v, axis=-1, keepdims=True)
    vc = v - mu
    var = jnp.mean(vc * vc, axis=-1, keepdims=True)
    vn = ((vc * lax.rsqrt(var + LN_EPS)) * lg_ref[...] + lb_ref[...]).astype(BF16)
    for c in range(n_chunk):
        rows = slice(c * CHUNK, (c + 1) * CHUNK)
        for g in range(G_B):
            cols = slice(g * LANES, (g + 1) * LANES)
            s = _dot(ws_ref[g], vn[rows, cols]) + bs_ref[g]
            o_ref[rows, cols] = (u[rows, cols] * s).astype(o_ref.dtype)


def _gmlp(uv, ln_g, ln_b, ws, bs_b, *, n_chunk=4):
    n = uv.shape[0]
    tm = n_chunk * CHUNK
    full = lambda shape: pl.BlockSpec(shape, lambda i: (0,) * len(shape))
    return pl.pallas_call(
        functools.partial(_gmlp_kernel, n_chunk=n_chunk),
        grid=(n // tm,),
        in_specs=[pl.BlockSpec((tm, 2 * D_B), lambda i: (i, 0)),
                  full((1, D_B)), full((1, D_B)), full((G_B, CHUNK, CHUNK)), full((G_B, CHUNK, LANES))],
        out_specs=pl.BlockSpec((tm, D_B), lambda i: (i, 0)),
        out_shape=jax.ShapeDtypeStruct((n, D_B), BF16),
        compiler_params=_cparams(("parallel",)),
        name="gmlp",
    )(uv, ln_g.reshape(1, D_B), ln_b.reshape(1, D_B), ws, bs_b)


def _merge_kernel(yf_ref, yr_ref, bonus_ref, g_ref, gng_ref, gnb_ref, avg_ref, yb_ref, ga_ref, gb_ref, wpa_ref, wpb_ref,
                  o_ref, oa_ref):
    @pl.when(pl.program_id(1) == 0)
    def _():
        avg = avg_ref[...]
        for p in range(N_PAIR):
            cols = slice(p * LANES, (p + 1) * LANES)
            yp = yf_ref[:, cols] + yr_ref[:, cols]
            yc = yp - _dot2(yp, avg)
            yn = yc * lax.rsqrt(_dot2(yc * yc, avg) + GN_EPS)
            oa_ref[:, cols] = ((yn * gng_ref[:, cols] + gnb_ref[:, cols] + bonus_ref[:, cols])
                               * g_ref[:, cols]).astype(BF16)

    o_ref[...] = (ga_ref[...] * _dot(oa_ref[...], wpa_ref[...])
                  + gb_ref[...] * _dot(yb_ref[...], wpb_ref[...])).astype(o_ref.dtype)


def _merge(y_fwd, y_bwd, bonus, g, gn_g, gn_b, yb, gts, w_pa, w_pb, *, tm=512, tn=512):
    n = y_fwd.shape[0]
    nj = D_MODEL // tn
    head_of_lane = jnp.arange(LANES) // HEAD
    avg = ((head_of_lane[:, None] == head_of_lane[None, :]).astype(F32) / HEAD).astype(BF16)
    full = lambda shape: pl.BlockSpec(shape, lambda i, j: (0,) * len(shape))
    row = lambda w: pl.BlockSpec((tm, w), lambda i, j: (i, 0))
    return pl.pallas_call(
        _merge_kernel,
        grid=(n // tm, nj),
        in_specs=[row(D_A), row(D_A), row(D_A), row(D_A), full((1, D_A)), full((1, D_A)), full((LANES, LANES)),
                  row(D_B),
                  pl.BlockSpec((tm, tn), lambda i, j: (i, j)), pl.BlockSpec((tm, tn), lambda i, j: (i, nj + j)),
                  pl.BlockSpec((D_A, tn), lambda i, j: (0, j)), pl.BlockSpec((D_B, tn), lambda i, j: (0, j))],
        out_specs=pl.BlockSpec((tm, tn), lambda i, j: (i, j)),
        out_shape=jax.ShapeDtypeStruct((n, D_MODEL), BF16),
        scratch_shapes=[pltpu.VMEM((tm, D_A), BF16)],
        compiler_params=_cparams(("parallel", "arbitrary")),
        name="merge",
    )(y_fwd, y_bwd, bonus, g, gn_g.reshape(1, D_A), gn_b.reshape(1, D_A), avg, yb, gts, gts, w_pa, w_pb)


def _w1_layout_kernel(a_ref, *refs):
    b_refs, o_ref = refs[:-1], refs[-1]
    valid = D_FF - pl.program_id(1) * D_FF_TILE
    keep = lax.broadcasted_iota(jnp.int32, (1, D_FF_TILE), 1) < valid
    a = jnp.where(keep, a_ref[0], 0.0)
    b = jnp.where(keep, jnp.concatenate([r[0] for r in b_refs], axis=1), 0.0)
    o_ref[0] = jnp.concatenate([a, b], axis=1).astype(BF16)


def _w2_layout_kernel(*refs):
    w_refs, o_ref = refs[:-1], refs[-1]
    valid = D_FF - pl.program_id(1) * D_FF_TILE
    keep = lax.broadcasted_iota(jnp.int32, (D_FF_TILE, 1), 0) < valid
    o_ref[0] = jnp.where(keep, jnp.concatenate([r[0] for r in w_refs], axis=0), 0.0).astype(BF16)


def _ffn_weight_layouts(w_in, w_out):
    s, d, _ = w_in.shape
    nj = D_FF_PAD // D_FF_TILE
    sub = D_FF_TILE // LANES
    first_b = D_FF // LANES
    last_blk = 2 * D_FF // LANES - 1
    b_spec = lambda k: pl.BlockSpec((1, d, LANES), lambda i, j: (i, 0, jnp.minimum(first_b + sub * j + k, last_blk)))
    w1 = pl.pallas_call(
        _w1_layout_kernel,
        grid=(s, nj),
        in_specs=[pl.BlockSpec((1, d, D_FF_TILE), lambda i, j: (i, 0, j))] + [b_spec(k) for k in range(sub)],
        out_specs=pl.BlockSpec((1, d, 2 * D_FF_TILE), lambda i, j: (i, 0, j)),
        out_shape=jax.ShapeDtypeStruct((s, d, 2 * D_FF_PAD), BF16),
        compiler_params=_cparams(("parallel", "parallel")),
        name="w1_layout",
    )(w_in, *([w_in] * sub))
    last_row_blk = D_FF // LANES - 1
    r_spec = lambda k: pl.BlockSpec((1, LANES, d), lambda i, j: (i, jnp.minimum(sub * j + k, last_row_blk), 0))
    w2 = pl.pallas_call(
        _w2_layout_kernel,
        grid=(s, nj),
        in_specs=[r_spec(k) for k in range(sub)],
        out_specs=pl.BlockSpec((1, D_FF_TILE, d), lambda i, j: (i, j, 0)),
        out_shape=jax.ShapeDtypeStruct((s, D_FF_PAD, d), BF16),
        compiler_params=_cparams(("parallel", "parallel")),
        name="w2_layout",
    )(*([w_out] * sub))
    return w1, w2


def _pair_states(s):
    b = s.shape[0]
    s = s.reshape(b, N_PAIR, 2, HEAD, HEAD)
    z = jnp.zeros_like(s[:, :, 0])
    top = jnp.concatenate([s[:, :, 0], z], axis=-1)
    bot = jnp.concatenate([z, s[:, :, 1]], axis=-1)
    return jnp.concatenate([top, bot], axis=-2)


def _unpair_states(sp):
    b = sp.shape[0]
    return jnp.stack([sp[:, :, :HEAD, :HEAD], sp[:, :, HEAD:, HEAD:]], axis=2).reshape(b, H_A, HEAD, HEAD)


def kernel(x_prompt, x_sample, c, state_rwkv_fwd, state_rwkv_bwd, c_ctx, w_ada, b_ada, ln_g, ln_b,
           ffn_w_in, ffn_w_out, w_in, shift_mu, rw_w0, rw_w2, rw_a0, rw_a2, rw_g2, rw_k_k, rw_k_a,
           rw_r_k, rw_gn_g, rw_gn_b, sg_ln_g, sg_ln_b, sg_w, sg_b, w_pa, w_pb, w_o):
    n_ctx_seq, t_ctx, d = x_prompt.shape
    n_lat_seq, t_lat, _ = x_sample.shape
    n_ctx = n_ctx_seq * t_ctx
    n_lat = n_lat_seq * t_lat
    tm = 1024
    ctx_tiles = n_ctx // tm
    lat_tiles_per_seq = t_lat // tm

    def group_of(i):
        return jnp.where(i < ctx_tiles, 0, 1 + (i - ctx_tiles) // lat_tiles_per_seq)

    x = jnp.concatenate([x_prompt.reshape(n_ctx, d), x_sample.reshape(n_lat, d)], axis=0)
    cond8 = jnp.zeros((16, d), F32).at[0].set(c_ctx).at[1:1 + n_lat_seq].set(c)
    new_f, new_b = [], []
    for l in range(DEPTH):
        w1, w2 = _ffn_weight_layouts(ffn_w_in[l], ffn_w_out[l])
        w_zs = jnp.pad(w_in[l][:, :N_SHIFT].astype(BF16), ((0, 0), (0, N_SHIFT_PAD - N_SHIFT)))
        w_uv = w_in[l][:, N_SHIFT:N_SHIFT + 2 * D_B].astype(BF16)
        w_gt = w_in[l][:, N_SHIFT + 2 * D_B:].astype(BF16)
        mu_p = jnp.pad(shift_mu[l], (0, N_SHIFT_PAD - N_SHIFT))
        zpad = jnp.zeros((2, LANES - LORA_W, D_A), F32)
        w2p = jnp.concatenate([rw_w2[l], zpad], axis=1).astype(BF16)
        a2p = jnp.concatenate([zpad, rw_a2[l]], axis=1).astype(BF16)
        g2p = jnp.pad(rw_g2[l], ((0, LORA_G_PAD - LORA_G), (0, 0))).astype(BF16)
        bs_b = jnp.broadcast_to(sg_b[l][:, :, None], (G_B, CHUNK, LANES))

        mod = _ada(cond8, w_ada[l], b_ada[l]).reshape(16, 9, d)

        x = _ffn(x, mod, w1, w2, ln_g[l, 0], ln_b[l, 0], group_of, which=0, tm=tm)

        zs = _mm_mod(x, mod, w_zs, group_of, which=1, act=_act_none, tm=tm, tn=1152, tn_out=1152, out_dtype=F32)
        uv = _mm_mod(x, mod, w_uv, group_of, which=1, act=_act_gelu, tm=tm, tn=1024, tn_out=1024, out_dtype=BF16)
        gts = _mm_mod(x, mod, w_gt, group_of, which=1, act=_act_sigmoid, tm=tm, tn=1024, tn_out=1024,
                      out_dtype=F32)
        prep = _rwkv_prep(zs, mu_p, w2p, a2p, g2p, rw_w0[l], rw_a0[l], rw_k_k[l], rw_k_a[l], rw_r_k[l].reshape(D_A),
                          n_ctx=n_ctx, t_ctx=t_ctx, t_lat=t_lat)
        bonus, g_out = prep[9], prep[10]
        assert t_ctx == SCAN_TB
        y_fwd, y_bwd, sf, sb = _rwkv_scan(prep[:9], _pair_states(state_rwkv_fwd[:, l]),
                                          _pair_states(state_rwkv_bwd[:, l]),
                                          n_ctx_seq=n_ctx_seq, n_lat_seq=n_lat_seq, t_lat=t_lat)
        new_f.append(_unpair_states(sf[:n_ctx_seq]))
        new_b.append(_unpair_states(sb[:n_ctx_seq]))
        yb = _gmlp(uv, sg_ln_g[l], sg_ln_b[l], sg_w[l].astype(BF16), bs_b)
        merged = _merge(y_fwd, y_bwd, bonus, g_out, rw_gn_g[l], rw_gn_b[l], yb, gts, w_pa[l].astype(BF16), w_pb[l].astype(BF16))
        x = _mm_res_ln(merged, w_o[l].astype(BF16), x, mod, ln_g[l, 1], ln_b[l, 1], group_of, gate_row=5, coef=1.0,
                       tm=tm, tk=1024)

        ffn2 = functools.partial(_ffn, x, mod, w1, w2, ln_g[l, 2], ln_b[l, 2], group_of, which=2, tm=tm)
        x_ctx = ffn2(tile0=0, n_tiles=ctx_tiles)
        x_lat = ffn2(tile0=ctx_tiles, n_tiles=n_lat // tm)
        if l + 1 < DEPTH:
            x = jnp.concatenate([x_ctx, x_lat], axis=0)

    y_prompt = x_ctx.reshape(n_ctx_seq, t_ctx, d)
    y_sample = x_lat.reshape(n_lat_seq, t_lat, d)
    new_state_fwd = jnp.stack(new_f, axis=1).astype(x_prompt.dtype)
    new_state_bwd = jnp.stack(new_b, axis=1).astype(x_prompt.dtype)
    return (y_prompt, y_sample, new_state_fwd, new_state_bwd)
```

```python
import functools

import jax
import jax.numpy as jnp
from jax import lax
from jax.experimental import pallas as pl
from jax.experimental.pallas import tpu as pltpu

F32 = jnp.float32
BF16 = jnp.bfloat16

D_MODEL = 2048
DEPTH = 1
D_A = 1024
HEAD = 64
H_A = D_A // HEAD
LORA_W = 64
LORA_A = 64
LORA_G = 160
D_B = 1024
CHUNK = 128
G_B = 8
D_FF = 5504
N_SHIFT = 3 * D_A + LORA_W + LORA_A + LORA_G
GRID_W = 64
ALPHA = (2.0 * DEPTH) ** 0.25
LN_EPS = 1e-5
GN_EPS = 64e-5
NRM_EPS = 1e-12

LANES = 128
VMEM_LIMIT = 56 * 1024 * 1024

LORA_G_PAD = 256
N_SHIFT_PAD = 3 * D_A + LORA_W + LORA_A + LORA_G_PAD
D_FF_TILE = 512
D_FF_PAD = ((D_FF + D_FF_TILE - 1) // D_FF_TILE) * D_FF_TILE
FFN_OUT_CHUNK = 512
LN_ROWS = 256
SCAN_C = 64
SCAN_TB = 256
PAIR = 2 * HEAD
N_PAIR = H_A // 2


def _cparams(sem):
    return pltpu.CompilerParams(dimension_semantics=sem, vmem_limit_bytes=VMEM_LIMIT)


def _dot(a, b):
    return jnp.dot(a, b, preferred_element_type=F32)


def _dot_nt(a, b):
    return lax.dot_general(a, b, (((1,), (1,)), ((), ())), preferred_element_type=F32)


def _dot2(x, m):
    hi = x.astype(BF16)
    lo = (x - hi.astype(F32)).astype(BF16)
    return _dot(hi, m) + _dot(lo, m)


def _ada_kernel(c_ref, w_ref, b_ref, o_ref):
    c = c_ref[...]
    s = (c * jax.nn.sigmoid(c)).astype(BF16)
    o_ref[...] = _dot(s, w_ref[...].astype(BF16)) + b_ref[...]


def _ada(cond8, w_ada, b_ada, tn=1024):
    rows, d = cond8.shape
    n = w_ada.shape[1]
    return pl.pallas_call(
        _ada_kernel,
        grid=(n // tn,),
        in_specs=[pl.BlockSpec((rows, d), lambda j: (0, 0)),
                  pl.BlockSpec((d, tn), lambda j: (0, j)),
                  pl.BlockSpec((1, tn), lambda j: (0, j))],
        out_specs=pl.BlockSpec((rows, tn), lambda j: (0, j)),
        out_shape=jax.ShapeDtypeStruct((rows, n), F32),
        compiler_params=_cparams(("parallel",)),
        name="ada",
    )(cond8, w_ada, b_ada.reshape(1, n))


def _act_none(acc):
    return acc


def _act_swiglu(acc):
    half = acc.shape[1] // 2
    a = acc[:, :half]
    return (a * jax.nn.sigmoid(a)) * acc[:, half:]


def _act_gelu(acc):
    return 0.5 * acc * (1.0 + lax.erf(acc * (2.0 ** -0.5)))


def _act_sigmoid(acc):
    return jax.nn.sigmoid(acc)


def _mm_mod_kernel(x_ref, mod_ref, w_ref, o_ref, h_ref, *, which, act, act_tail, tail_from):
    j = pl.program_id(1)

    @pl.when(j == 0)
    def _():
        shift = mod_ref[0, 3 * which:3 * which + 1, :]
        scale = mod_ref[0, 3 * which + 1:3 * which + 2, :]
        h_ref[...] = (x_ref[...] * (1.0 + scale) + shift).astype(BF16)

    acc = _dot(h_ref[...], w_ref[...])
    if act_tail is None:
        o_ref[...] = act(acc).astype(o_ref.dtype)
    else:
        @pl.when(j < tail_from)
        def _():
            o_ref[...] = act(acc).astype(o_ref.dtype)

        @pl.when(j >= tail_from)
        def _():
            o_ref[...] = act_tail(acc).astype(o_ref.dtype)


def _mm_mod(x, mod3, w, group_of, *, which, act, tm, tn, tn_out, out_dtype, act_tail=None, tail_from=0):
    n, d = x.shape
    nj = w.shape[1] // tn
    return pl.pallas_call(
        functools.partial(_mm_mod_kernel, which=which, act=act, act_tail=act_tail, tail_from=tail_from),
        grid=(n // tm, nj),
        in_specs=[pl.BlockSpec((tm, d), lambda i, j: (i, 0)),
                  pl.BlockSpec((1, 9, d), lambda i, j: (group_of(i), 0, 0)),
                  pl.BlockSpec((d, tn), lambda i, j: (0, j))],
        out_specs=pl.BlockSpec((tm, tn_out), lambda i, j: (i, j)),
        out_shape=jax.ShapeDtypeStruct((n, nj * tn_out), out_dtype),
        scratch_shapes=[pltpu.VMEM((tm, d), BF16)],
        compiler_params=_cparams(("parallel", "arbitrary")),
        name="mm_mod",
    )(x, mod3, w)


def _mm_res_ln_kernel(a_ref, w_ref, x_ref, mod_ref, g_ref, b_ref, o_ref, *, gate_row, coef, nk):
    k = pl.program_id(1)

    @pl.when(k == 0)
    def _():
        o_ref[...] = jnp.zeros_like(o_ref)

    for n0 in range(0, o_ref.shape[1], FFN_OUT_CHUNK):
        o_ref[:, n0:n0 + FFN_OUT_CHUNK] += _dot(a_ref[...], w_ref[:, n0:n0 + FFN_OUT_CHUNK])

    @pl.when(k == nk - 1)
    def _():
        gate = coef * mod_ref[0, gate_row:gate_row + 1, :]
        for r0 in range(0, o_ref.shape[0], LN_ROWS):
            rows = slice(r0, r0 + LN_ROWS)
            o_ref[rows, :] = _res_ln(x_ref[rows, :], gate * o_ref[rows, :], g_ref[...], b_ref[...])


def _mm_res_ln(a, w, x, mod3, ln_g, ln_b, group_of, *, gate_row, coef, tm, tk):
    n, kdim = a.shape
    d = w.shape[1]
    nk = kdim // tk
    return pl.pallas_call(
        functools.partial(_mm_res_ln_kernel, gate_row=gate_row, coef=coef, nk=nk),
        grid=(n // tm, nk),
        in_specs=[pl.BlockSpec((tm, tk), lambda i, k: (i, k)),
                  pl.BlockSpec((tk, d), lambda i, k: (k, 0)),
                  pl.BlockSpec((tm, d), lambda i, k: (i, 0)),
                  pl.BlockSpec((1, 9, d), lambda i, k: (group_of(i), 0, 0)),
                  pl.BlockSpec((1, d), lambda i, k: (0, 0)),
                  pl.BlockSpec((1, d), lambda i, k: (0, 0))],
        out_specs=pl.BlockSpec((tm, d), lambda i, k: (i, 0)),
        out_shape=jax.ShapeDtypeStruct((n, d), F32),
        compiler_params=_cparams(("parallel", "arbitrary")),
        name="mm_res_ln",
    )(a, w, x, mod3, ln_g.reshape(1, d), ln_b.reshape(1, d))


def _ffn_kernel(x_ref, mod_ref, w1_ref, w2_ref, g_ref, b_ref, o_ref, h_ref, *, which, nj):
    j = pl.program_id(1)

    @pl.when(j == 0)
    def _():
        shift = mod_ref[0, 3 * which:3 * which + 1, :]
        scale = mod_ref[0, 3 * which + 1:3 * which + 2, :]
        h_ref[...] = (x_ref[...] * (1.0 + scale) + shift).astype(BF16)
        o_ref[...] = jnp.zeros_like(o_ref)

    half = h_ref.shape[0] // 2
    for r0 in (0, half):
        h = h_ref[r0:r0 + half, :]
        a = _dot(h, w1_ref[0, :, 0:D_FF_TILE])
        hid = ((a * jax.nn.sigmoid(a)) * _dot(h, w1_ref[0, :, D_FF_TILE:2 * D_FF_TILE])).astype(BF16)
        for n0 in range(0, o_ref.shape[1], FFN_OUT_CHUNK):
            o_ref[r0:r0 + half, n0:n0 + FFN_OUT_CHUNK] += _dot(hid, w2_ref[0, :, n0:n0 + FFN_OUT_CHUNK])

    @pl.when(j == nj - 1)
    def _():
        gate = 0.5 * mod_ref[0, 3 * which + 2:3 * which + 3, :]
        for r0 in range(0, o_ref.shape[0], LN_ROWS):
            rows = slice(r0, r0 + LN_ROWS)
            o_ref[rows, :] = _res_ln(x_ref[rows, :], gate * o_ref[rows, :], g_ref[...], b_ref[...])


def _res_ln(x, f, g, b):
    y = ALPHA * x + f
    mu = jnp.mean(y, axis=-1, keepdims=True)
    yc = y - mu
    var = jnp.mean(yc * yc, axis=-1, keepdims=True)
    return yc * lax.rsqrt(var + LN_EPS) * g + b


def _ffn(x, mod3, w1, w2, ln_g, ln_b, group_of, *, which, tm, tile0=0, n_tiles=None):
    n, d = x.shape
    n_tiles = n // tm if n_tiles is None else n_tiles
    nj = w2.shape[1] // D_FF_TILE
    s = which // 2
    return pl.pallas_call(
        functools.partial(_ffn_kernel, which=which, nj=nj),
        grid=(n_tiles, nj),
        in_specs=[pl.BlockSpec((tm, d), lambda i, j: (tile0 + i, 0)),
                  pl.BlockSpec((1, 9, d), lambda i, j: (group_of(tile0 + i), 0, 0)),
                  pl.BlockSpec((1, d, 2 * D_FF_TILE), lambda i, j: (s, 0, j)),
                  pl.BlockSpec((1, D_FF_TILE, d), lambda i, j: (s, j, 0)),
                  pl.BlockSpec((1, d), lambda i, j: (0, 0)),
                  pl.BlockSpec((1, d), lambda i, j: (0, 0))],
        out_specs=pl.BlockSpec((tm, d), lambda i, j: (i, 0)),
        out_shape=jax.ShapeDtypeStruct((n_tiles * tm, d), F32),
        scratch_shapes=[pltpu.VMEM((tm, d), BF16)],
        compiler_params=_cparams(("parallel", "arbitrary")),
        name="ffn",
    )(x, mod3, w1, w2, ln_g.reshape(1, d), ln_b.reshape(1, d))


def _prep_kernel(z_ref, za_ref, zb_ref, mu_ref, w2_ref, a2_ref, g2_ref, w0_ref, a0_ref, kk_ref_, ka_ref, rk_ref, seg_ref,
                 r_o, v_o, kk_o, lw0_o, kd0_o, bb0_o, lw1_o, kd1_o, bb1_o, bonus_o, g_o, zsh,
                 *, n_ctx_tiles, lat_tiles, grid_w):
    i = pl.program_id(0)
    tm = z_ref.shape[0]
    t = lax.broadcasted_iota(jnp.int32, (tm, 1), 0)
    z = z_ref[...]
    prev_row = pltpu.roll(z, 1, 0)
    next_row = pltpu.roll(z, tm - 1, 0)

    @pl.when(i < n_ctx_tiles)
    def _():
        nb = (jnp.where(t > 0, prev_row, 0.0) + jnp.where(t < tm - 1, next_row, 0.0)) * 0.5
        zsh[...] = z + mu_ref[...] * (nb - z)

    @pl.when(i >= n_ctx_tiles)
    def _():
        tpos = lax.rem(i - n_ctx_tiles, lat_tiles) * tm + t
        col = jnp.bitwise_and(tpos, grid_w - 1)
        above = jnp.concatenate([za_ref[...], z[0:tm - grid_w]], axis=0)
        below = jnp.concatenate([z[grid_w:tm], zb_ref[...]], axis=0)
        nb = (jnp.where(tpos >= grid_w, above, 0.0) + jnp.where(tpos < lat_tiles * tm - grid_w, below, 0.0)
              + jnp.where(col > 0, prev_row, 0.0) + jnp.where(col < grid_w - 1, next_row, 0.0)) * 0.25
        zsh[...] = z + mu_ref[...] * (nb - z)

    r = zsh[:, 0:D_A]
    k = zsh[:, D_A:2 * D_A]
    v = zsh[:, 2 * D_A:3 * D_A]
    xwa = zsh[:, 3 * D_A:3 * D_A + LANES]
    xg = zsh[:, 3 * D_A + LANES:3 * D_A + LANES + LORA_G_PAD]
    seg = seg_ref[...]
    r_o[...] = r
    v_o[...] = v

    g_o[...] = _dot(jax.nn.sigmoid(xg).astype(BF16), g2_ref[...])

    kk = k * kk_ref_[...]
    ksq = kk * kk
    ss = jnp.concatenate([_dot2(ksq[:, p * LANES:(p + 1) * LANES], seg) for p in range(N_PAIR)], axis=1)
    kk = kk / jnp.maximum(jnp.sqrt(ss), NRM_EPS)
    kk_o[...] = kk

    tw = jnp.tanh(xwa).astype(BF16)
    xa = xwa.astype(BF16)
    k_a = ka_ref[...]
    r_k = rk_ref[...]
    outs = ((lw0_o, kd0_o, bb0_o), (lw1_o, kd1_o, bb1_o))
    rkd = jnp.zeros_like(r)
    for d in range(2):
        lw_o, kd_o, bb_o = outs[d]
        wl = w0_ref[d:d + 1, :] + _dot(tw, w2_ref[d])
        wlog = -(jnp.maximum(-wl, 0.0) + jnp.log1p(jnp.exp(-jnp.abs(wl)))) - 0.5
        lw_o[...] = -jnp.exp(wlog)
        a = jax.nn.sigmoid(a0_ref[d:d + 1, :] + _dot(xa, a2_ref[d]))
        kd = k * (1.0 + (a - 1.0) * k_a)
        kd_o[...] = kd
        bb_o[...] = kk * a
        rkd = rkd + r * kd * r_k
    rs = jnp.concatenate([_dot2(rkd[:, p * LANES:(p + 1) * LANES], seg) for p in range(N_PAIR)], axis=1)
    bonus_o[...] = rs * v


def _rwkv_prep(zs, mu, w2p, a2p, g2p, w0, a0, k_k, k_a, r_k, *, n_ctx, t_ctx, t_lat):
    n, ncol = zs.shape
    tm = t_ctx
    assert t_lat % tm == 0 and tm % GRID_W == 0 and n_ctx % tm == 0
    halo_per_tile = tm // GRID_W
    n_halo = n // GRID_W
    head_of_lane = jnp.arange(LANES) // HEAD
    seg = (head_of_lane[:, None] == head_of_lane[None, :]).astype(BF16)
    full = lambda shape: pl.BlockSpec(shape, lambda i: (0,) * len(shape))
    row = pl.BlockSpec((tm, D_A), lambda i: (i, 0))
    out = jax.ShapeDtypeStruct((n, D_A), F32)
    return pl.pallas_call(
        functools.partial(_prep_kernel, n_ctx_tiles=n_ctx // tm, lat_tiles=t_lat // tm, grid_w=GRID_W),
        grid=(n // tm,),
        in_specs=[pl.BlockSpec((tm, ncol), lambda i: (i, 0)),
                  pl.BlockSpec((GRID_W, ncol), lambda i: (jnp.maximum(i * halo_per_tile - 1, 0), 0)),
                  pl.BlockSpec((GRID_W, ncol), lambda i: (jnp.minimum((i + 1) * halo_per_tile, n_halo - 1), 0)),
                  full((1, ncol)),
                  full((2, LANES, D_A)), full((2, LANES, D_A)), full((LORA_G_PAD, D_A)),
                  full((2, D_A)), full((2, D_A)), full((1, D_A)), full((1, D_A)), full((1, D_A)),
                  full((LANES, LANES))],
        out_specs=[row] * 11,
        out_shape=[out] * 11,
        scratch_shapes=[pltpu.VMEM((tm, ncol), F32)],
        compiler_params=_cparams(("parallel",)),
        name="rwkv_prep",
    )(zs, zs, zs, mu.reshape(1, ncol), w2p, a2p, g2p, w0, a0,
      k_k.reshape(1, D_A), k_a.reshape(1, D_A), r_k.reshape(1, D_A), seg)


def _blockdiag(x, lane_a):
    return jnp.concatenate([jnp.where(lane_a, x, 0.0), jnp.where(lane_a, 0.0, x)], axis=0)


def _pair_mm(a, b, lane_a):
    return _dot(a.astype(BF16), _blockdiag(b, lane_a).astype(BF16))


def _scan_chunks(chains):
    cs = SCAN_C
    ids = range(len(chains))
    lane_a = chains[0][6][0]
    bd_mask = chains[0][6][6]
    rows = [pl.ds(pl.multiple_of(ch[0] * cs, cs), cs) for ch in chains]
    lanes = [slice(ch[2] * PAIR, (ch[2] + 1) * PAIR) for ch in chains]
    load = lambda k: [chains[i][3][k][rows[i], lanes[i]] for i in ids]
    r, v, kk, lw, kd, bb = (load(k) for k in range(6))
    tri2, strict, incl, eye2, level_masks = ([chains[i][6][k] for i in ids] for k in range(1, 6))

    def split2(x):
        hi = x.astype(BF16)
        return jnp.concatenate([hi, (x - hi.astype(F32)).astype(BF16)], axis=0)

    lp = [_dot(tri2[i], split2(lw[i])) for i in ids]
    total = [lp[i][0:1, :] if chains[i][1] else lp[i][cs - 1:cs, :] for i in ids]
    p_inv = [jnp.exp(-lp[i]) for i in ids]
    lhs = [jnp.concatenate([kk[i] * jnp.exp(lp[i] - lw[i]), r[i] * jnp.exp(lp[i])], axis=0).astype(BF16)
           for i in ids]
    rhs = [jnp.concatenate([_blockdiag(bb[i] * p_inv[i], lane_a), _blockdiag(kd[i] * p_inv[i], lane_a)],
                           axis=0).astype(BF16) for i in ids]
    gram = [_dot_nt(lhs[i], rhs[i]) for i in ids]
    l_beta = [jnp.where(strict[i], gram[i][0:cs, 0:PAIR], 0.0) for i in ids]
    l_kappa = [jnp.where(strict[i], gram[i][0:cs, PAIR:2 * PAIR], 0.0).astype(BF16) for i in ids]
    m_both = [jnp.concatenate([jnp.where(incl[i], gram[i][cs:2 * cs, PAIR:2 * PAIR], 0.0),
                               jnp.where(incl[i], -gram[i][cs:2 * cs, 0:PAIR], 0.0)], axis=1).astype(BF16)
              for i in ids]

    dinv = [eye2[i] - jnp.where(level_masks[i][0], l_beta[i], 0.0) for i in ids]
    for lvl in range(1, len(level_masks[0])):
        x = [_pair_mm(jnp.where(level_masks[i][lvl], l_beta[i], 0.0), dinv[i], lane_a) for i in ids]
        dinv = [dinv[i] - _pair_mm(dinv[i], x[i], lane_a) for i in ids]

    s0 = [chains[i][4][chains[i][2]] for i in ids]
    from_state = [_dot_nt(lhs[i], s0[i].astype(BF16)) for i in ids]
    v_bd = [_blockdiag(v[i], lane_a).astype(BF16) for i in ids]
    w_rhs = [from_state[i][0:cs] + _dot(l_kappa[i], v_bd[i]) for i in ids]
    u = [_pair_mm(dinv[i], w_rhs[i], lane_a) for i in ids]
    y = [from_state[i][cs:2 * cs]
         + _dot(m_both[i], jnp.concatenate([v_bd[i], _blockdiag(u[i], lane_a).astype(BF16)], axis=0)) for i in ids]
    for i in ids:
        chains[i][5][rows[i], lanes[i]] = y[i]

    to_end = [jnp.exp(total[i] - lp[i]) for i in ids]
    zt = [jnp.concatenate([v[i], -u[i]], axis=0).T.astype(BF16) for i in ids]
    kb_end = [jnp.concatenate([kd[i] * to_end[i], bb[i] * to_end[i]], axis=0).astype(BF16) for i in ids]
    upd = [_dot(zt[i], kb_end[i]) for i in ids]
    for i in ids:
        chains[i][4][chains[i][2]] = s0[i] * jnp.exp(total[i]) + jnp.where(bd_mask, upd[i], 0.0)


def _scan_consts():
    cs = SCAN_C
    lane = lax.broadcasted_iota(jnp.int32, (cs, PAIR), 1)
    t = lax.broadcasted_iota(jnp.int32, (cs, PAIR), 0)
    i = jnp.bitwise_and(lane, HEAD - 1)
    lane_a = lane < HEAD
    out = {}
    for reverse in (False, True):
        before = (i > t) if reverse else (i < t)
        strict = before
        incl = jnp.logical_or(before, i == t)
        tri2 = incl.astype(BF16)
        eye2 = (i == t).astype(F32)
        level_masks = []
        s = 1
        while s < cs:
            sh_s = s.bit_length() - 1
            same_2s = lax.shift_right_logical(t, sh_s + 1) == lax.shift_right_logical(i, sh_s + 1)
            diff_s = lax.shift_right_logical(t, sh_s) != lax.shift_right_logical(i, sh_s)
            level_masks.append(jnp.logical_and(jnp.logical_and(same_2s, diff_s), strict))
            s *= 2
        out[reverse] = (tri2, strict, incl, eye2, tuple(level_masks))
    row = lax.broadcasted_iota(jnp.int32, (PAIR, PAIR), 0)
    col = lax.broadcasted_iota(jnp.int32, (PAIR, PAIR), 1)
    bd_mask = (row < HEAD) == (col < HEAD)
    return lane_a, out, bd_mask


def _scan_kernel(rf, vf, kkf, lw0, kd0, bb0, rb, vb, kkb, lw1, kd1, bb1, s0f_ref, s0b_ref,
                 yf_ref, yb_ref, sf_ref, sb_ref, sf_scr, sb_scr, *, n_ctx_seq, lat_tb):
    g = pl.program_id(0)
    is_ctx = g < n_ctx_seq
    j = lax.rem(jnp.maximum(g - n_ctx_seq, 0), lat_tb)
    n_chunk = SCAN_TB // SCAN_C
    lane_a, per_dir, bd_mask = _scan_consts()
    consts_f = (lane_a,) + per_dir[False] + (bd_mask,)
    consts_b = (lane_a,) + per_dir[True] + (bd_mask,)

    @pl.when(is_ctx)
    def _():
        sf_scr[...] = jnp.zeros_like(sf_scr)
        sb_scr[...] = jnp.zeros_like(sb_scr)

    @pl.when(jnp.logical_and(jnp.logical_not(is_ctx), j == 0))
    def _():
        sf_scr[...] = s0f_ref[0]
        sb_scr[...] = s0b_ref[0]

    def body(c, carry):
        chains = []
        for p in range(N_PAIR):
            chains.append((c, False, p, (rf, vf, kkf, lw0, kd0, bb0), sf_scr, yf_ref, consts_f))
            chains.append((n_chunk - 1 - c, True, p, (rb, vb, kkb, lw1, kd1, bb1), sb_scr, yb_ref, consts_b))
        _scan_chunks(chains)
        return carry

    lax.fori_loop(0, n_chunk, body, 0)

    @pl.when(jnp.logical_or(is_ctx, j == lat_tb - 1))
    def _():
        sf_ref[0] = sf_scr[...]
        sb_ref[0] = sb_scr[...]


def _rwkv_scan(prep, s0f, s0b, *, n_ctx_seq, n_lat_seq, t_lat):
    lat_tb = t_lat // SCAN_TB
    n_blocks = n_ctx_seq + n_lat_seq * lat_tb
    n_seq = n_ctx_seq + n_lat_seq

    def lat_idx(g):
        q = jnp.maximum(g - n_ctx_seq, 0)
        return q // lat_tb, lax.rem(q, lat_tb)

    def bwd_block(g):
        s, j = lat_idx(g)
        return jnp.where(g < n_ctx_seq, g, n_ctx_seq + s * lat_tb + (lat_tb - 1 - j))

    def seq_of(g):
        return jnp.where(g < n_ctx_seq, g, n_ctx_seq + lat_idx(g)[0])

    fwd = pl.BlockSpec((SCAN_TB, D_A), lambda g: (g, 0))
    bwd = pl.BlockSpec((SCAN_TB, D_A), lambda g: (bwd_block(g), 0))
    st_in = pl.BlockSpec((1, N_PAIR, PAIR, PAIR), lambda g: (lat_idx(g)[0], 0, 0, 0))
    st_out = pl.BlockSpec((1, N_PAIR, PAIR, PAIR), lambda g: (seq_of(g), 0, 0, 0))
    st_shape = jax.ShapeDtypeStruct((n_seq, N_PAIR, PAIR, PAIR), F32)
    y_shape = jax.ShapeDtypeStruct((n_blocks * SCAN_TB, D_A), F32)
    r, v, kk, lw0, kd0, bb0, lw1, kd1, bb1 = prep
    return pl.pallas_call(
        functools.partial(_scan_kernel, n_ctx_seq=n_ctx_seq, lat_tb=lat_tb),
        grid=(n_blocks,),
        in_specs=[fwd] * 6 + [bwd] * 6 + [st_in, st_in],
        out_specs=[fwd, bwd, st_out, st_out],
        out_shape=[y_shape, y_shape, st_shape, st_shape],
        scratch_shapes=[pltpu.VMEM((N_PAIR, PAIR, PAIR), F32), pltpu.VMEM((N_PAIR, PAIR, PAIR), F32)],
        compiler_params=_cparams(("arbitrary",)),
        name="rwkv_scan",
    )(r, v, kk, lw0, kd0, bb0, r, v, kk, lw1, kd1, bb1, s0f, s0b)


def _gmlp_kernel(uv_ref, lg_ref, lb_ref, ws_ref, bs_ref, o_ref, *, n_chunk):
    u = uv_ref[:, 0:D_B].astype(F32)
    v = uv_ref[:, D_B:2 * D_B].astype(F32)
    mu = jnp.mean(v, axis=-1, keepdims=True)
    vc = v - mu
    var = jnp.mean(vc * vc, axis=-1, keepdims=True)
    vn = ((vc * lax.rsqrt(var + LN_EPS)) * lg_ref[...] + lb_ref[...]).astype(BF16)
    for c in range(n_chunk):
        rows = slice(c * CHUNK, (c + 1) * CHUNK)
        for g in range(G_B):
            cols = slice(g * LANES, (g + 1) * LANES)
            s = _dot(ws_ref[g], vn[rows, cols]) + bs_ref[g]
            o_ref[rows, cols] = (u[rows, cols] * s).astype(o_ref.dtype)


def _gmlp(uv, ln_g, ln_b, ws, bs_b, *, n_chunk=4):
    n = uv.shape[0]
    tm = n_chunk * CHUNK
    full = lambda shape: pl.BlockSpec(shape, lambda i: (0,) * len(shape))
    return pl.pallas_call(
        functools.partial(_gmlp_kernel, n_chunk=n_chunk),
        grid=(n // tm,),
        in_specs=[pl.BlockSpec((tm, 2 * D_B), lambda i: (i, 0)),
                  full((1, D_B)), full((1, D_B)), full((G_B, CHUNK, CHUNK)), full((G_B, CHUNK, LANES))],
        out_specs=pl.BlockSpec((tm, D_B), lambda i: (i, 0)),
        out_shape=jax.ShapeDtypeStruct((n, D_B), BF16),
        compiler_params=_cparams(("parallel",)),
        name="gmlp",
    )(uv, ln_g.reshape(1, D_B), ln_b.reshape(1, D_B), ws, bs_b)


def _merge_kernel(yf_ref, yr_ref, bonus_ref, g_ref, gng_ref, gnb_ref, avg_ref, yb_ref, ga_ref, gb_ref, wpa_ref, wpb_ref,
                  o_ref):
    avg = avg_ref[...]
    parts = []
    for p in range(N_PAIR):
        cols = slice(p * LANES, (p + 1) * LANES)
        yp = yf_ref[:, cols] + yr_ref[:, cols]
        yc = yp - _dot2(yp, avg)
        yn = yc * lax.rsqrt(_dot2(yc * yc, avg) + GN_EPS)
        parts.append(((yn * gng_ref[:, cols] + gnb_ref[:, cols] + bonus_ref[:, cols]) * g_ref[:, cols]).astype(BF16))
    oa = jnp.concatenate(parts, axis=1)
    yb = yb_ref[...]
    for n0 in range(0, o_ref.shape[1], FFN_OUT_CHUNK):
        cols = slice(n0, n0 + FFN_OUT_CHUNK)
        o_ref[:, cols] = (ga_ref[:, cols].astype(F32) * _dot(oa, wpa_ref[:, cols])
                          + gb_ref[:, cols].astype(F32) * _dot(yb, wpb_ref[:, cols])).astype(o_ref.dtype)


def _merge(y_fwd, y_bwd, bonus, g, gn_g, gn_b, yb, act, w_pa, w_pb, *, tm=512):
    n = y_fwd.shape[0]
    head_of_lane = jnp.arange(LANES) // HEAD
    avg = ((head_of_lane[:, None] == head_of_lane[None, :]).astype(F32) / HEAD).astype(BF16)
    full = lambda shape: pl.BlockSpec(shape, lambda i: (0,) * len(shape))
    row = lambda w, c=0: pl.BlockSpec((tm, w), lambda i: (i, c))
    return pl.pallas_call(
        _merge_kernel,
        grid=(n // tm,),
        in_specs=[row(D_A), row(D_A), row(D_A), row(D_A), full((1, D_A)), full((1, D_A)), full((LANES, LANES)),
                  row(D_B), row(D_MODEL, 1), row(D_MODEL, 2), full((D_A, D_MODEL)), full((D_B, D_MODEL))],
        out_specs=row(D_MODEL),
        out_shape=jax.ShapeDtypeStruct((n, D_MODEL), BF16),
        compiler_params=_cparams(("parallel",)),
        name="merge",
    )(y_fwd, y_bwd, bonus, g, gn_g.reshape(1, D_A), gn_b.reshape(1, D_A), avg, yb, act, act, w_pa, w_pb)


def _w1_layout_kernel(a_ref, *refs):
    b_refs, o_ref = refs[:-1], refs[-1]
    valid = D_FF - pl.program_id(1) * D_FF_TILE
    keep = lax.broadcasted_iota(jnp.int32, (1, D_FF_TILE), 1) < valid
    a = jnp.where(keep, a_ref[0], 0.0)
    b = jnp.where(keep, jnp.concatenate([r[0] for r in b_refs], axis=1), 0.0)
    o_ref[0] = jnp.concatenate([a, b], axis=1).astype(BF16)


def _w2_layout_kernel(*refs):
    w_refs, o_ref = refs[:-1], refs[-1]
    valid = D_FF - pl.program_id(1) * D_FF_TILE
    keep = lax.broadcasted_iota(jnp.int32, (D_FF_TILE, 1), 0) < valid
    o_ref[0] = jnp.where(keep, jnp.concatenate([r[0] for r in w_refs], axis=0), 0.0).astype(BF16)


def _ffn_weight_layouts(w_in, w_out):
    s, d, _ = w_in.shape
    nj = D_FF_PAD // D_FF_TILE
    sub = D_FF_TILE // LANES
    first_b = D_FF // LANES
    last_blk = 2 * D_FF // LANES - 1
    b_spec = lambda k: pl.BlockSpec((1, d, LANES), lambda i, j: (i, 0, jnp.minimum(first_b + sub * j + k, last_blk)))
    w1 = pl.pallas_call(
        _w1_layout_kernel,
        grid=(s, nj),
        in_specs=[pl.BlockSpec((1, d, D_FF_TILE), lambda i, j: (i, 0, j))] + [b_spec(k) for k in range(sub)],
        out_specs=pl.BlockSpec((1, d, 2 * D_FF_TILE), lambda i, j: (i, 0, j)),
        out_shape=jax.ShapeDtypeStruct((s, d, 2 * D_FF_PAD), BF16),
        compiler_params=_cparams(("parallel", "parallel")),
        name="w1_layout",
    )(w_in, *([w_in] * sub))
    last_row_blk = D_FF // LANES - 1
    r_spec = lambda k: pl.BlockSpec((1, LANES, d), lambda i, j: (i, jnp.minimum(sub * j + k, last_row_blk), 0))
    w2 = pl.pallas_call(
        _w2_layout_kernel,
        grid=(s, nj),
        in_specs=[r_spec(k) for k in range(sub)],
        out_specs=pl.BlockSpec((1, D_FF_TILE, d), lambda i, j: (i, j, 0)),
        out_shape=jax.ShapeDtypeStruct((s, D_FF_PAD, d), BF16),
        compiler_params=_cparams(("parallel", "parallel")),
        name="w2_layout",
    )(*([w_out] * sub))
    return w1, w2


def _pair_states(s):
    b = s.shape[0]
    s = s.reshape(b, N_PAIR, 2, HEAD, HEAD)
    z = jnp.zeros_like(s[:, :, 0])
    top = jnp.concatenate([s[:, :, 0], z], axis=-1)
    bot = jnp.concatenate([z, s[:, :, 1]], axis=-1)
    return jnp.concatenate([top, bot], axis=-2)


def _unpair_states(sp):
    b = sp.shape[0]
    return jnp.stack([sp[:, :, :HEAD, :HEAD], sp[:, :, HEAD:, HEAD:]], axis=2).reshape(b, H_A, HEAD, HEAD)


def kernel(x_prompt, x_sample, c, state_rwkv_fwd, state_rwkv_bwd, c_ctx, w_ada, b_ada, ln_g, ln_b,
           ffn_w_in, ffn_w_out, w_in, shift_mu, rw_w0, rw_w2, rw_a0, rw_a2, rw_g2, rw_k_k, rw_k_a,
           rw_r_k, rw_gn_g, rw_gn_b, sg_ln_g, sg_ln_b, sg_w, sg_b, w_pa, w_pb, w_o):
    n_ctx_seq, t_ctx, d = x_prompt.shape
    n_lat_seq, t_lat, _ = x_sample.shape
    n_ctx = n_ctx_seq * t_ctx
    n_lat = n_lat_seq * t_lat
    tm = 1024
    ctx_tiles = n_ctx // tm
    lat_tiles_per_seq = t_lat // tm

    def group_of(i):
        return jnp.where(i < ctx_tiles, 0, 1 + (i - ctx_tiles) // lat_tiles_per_seq)

    x = jnp.concatenate([x_prompt.reshape(n_ctx, d), x_sample.reshape(n_lat, d)], axis=0)
    cond8 = jnp.zeros((16, d), F32).at[0].set(c_ctx).at[1:1 + n_lat_seq].set(c)
    new_f, new_b = [], []
    for l in range(DEPTH):
        w1, w2 = _ffn_weight_layouts(ffn_w_in[l], ffn_w_out[l])
        w_zs = jnp.pad(w_in[l][:, :N_SHIFT].astype(BF16), ((0, 0), (0, N_SHIFT_PAD - N_SHIFT)))
        w_act = w_in[l][:, N_SHIFT:].astype(BF16)
        mu_p = jnp.pad(shift_mu[l], (0, N_SHIFT_PAD - N_SHIFT))
        zpad = jnp.zeros((2, LANES - LORA_W, D_A), F32)
        w2p = jnp.concatenate([rw_w2[l], zpad], axis=1).astype(BF16)
        a2p = jnp.concatenate([zpad, rw_a2[l]], axis=1).astype(BF16)
        g2p = jnp.pad(rw_g2[l], ((0, LORA_G_PAD - LORA_G), (0, 0))).astype(BF16)
        bs_b = jnp.broadcast_to(sg_b[l][:, :, None], (G_B, CHUNK, LANES))

        mod = _ada(cond8, w_ada[l], b_ada[l]).reshape(16, 9, d)

        x = _ffn(x, mod, w1, w2, ln_g[l, 0], ln_b[l, 0], group_of, which=0, tm=tm)

        zs = _mm_mod(x, mod, w_zs, group_of, which=1, act=_act_none, tm=tm, tn=1152, tn_out=1152, out_dtype=F32)
        act = _mm_mod(x, mod, w_act, group_of, which=1, act=_act_gelu, act_tail=_act_sigmoid,
                      tail_from=2 * D_B // 1024, tm=tm, tn=1024, tn_out=1024, out_dtype=BF16)
        prep = _rwkv_prep(zs, mu_p, w2p, a2p, g2p, rw_w0[l], rw_a0[l], rw_k_k[l], rw_k_a[l], rw_r_k[l].reshape(D_A),
                          n_ctx=n_ctx, t_ctx=t_ctx, t_lat=t_lat)
        bonus, g_out = prep[9], prep[10]
        assert t_ctx == SCAN_TB
        y_fwd, y_bwd, sf, sb = _rwkv_scan(prep[:9], _pair_states(state_rwkv_fwd[:, l]),
                                          _pair_states(state_rwkv_bwd[:, l]),
                                          n_ctx_seq=n_ctx_seq, n_lat_seq=n_lat_seq, t_lat=t_lat)
        new_f.append(_unpair_states(sf[:n_ctx_seq]))
        new_b.append(_unpair_states(sb[:n_ctx_seq]))
        assert 2 * D_B == D_MODEL
        yb = _gmlp(act, sg_ln_g[l], sg_ln_b[l], sg_w[l].astype(BF16), bs_b)
        merged = _merge(y_fwd, y_bwd, bonus, g_out, rw_gn_g[l], rw_gn_b[l], yb, act,
                        w_pa[l].astype(BF16), w_pb[l].astype(BF16))
        x = _mm_res_ln(merged, w_o[l].astype(BF16), x, mod, ln_g[l, 1], ln_b[l, 1], group_of, gate_row=5, coef=1.0,
                       tm=tm, tk=1024)

        ffn2 = functools.partial(_ffn, x, mod, w1, w2, ln_g[l, 2], ln_b[l, 2], group_of, which=2, tm=tm)
        x_ctx = ffn2(tile0=0, n_tiles=ctx_tiles)
        x_lat = ffn2(tile0=ctx_tiles, n_tiles=n_lat // tm)
        if l + 1 < DEPTH:
            x = jnp.concatenate([x_ctx, x_lat], axis=0)

    y_prompt = x_ctx.reshape(n_ctx_seq, t_ctx, d)
    y_sample = x_lat.reshape(n_lat_seq, t_lat, d)
    new_state_fwd = jnp.stack(new_f, axis=1).astype(x_prompt.dtype)
    new_state_bwd = jnp.stack(new_b, axis=1).astype(x_prompt.dtype)
    return (y_prompt, y_sample, new_state_fwd, new_state_bwd)
```

```python
import functools

import numpy as np
import jax
import jax.numpy as jnp
from jax import lax
from jax.experimental import pallas as pl
from jax.experimental.pallas import tpu as pltpu

F32 = jnp.float32
BF16 = jnp.bfloat16

D_MODEL = 2048
DEPTH = 1
D_A = 1024
HEAD = 64
H_A = D_A // HEAD
LORA_W = 64
LORA_A = 64
LORA_G = 160
D_B = 1024
CHUNK = 128
G_B = 8
D_FF = 5504
N_SHIFT = 3 * D_A + LORA_W + LORA_A + LORA_G
GRID_W = 64
ALPHA = (2.0 * DEPTH) ** 0.25
LN_EPS = 1e-5
GN_EPS = 64e-5
NRM_EPS = 1e-12

LANES = 128
VMEM_LIMIT = 56 * 1024 * 1024

LORA_G_PAD = 256
N_SHIFT_PAD = 3 * D_A + LORA_W + LORA_A + LORA_G_PAD
D_FF_TILE = 512
D_FF_PAD = ((D_FF + D_FF_TILE - 1) // D_FF_TILE) * D_FF_TILE
FFN_OUT_CHUNK = 512
LN_ROWS = 256
MM_ACT_CHUNK = 256
SCAN_C = 64
SCAN_TB = 256
PAIR = 2 * HEAD
N_PAIR = H_A // 2


def _cparams(sem):
    return pltpu.CompilerParams(dimension_semantics=sem, vmem_limit_bytes=VMEM_LIMIT)


def _dot(a, b):
    return jnp.dot(a, b, preferred_element_type=F32)


def _dot_nt(a, b):
    return lax.dot_general(a, b, (((1,), (1,)), ((), ())), preferred_element_type=F32)


def _dot2(x, m):
    hi = x.astype(BF16)
    lo = (x - hi.astype(F32)).astype(BF16)
    return _dot(hi, m) + _dot(lo, m)


def _ada_kernel(c_ref, w_ref, b_ref, o_ref):
    c = c_ref[...]
    s = (c * jax.nn.sigmoid(c)).astype(BF16)
    o_ref[...] = _dot(s, w_ref[...].astype(BF16)) + b_ref[...]


def _ada(cond8, w_ada, b_ada, tn=1024):
    rows, d = cond8.shape
    n = w_ada.shape[1]
    return pl.pallas_call(
        _ada_kernel,
        grid=(n // tn,),
        in_specs=[pl.BlockSpec((rows, d), lambda j: (0, 0)),
                  pl.BlockSpec((d, tn), lambda j: (0, j)),
                  pl.BlockSpec((1, tn), lambda j: (0, j))],
        out_specs=pl.BlockSpec((rows, tn), lambda j: (0, j)),
        out_shape=jax.ShapeDtypeStruct((rows, n), F32),
        compiler_params=_cparams(("parallel",)),
        name="ada",
    )(cond8, w_ada, b_ada.reshape(1, n))


def _act_none(acc):
    return acc


def _act_gelu(acc):
    return 0.5 * acc * (1.0 + lax.erf(acc * (2.0 ** -0.5)))


def _act_sigmoid(acc):
    return jax.nn.sigmoid(acc)


def _mm_mod_kernel(x_ref, mod_ref, w_ref, o_ref, h_ref, *, which, act, act_tail, tail_from):
    j = pl.program_id(1)

    @pl.when(j == 0)
    def _():
        shift = mod_ref[0, 3 * which:3 * which + 1, :]
        scale = mod_ref[0, 3 * which + 1:3 * which + 2, :]
        h_ref[...] = (x_ref[...] * (1.0 + scale) + shift).astype(BF16)

    def tile(fn):
        step = o_ref.shape[1] if fn is _act_none else MM_ACT_CHUNK
        for n0 in range(0, o_ref.shape[1], step):
            o_ref[:, n0:n0 + step] = fn(_dot(h_ref[...], w_ref[:, n0:n0 + step])).astype(o_ref.dtype)

    if act_tail is None:
        tile(act)
    else:
        pl.when(j < tail_from)(lambda: tile(act))
        pl.when(j >= tail_from)(lambda: tile(act_tail))


def _mm_mod(x, mod3, w, group_of, *, which, act, tm, tn, tn_out, out_dtype, act_tail=None, tail_from=0):
    n, d = x.shape
    nj = w.shape[1] // tn
    return pl.pallas_call(
        functools.partial(_mm_mod_kernel, which=which, act=act, act_tail=act_tail, tail_from=tail_from),
        grid=(n // tm, nj),
        in_specs=[pl.BlockSpec((tm, d), lambda i, j: (i, 0)),
                  pl.BlockSpec((1, 9, d), lambda i, j: (group_of(i), 0, 0)),
                  pl.BlockSpec((d, tn), lambda i, j: (0, j))],
        out_specs=pl.BlockSpec((tm, tn_out), lambda i, j: (i, j)),
        out_shape=jax.ShapeDtypeStruct((n, nj * tn_out), out_dtype),
        scratch_shapes=[pltpu.VMEM((tm, d), BF16)],
        compiler_params=_cparams(("parallel", "arbitrary")),
        name="mm_mod",
    )(x, mod3, w)


def _mm_res_ln_kernel(a_ref, w_ref, x_ref, mod_ref, g_ref, b_ref, o_ref, *, gate_row, coef, nk):
    k = pl.program_id(1)

    @pl.when(k == 0)
    def _():
        o_ref[...] = jnp.zeros_like(o_ref)

    for n0 in range(0, o_ref.shape[1], FFN_OUT_CHUNK):
        o_ref[:, n0:n0 + FFN_OUT_CHUNK] += _dot(a_ref[...], w_ref[:, n0:n0 + FFN_OUT_CHUNK])

    @pl.when(k == nk - 1)
    def _():
        gate = coef * mod_ref[0, gate_row:gate_row + 1, :]
        for r0 in range(0, o_ref.shape[0], LN_ROWS):
            rows = slice(r0, r0 + LN_ROWS)
            o_ref[rows, :] = _res_ln(x_ref[rows, :], gate * o_ref[rows, :], g_ref[...], b_ref[...])


def _mm_res_ln(a, w, x, mod3, ln_g, ln_b, group_of, *, gate_row, coef, tm, tk):
    n, kdim = a.shape
    d = w.shape[1]
    nk = kdim // tk
    return pl.pallas_call(
        functools.partial(_mm_res_ln_kernel, gate_row=gate_row, coef=coef, nk=nk),
        grid=(n // tm, nk),
        in_specs=[pl.BlockSpec((tm, tk), lambda i, k: (i, k)),
                  pl.BlockSpec((tk, d), lambda i, k: (k, 0)),
                  pl.BlockSpec((tm, d), lambda i, k: (i, 0)),
                  pl.BlockSpec((1, 9, d), lambda i, k: (group_of(i), 0, 0)),
                  pl.BlockSpec((1, d), lambda i, k: (0, 0)),
                  pl.BlockSpec((1, d), lambda i, k: (0, 0))],
        out_specs=pl.BlockSpec((tm, d), lambda i, k: (i, 0)),
        out_shape=jax.ShapeDtypeStruct((n, d), F32),
        compiler_params=_cparams(("parallel", "arbitrary")),
        name="mm_res_ln",
    )(a, w, x, mod3, ln_g.reshape(1, d), ln_b.reshape(1, d))


def _ffn_kernel(x_ref, mod_ref, w1_ref, w2_ref, g_ref, b_ref, o_ref, h_ref, *, which, nj):
    j = pl.program_id(1)
    assert nj >= 2
    half = h_ref.shape[0] // 2

    def step(first, last):
        for r0 in (0, half):
            rows = slice(r0, r0 + half)
            if first:
                shift = mod_ref[0, 3 * which:3 * which + 1, :]
                scale = mod_ref[0, 3 * which + 1:3 * which + 2, :]
                h_ref[rows, :] = (x_ref[rows, :] * (1.0 + scale) + shift).astype(BF16)
            h = h_ref[rows, :]
            a = _dot(h, w1_ref[0, :, 0:D_FF_TILE])
            hid = ((a * jax.nn.sigmoid(a)) * _dot(h, w1_ref[0, :, D_FF_TILE:2 * D_FF_TILE])).astype(BF16)
            for n0 in range(0, o_ref.shape[1], FFN_OUT_CHUNK):
                cols = slice(n0, n0 + FFN_OUT_CHUNK)
                part = _dot(hid, w2_ref[0, :, cols])
                if first:
                    o_ref[rows, cols] = part
                else:
                    o_ref[rows, cols] += part
            if last:
                gate = 0.5 * mod_ref[0, 3 * which + 2:3 * which + 3, :]
                for q0 in range(r0, r0 + half, LN_ROWS):
                    q = slice(q0, q0 + LN_ROWS)
                    o_ref[q, :] = _res_ln(x_ref[q, :], gate * o_ref[q, :], g_ref[...], b_ref[...])

    pl.when(j == 0)(lambda: step(True, False))
    pl.when(jnp.logical_and(j > 0, j < nj - 1))(lambda: step(False, False))
    pl.when(j == nj - 1)(lambda: step(False, True))


def _res_ln(x, f, g, b):
    y = ALPHA * x + f
    mu = jnp.mean(y, axis=-1, keepdims=True)
    yc = y - mu
    var = jnp.mean(yc * yc, axis=-1, keepdims=True)
    return yc * lax.rsqrt(var + LN_EPS) * g + b


def _ffn(x, mod3, w1, w2, ln_g, ln_b, group_of, *, which, tm, tile0=0, n_tiles=None):
    n, d = x.shape
    n_tiles = n // tm if n_tiles is None else n_tiles
    nj = w2.shape[1] // D_FF_TILE
    s = which // 2
    return pl.pallas_call(
        functools.partial(_ffn_kernel, which=which, nj=nj),
        grid=(n_tiles, nj),
        in_specs=[pl.BlockSpec((tm, d), lambda i, j: (tile0 + i, 0)),
                  pl.BlockSpec((1, 9, d), lambda i, j: (group_of(tile0 + i), 0, 0)),
                  pl.BlockSpec((1, d, 2 * D_FF_TILE), lambda i, j: (s, 0, j)),
                  pl.BlockSpec((1, D_FF_TILE, d), lambda i, j: (s, j, 0)),
                  pl.BlockSpec((1, d), lambda i, j: (0, 0)),
                  pl.BlockSpec((1, d), lambda i, j: (0, 0))],
        out_specs=pl.BlockSpec((tm, d), lambda i, j: (i, 0)),
        out_shape=jax.ShapeDtypeStruct((n_tiles * tm, d), F32),
        scratch_shapes=[pltpu.VMEM((tm, d), BF16)],
        compiler_params=_cparams(("parallel", "arbitrary")),
        name="ffn",
    )(x, mod3, w1, w2, ln_g.reshape(1, d), ln_b.reshape(1, d))


def _prep_kernel(z_ref, za_ref, zb_ref, nbw_ref, mu_ref, w2_ref, a2_ref, g2_ref, w0_ref, a0_ref, kk_ref_, ka_ref, rk_ref,
                 seg_ref, r_o, v_o, kk_o, lw0_o, kd0_o, bb0_o, lw1_o, kd1_o, bb1_o, bonus_o, g_o, zsh):
    z = z_ref[...]
    stacked = jnp.concatenate([za_ref[...], z, zb_ref[...]], axis=0)
    nbw = nbw_ref[0]
    t0 = stacked.astype(BF16)
    t1 = (stacked - t0.astype(F32)).astype(BF16)
    nb = _dot(nbw, t0) + _dot(nbw, t1)
    zsh[...] = z + mu_ref[...] * (nb - z)

    r = zsh[:, 0:D_A]
    k = zsh[:, D_A:2 * D_A]
    v = zsh[:, 2 * D_A:3 * D_A]
    xwa = zsh[:, 3 * D_A:3 * D_A + LANES]
    xg = zsh[:, 3 * D_A + LANES:3 * D_A + LANES + LORA_G_PAD]
    seg = seg_ref[...]
    r_o[...] = r
    v_o[...] = v

    g_o[...] = _dot(jax.nn.sigmoid(xg).astype(BF16), g2_ref[...])

    kk = k * kk_ref_[...]
    ksq = kk * kk
    ss = jnp.concatenate([_dot2(ksq[:, p * LANES:(p + 1) * LANES], seg) for p in range(N_PAIR)], axis=1)
    kk = kk / jnp.maximum(jnp.sqrt(ss), NRM_EPS)
    kk_o[...] = kk

    tw = jnp.tanh(xwa).astype(BF16)
    xa = xwa.astype(BF16)
    k_a = ka_ref[...]
    r_k = rk_ref[...]
    outs = ((lw0_o, kd0_o, bb0_o), (lw1_o, kd1_o, bb1_o))
    rkd = jnp.zeros_like(r)
    for d in range(2):
        lw_o, kd_o, bb_o = outs[d]
        wl = w0_ref[d:d + 1, :] + _dot(tw, w2_ref[d])
        wlog = -(jnp.maximum(-wl, 0.0) + jnp.log(1.0 + jnp.exp(-jnp.abs(wl)))) - 0.5
        lw_o[...] = -jnp.exp(wlog)
        a = jax.nn.sigmoid(a0_ref[d:d + 1, :] + _dot(xa, a2_ref[d]))
        kd = k * (1.0 + (a - 1.0) * k_a)
        kd_o[...] = kd
        bb_o[...] = kk * a
        rkd = rkd + r * kd * r_k
    rs = jnp.concatenate([_dot2(rkd[:, p * LANES:(p + 1) * LANES], seg) for p in range(N_PAIR)], axis=1)
    bonus_o[...] = rs * v


def _neighbour_weights(tm, lat_tiles):
    assert lat_tiles >= 2
    w = np.zeros((4, tm, tm + 2 * GRID_W), np.float32)
    for t in range(tm):
        me = GRID_W + t
        if t > 0:
            w[0, t, me - 1] = 0.5
        if t < tm - 1:
            w[0, t, me + 1] = 0.5
        for variant, (has_above, has_below) in ((1, (False, True)), (2, (True, True)), (3, (True, False))):
            if t % GRID_W > 0:
                w[variant, t, me - 1] = 0.25
            if t % GRID_W < GRID_W - 1:
                w[variant, t, me + 1] = 0.25
            if t >= GRID_W or has_above:
                w[variant, t, me - GRID_W] = 0.25
            if t < tm - GRID_W or has_below:
                w[variant, t, me + GRID_W] = 0.25
    return jnp.asarray(w, BF16)


def _rwkv_prep(zs, mu, w2p, a2p, g2p, w0, a0, k_k, k_a, r_k, *, n_ctx, t_ctx, t_lat):
    n, ncol = zs.shape
    tm = t_ctx
    assert t_lat % tm == 0 and tm % GRID_W == 0 and n_ctx % tm == 0
    halo_per_tile = tm // GRID_W
    n_halo = n // GRID_W
    n_ctx_tiles, lat_tiles = n_ctx // tm, t_lat // tm

    def variant(i):
        jj = lax.rem(jnp.maximum(i - n_ctx_tiles, 0), lat_tiles)
        return jnp.where(i < n_ctx_tiles, 0, jnp.where(jj == 0, 1, jnp.where(jj == lat_tiles - 1, 3, 2)))
    head_of_lane = jnp.arange(LANES) // HEAD
    seg = (head_of_lane[:, None] == head_of_lane[None, :]).astype(BF16)
    full = lambda shape: pl.BlockSpec(shape, lambda i: (0,) * len(shape))
    row = pl.BlockSpec((tm, D_A), lambda i: (i, 0))
    out = jax.ShapeDtypeStruct((n, D_A), F32)
    return pl.pallas_call(
        _prep_kernel,
        grid=(n // tm,),
        in_specs=[pl.BlockSpec((tm, ncol), lambda i: (i, 0)),
                  pl.BlockSpec((GRID_W, ncol), lambda i: (jnp.maximum(i * halo_per_tile - 1, 0), 0)),
                  pl.BlockSpec((GRID_W, ncol), lambda i: (jnp.minimum((i + 1) * halo_per_tile, n_halo - 1), 0)),
                  pl.BlockSpec((1, tm, tm + 2 * GRID_W), lambda i: (variant(i), 0, 0)),
                  full((1, ncol)),
                  full((2, LANES, D_A)), full((2, LANES, D_A)), full((LORA_G_PAD, D_A)),
                  full((2, D_A)), full((2, D_A)), full((1, D_A)), full((1, D_A)), full((1, D_A)),
                  full((LANES, LANES))],
        out_specs=[row] * 11,
        out_shape=[out] * 11,
        scratch_shapes=[pltpu.VMEM((tm, ncol), F32)],
        compiler_params=_cparams(("parallel",)),
        name="rwkv_prep",
    )(zs, zs, zs, _neighbour_weights(tm, lat_tiles), mu.reshape(1, ncol), w2p, a2p, g2p, w0, a0,
      k_k.reshape(1, D_A), k_a.reshape(1, D_A), r_k.reshape(1, D_A), seg)


def _blockdiag(x, lane_a):
    return jnp.concatenate([jnp.where(lane_a, x, 0.0), jnp.where(lane_a, 0.0, x)], axis=0)


def _pair_mm(a, b, lane_a):
    return _dot(a.astype(BF16), _blockdiag(b, lane_a).astype(BF16))


def _scan_chunks(chains):
    cs = SCAN_C
    ids = range(len(chains))
    lane_a = chains[0][6][0]
    bd_mask = chains[0][6][6]
    rows = [pl.ds(pl.multiple_of(ch[0] * cs, cs), cs) for ch in chains]
    lanes = [slice(ch[2] * PAIR, (ch[2] + 1) * PAIR) for ch in chains]
    load = lambda k: [chains[i][3][k][rows[i], lanes[i]] for i in ids]
    r, v, kk, lw, kd, bb = (load(k) for k in range(6))
    tri2, strict, incl, eye2, level_masks = ([chains[i][6][k] for i in ids] for k in range(1, 6))

    def split2(x):
        hi = x.astype(BF16)
        return jnp.concatenate([hi, (x - hi.astype(F32)).astype(BF16)], axis=0)

    lp = [_dot(tri2[i], split2(lw[i])) for i in ids]
    total = [lp[i][0:1, :] if chains[i][1] else lp[i][cs - 1:cs, :] for i in ids]
    p_inv = [jnp.exp(-lp[i]) for i in ids]
    lhs = [jnp.concatenate([kk[i] * jnp.exp(lp[i] - lw[i]), r[i] * jnp.exp(lp[i])], axis=0).astype(BF16)
           for i in ids]
    rhs = [jnp.concatenate([_blockdiag(bb[i] * p_inv[i], lane_a), _blockdiag(kd[i] * p_inv[i], lane_a)],
                           axis=0).astype(BF16) for i in ids]
    gram = [_dot_nt(lhs[i], rhs[i]) for i in ids]
    l_beta = [jnp.where(strict[i], gram[i][0:cs, 0:PAIR], 0.0) for i in ids]
    l_kappa = [jnp.where(strict[i], gram[i][0:cs, PAIR:2 * PAIR], 0.0).astype(BF16) for i in ids]
    m_both = [jnp.concatenate([jnp.where(incl[i], gram[i][cs:2 * cs, PAIR:2 * PAIR], 0.0),
                               jnp.where(incl[i], -gram[i][cs:2 * cs, 0:PAIR], 0.0)], axis=1).astype(BF16)
              for i in ids]

    dinv = [eye2[i] - jnp.where(level_masks[i][0], l_beta[i], 0.0) for i in ids]
    for lvl in range(1, len(level_masks[0])):
        x = [_pair_mm(jnp.where(level_masks[i][lvl], l_beta[i], 0.0), dinv[i], lane_a) for i in ids]
        dinv = [dinv[i] - _pair_mm(dinv[i], x[i], lane_a) for i in ids]

    s0 = [chains[i][4][chains[i][2]] for i in ids]
    from_state = [_dot_nt(lhs[i], s0[i].astype(BF16)) for i in ids]
    v_bd = [_blockdiag(v[i], lane_a).astype(BF16) for i in ids]
    w_rhs = [from_state[i][0:cs] + _dot(l_kappa[i], v_bd[i]) for i in ids]
    u = [_pair_mm(dinv[i], w_rhs[i], lane_a) for i in ids]
    y = [from_state[i][cs:2 * cs]
         + _dot(m_both[i], jnp.concatenate([v_bd[i], _blockdiag(u[i], lane_a).astype(BF16)], axis=0)) for i in ids]
    for i in ids:
        chains[i][5][rows[i], lanes[i]] = y[i]

    to_end = [jnp.exp(total[i] - lp[i]) for i in ids]
    zt = [jnp.concatenate([v[i], -u[i]], axis=0).T.astype(BF16) for i in ids]
    kb_end = [jnp.concatenate([kd[i] * to_end[i], bb[i] * to_end[i]], axis=0).astype(BF16) for i in ids]
    upd = [_dot(zt[i], kb_end[i]) for i in ids]
    for i in ids:
        chains[i][4][chains[i][2]] = s0[i] * jnp.exp(total[i]) + jnp.where(bd_mask, upd[i], 0.0)


def _scan_consts():
    cs = SCAN_C
    lane = lax.broadcasted_iota(jnp.int32, (cs, PAIR), 1)
    t = lax.broadcasted_iota(jnp.int32, (cs, PAIR), 0)
    i = jnp.bitwise_and(lane, HEAD - 1)
    lane_a = lane < HEAD
    out = {}
    for reverse in (False, True):
        before = (i > t) if reverse else (i < t)
        strict = before
        incl = jnp.logical_or(before, i == t)
        tri2 = incl.astype(BF16)
        eye2 = (i == t).astype(F32)
        level_masks = []
        s = 1
        while s < cs:
            sh_s = s.bit_length() - 1
            same_2s = lax.shift_right_logical(t, sh_s + 1) == lax.shift_right_logical(i, sh_s + 1)
            diff_s = lax.shift_right_logical(t, sh_s) != lax.shift_right_logical(i, sh_s)
            level_masks.append(jnp.logical_and(jnp.logical_and(same_2s, diff_s), strict))
            s *= 2
        out[reverse] = (tri2, strict, incl, eye2, tuple(level_masks))
    row = lax.broadcasted_iota(jnp.int32, (PAIR, PAIR), 0)
    col = lax.broadcasted_iota(jnp.int32, (PAIR, PAIR), 1)
    bd_mask = (row < HEAD) == (col < HEAD)
    return lane_a, out, bd_mask


def _scan_kernel(rf, vf, kkf, lw0, kd0, bb0, rb, vb, kkb, lw1, kd1, bb1, s0f_ref, s0b_ref,
                 yf_ref, yb_ref, sf_ref, sb_ref, sf_scr, sb_scr, *, n_ctx_seq, lat_tb):
    g = pl.program_id(0)
    is_ctx = g < n_ctx_seq
    j = lax.rem(jnp.maximum(g - n_ctx_seq, 0), lat_tb)
    n_chunk = SCAN_TB // SCAN_C
    lane_a, per_dir, bd_mask = _scan_consts()
    consts_f = (lane_a,) + per_dir[False] + (bd_mask,)
    consts_b = (lane_a,) + per_dir[True] + (bd_mask,)

    @pl.when(is_ctx)
    def _():
        sf_scr[...] = jnp.zeros_like(sf_scr)
        sb_scr[...] = jnp.zeros_like(sb_scr)

    @pl.when(jnp.logical_and(jnp.logical_not(is_ctx), j == 0))
    def _():
        sf_scr[...] = s0f_ref[0]
        sb_scr[...] = s0b_ref[0]

    def body(c, carry):
        chains = []
        for p in range(N_PAIR):
            chains.append((c, False, p, (rf, vf, kkf, lw0, kd0, bb0), sf_scr, yf_ref, consts_f))
            chains.append((n_chunk - 1 - c, True, p, (rb, vb, kkb, lw1, kd1, bb1), sb_scr, yb_ref, consts_b))
        _scan_chunks(chains)
        return carry

    lax.fori_loop(0, n_chunk, body, 0)

    @pl.when(jnp.logical_or(is_ctx, j == lat_tb - 1))
    def _():
        sf_ref[0] = sf_scr[...]
        sb_ref[0] = sb_scr[...]


def _rwkv_scan(prep, s0f, s0b, *, n_ctx_seq, n_lat_seq, t_lat):
    lat_tb = t_lat // SCAN_TB
    n_blocks = n_ctx_seq + n_lat_seq * lat_tb
    n_seq = n_ctx_seq + n_lat_seq

    def lat_idx(g):
        q = jnp.maximum(g - n_ctx_seq, 0)
        return q // lat_tb, lax.rem(q, lat_tb)

    def bwd_block(g):
        s, j = lat_idx(g)
        return jnp.where(g < n_ctx_seq, g, n_ctx_seq + s * lat_tb + (lat_tb - 1 - j))

    def seq_of(g):
        return jnp.where(g < n_ctx_seq, g, n_ctx_seq + lat_idx(g)[0])

    fwd = pl.BlockSpec((SCAN_TB, D_A), lambda g: (g, 0))
    bwd = pl.BlockSpec((SCAN_TB, D_A), lambda g: (bwd_block(g), 0))
    st_in = pl.BlockSpec((1, N_PAIR, PAIR, PAIR), lambda g: (lat_idx(g)[0], 0, 0, 0))
    st_out = pl.BlockSpec((1, N_PAIR, PAIR, PAIR), lambda g: (seq_of(g), 0, 0, 0))
    st_shape = jax.ShapeDtypeStruct((n_seq, N_PAIR, PAIR, PAIR), F32)
    y_shape = jax.ShapeDtypeStruct((n_blocks * SCAN_TB, D_A), F32)
    r, v, kk, lw0, kd0, bb0, lw1, kd1, bb1 = prep
    return pl.pallas_call(
        functools.partial(_scan_kernel, n_ctx_seq=n_ctx_seq, lat_tb=lat_tb),
        grid=(n_blocks,),
        in_specs=[fwd] * 6 + [bwd] * 6 + [st_in, st_in],
        out_specs=[fwd, bwd, st_out, st_out],
        out_shape=[y_shape, y_shape, st_shape, st_shape],
        scratch_shapes=[pltpu.VMEM((N_PAIR, PAIR, PAIR), F32), pltpu.VMEM((N_PAIR, PAIR, PAIR), F32)],
        compiler_params=_cparams(("arbitrary",)),
        name="rwkv_scan",
    )(r, v, kk, lw0, kd0, bb0, r, v, kk, lw1, kd1, bb1, s0f, s0b)


def _gmlp_kernel(uv_ref, lg_ref, lb_ref, ws_ref, bs_ref, o_ref, *, n_chunk):
    u = uv_ref[:, 0:D_B].astype(F32)
    v = uv_ref[:, D_B:2 * D_B].astype(F32)
    mu = jnp.mean(v, axis=-1, keepdims=True)
    vc = v - mu
    var = jnp.mean(vc * vc, axis=-1, keepdims=True)
    vn = ((vc * lax.rsqrt(var + LN_EPS)) * lg_ref[...] + lb_ref[...]).astype(BF16)
    for c in range(n_chunk):
        rows = slice(c * CHUNK, (c + 1) * CHUNK)
        for g in range(G_B):
            cols = slice(g * LANES, (g + 1) * LANES)
            s = _dot(ws_ref[g], vn[rows, cols]) + bs_ref[g]
            o_ref[rows, cols] = (u[rows, cols] * s).astype(o_ref.dtype)


def _gmlp(uv, ln_g, ln_b, ws, bs_b, *, n_chunk=4):
    n = uv.shape[0]
    tm = n_chunk * CHUNK
    full = lambda shape: pl.BlockSpec(shape, lambda i: (0,) * len(shape))
    return pl.pallas_call(
        functools.partial(_gmlp_kernel, n_chunk=n_chunk),
        grid=(n // tm,),
        in_specs=[pl.BlockSpec((tm, 2 * D_B), lambda i: (i, 0)),
                  full((1, D_B)), full((1, D_B)), full((G_B, CHUNK, CHUNK)), full((G_B, CHUNK, LANES))],
        out_specs=pl.BlockSpec((tm, D_B), lambda i: (i, 0)),
        out_shape=jax.ShapeDtypeStruct((n, D_B), BF16),
        compiler_params=_cparams(("parallel",)),
        name="gmlp",
    )(uv, ln_g.reshape(1, D_B), ln_b.reshape(1, D_B), ws, bs_b)


def _merge_kernel(yf_ref, yr_ref, bonus_ref, g_ref, gng_ref, gnb_ref, avg_ref, yb_ref, ga_ref, gb_ref, wpa_ref, wpb_ref,
                  o_ref):
    avg = avg_ref[...]
    parts = []
    for p in range(N_PAIR):
        cols = slice(p * LANES, (p + 1) * LANES)
        yp = yf_ref[:, cols] + yr_ref[:, cols]
        yc = yp - _dot2(yp, avg)
        yn = yc * lax.rsqrt(_dot2(yc * yc, avg) + GN_EPS)
        parts.append(((yn * gng_ref[:, cols] + gnb_ref[:, cols] + bonus_ref[:, cols]) * g_ref[:, cols]).astype(BF16))
    oa = jnp.concatenate(parts, axis=1)
    yb = yb_ref[...]
    for n0 in range(0, o_ref.shape[1], FFN_OUT_CHUNK):
        cols = slice(n0, n0 + FFN_OUT_CHUNK)
        o_ref[:, cols] = (ga_ref[:, cols].astype(F32) * _dot(oa, wpa_ref[:, cols])
                          + gb_ref[:, cols].astype(F32) * _dot(yb, wpb_ref[:, cols])).astype(o_ref.dtype)


def _merge(y_fwd, y_bwd, bonus, g, gn_g, gn_b, yb, act, w_pa, w_pb, *, tm=512):
    n = y_fwd.shape[0]
    head_of_lane = jnp.arange(LANES) // HEAD
    avg = ((head_of_lane[:, None] == head_of_lane[None, :]).astype(F32) / HEAD).astype(BF16)
    full = lambda shape: pl.BlockSpec(shape, lambda i: (0,) * len(shape))
    row = lambda w, c=0: pl.BlockSpec((tm, w), lambda i: (i, c))
    return pl.pallas_call(
        _merge_kernel,
        grid=(n // tm,),
        in_specs=[row(D_A), row(D_A), row(D_A), row(D_A), full((1, D_A)), full((1, D_A)), full((LANES, LANES)),
                  row(D_B), row(D_MODEL, 1), row(D_MODEL, 2), full((D_A, D_MODEL)), full((D_B, D_MODEL))],
        out_specs=row(D_MODEL),
        out_shape=jax.ShapeDtypeStruct((n, D_MODEL), BF16),
        compiler_params=_cparams(("parallel",)),
        name="merge",
    )(y_fwd, y_bwd, bonus, g, gn_g.reshape(1, D_A), gn_b.reshape(1, D_A), avg, yb, act, act, w_pa, w_pb)


def _w1_layout_kernel(a_ref, *refs):
    b_refs, o_ref = refs[:-1], refs[-1]
    valid = D_FF - pl.program_id(1) * D_FF_TILE
    keep = lax.broadcasted_iota(jnp.int32, (1, D_FF_TILE), 1) < valid
    a = jnp.where(keep, a_ref[0], 0.0)
    b = jnp.where(keep, jnp.concatenate([r[0] for r in b_refs], axis=1), 0.0)
    o_ref[0] = jnp.concatenate([a, b], axis=1).astype(BF16)


def _w2_layout_kernel(*refs):
    w_refs, o_ref = refs[:-1], refs[-1]
    valid = D_FF - pl.program_id(1) * D_FF_TILE
    keep = lax.broadcasted_iota(jnp.int32, (D_FF_TILE, 1), 0) < valid
    o_ref[0] = jnp.where(keep, jnp.concatenate([r[0] for r in w_refs], axis=0), 0.0).astype(BF16)


def _ffn_weight_layouts(w_in, w_out):
    s, d, _ = w_in.shape
    nj = D_FF_PAD // D_FF_TILE
    sub = D_FF_TILE // LANES
    first_b = D_FF // LANES
    last_blk = 2 * D_FF // LANES - 1
    b_spec = lambda k: pl.BlockSpec((1, d, LANES), lambda i, j: (i, 0, jnp.minimum(first_b + sub * j + k, last_blk)))
    w1 = pl.pallas_call(
        _w1_layout_kernel,
        grid=(s, nj),
        in_specs=[pl.BlockSpec((1, d, D_FF_TILE), lambda i, j: (i, 0, j))] + [b_spec(k) for k in range(sub)],
        out_specs=pl.BlockSpec((1, d, 2 * D_FF_TILE), lambda i, j: (i, 0, j)),
        out_shape=jax.ShapeDtypeStruct((s, d, 2 * D_FF_PAD), BF16),
        compiler_params=_cparams(("parallel", "parallel")),
        name="w1_layout",
    )(w_in, *([w_in] * sub))
    last_row_blk = D_FF // LANES - 1
    r_spec = lambda k: pl.BlockSpec((1, LANES, d), lambda i, j: (i, jnp.minimum(sub * j + k, last_row_blk), 0))
    w2 = pl.pallas_call(
        _w2_layout_kernel,
        grid=(s, nj),
        in_specs=[r_spec(k) for k in range(sub)],
        out_specs=pl.BlockSpec((1, D_FF_TILE, d), lambda i, j: (i, j, 0)),
        out_shape=jax.ShapeDtypeStruct((s, D_FF_PAD, d), BF16),
        compiler_params=_cparams(("parallel", "parallel")),
        name="w2_layout",
    )(*([w_out] * sub))
    return w1, w2


def _pair_states(s):
    b = s.shape[0]
    s = s.reshape(b, N_PAIR, 2, HEAD, HEAD)
    z = jnp.zeros_like(s[:, :, 0])
    top = jnp.concatenate([s[:, :, 0], z], axis=-1)
    bot = jnp.concatenate([z, s[:, :, 1]], axis=-1)
    return jnp.concatenate([top, bot], axis=-2)


def _unpair_states(sp):
    b = sp.shape[0]
    return jnp.stack([sp[:, :, :HEAD, :HEAD], sp[:, :, HEAD:, HEAD:]], axis=2).reshape(b, H_A, HEAD, HEAD)


def kernel(x_prompt, x_sample, c, state_rwkv_fwd, state_rwkv_bwd, c_ctx, w_ada, b_ada, ln_g, ln_b,
           ffn_w_in, ffn_w_out, w_in, shift_mu, rw_w0, rw_w2, rw_a0, rw_a2, rw_g2, rw_k_k, rw_k_a,
           rw_r_k, rw_gn_g, rw_gn_b, sg_ln_g, sg_ln_b, sg_w, sg_b, w_pa, w_pb, w_o):
    n_ctx_seq, t_ctx, d = x_prompt.shape
    n_lat_seq, t_lat, _ = x_sample.shape
    n_ctx = n_ctx_seq * t_ctx
    n_lat = n_lat_seq * t_lat
    tm = 1024
    ctx_tiles = n_ctx // tm
    lat_tiles_per_seq = t_lat // tm

    def group_of(i):
        return jnp.where(i < ctx_tiles, 0, 1 + (i - ctx_tiles) // lat_tiles_per_seq)

    x = jnp.concatenate([x_prompt.reshape(n_ctx, d), x_sample.reshape(n_lat, d)], axis=0)
    cond8 = jnp.zeros((16, d), F32).at[0].set(c_ctx).at[1:1 + n_lat_seq].set(c)
    new_f, new_b = [], []
    for l in range(DEPTH):
        w1, w2 = _ffn_weight_layouts(ffn_w_in[l], ffn_w_out[l])
        w_zs = jnp.pad(w_in[l][:, :N_SHIFT].astype(BF16), ((0, 0), (0, N_SHIFT_PAD - N_SHIFT)))
        w_act = w_in[l][:, N_SHIFT:].astype(BF16)
        mu_p = jnp.pad(shift_mu[l], (0, N_SHIFT_PAD - N_SHIFT))
        zpad = jnp.zeros((2, LANES - LORA_W, D_A), F32)
        w2p = jnp.concatenate([rw_w2[l], zpad], axis=1).astype(BF16)
        a2p = jnp.concatenate([zpad, rw_a2[l]], axis=1).astype(BF16)
        g2p = jnp.pad(rw_g2[l], ((0, LORA_G_PAD - LORA_G), (0, 0))).astype(BF16)
        bs_b = jnp.broadcast_to(sg_b[l][:, :, None], (G_B, CHUNK, LANES))

        mod = _ada(cond8, w_ada[l], b_ada[l]).reshape(16, 9, d)

        x = _ffn(x, mod, w1, w2, ln_g[l, 0], ln_b[l, 0], group_of, which=0, tm=tm)

        zs = _mm_mod(x, mod, w_zs, group_of, which=1, act=_act_none, tm=tm, tn=N_SHIFT_PAD // 3,
                     tn_out=N_SHIFT_PAD // 3, out_dtype=F32)
        act = _mm_mod(x, mod, w_act, group_of, which=1, act=_act_gelu, act_tail=_act_sigmoid,
                      tail_from=2 * D_B // 1024, tm=tm, tn=1024, tn_out=1024, out_dtype=BF16)
        prep = _rwkv_prep(zs, mu_p, w2p, a2p, g2p, rw_w0[l], rw_a0[l], rw_k_k[l], rw_k_a[l], rw_r_k[l].reshape(D_A),
                          n_ctx=n_ctx, t_ctx=t_ctx, t_lat=t_lat)
        bonus, g_out = prep[9], prep[10]
        assert t_ctx == SCAN_TB
        y_fwd, y_bwd, sf, sb = _rwkv_scan(prep[:9], _pair_states(state_rwkv_fwd[:, l]),
                                          _pair_states(state_rwkv_bwd[:, l]),
                                          n_ctx_seq=n_ctx_seq, n_lat_seq=n_lat_seq, t_lat=t_lat)
        new_f.append(_unpair_states(sf[:n_ctx_seq]))
        new_b.append(_unpair_states(sb[:n_ctx_seq]))
        assert 2 * D_B == D_MODEL
        yb = _gmlp(act, sg_ln_g[l], sg_ln_b[l], sg_w[l].astype(BF16), bs_b)
        merged = _merge(y_fwd, y_bwd, bonus, g_out, rw_gn_g[l], rw_gn_b[l], yb, act,
                        w_pa[l].astype(BF16), w_pb[l].astype(BF16))
        x = _mm_res_ln(merged, w_o[l].astype(BF16), x, mod, ln_g[l, 1], ln_b[l, 1], group_of, gate_row=5, coef=1.0,
                       tm=tm, tk=1024)

        ffn2 = functools.partial(_ffn, x, mod, w1, w2, ln_g[l, 2], ln_b[l, 2], group_of, which=2, tm=tm)
        x_ctx = ffn2(tile0=0, n_tiles=ctx_tiles)
        x_lat = ffn2(tile0=ctx_tiles, n_tiles=n_lat // tm)
        if l + 1 < DEPTH:
            x = jnp.concatenate([x_ctx, x_lat], axis=0)

    y_prompt = x_ctx.reshape(n_ctx_seq, t_ctx, d)
    y_sample = x_lat.reshape(n_lat_seq, t_lat, d)
    new_state_fwd = jnp.stack(new_f, axis=1).astype(x_prompt.dtype)
    new_state_bwd = jnp.stack(new_b, axis=1).astype(x_prompt.dtype)
    return (y_prompt, y_sample, new_state_fwd, new_state_bwd)
```

```python
import functools

import numpy as np
import jax
import jax.numpy as jnp
from jax import lax
from jax.experimental import pallas as pl
from jax.experimental.pallas import tpu as pltpu

F32 = jnp.float32
BF16 = jnp.bfloat16

D_MODEL = 2048
DEPTH = 1
D_A = 1024
HEAD = 64
H_A = D_A // HEAD
LORA_W = 64
LORA_A = 64
LORA_G = 160
D_B = 1024
CHUNK = 128
G_B = 8
D_FF = 5504
N_SHIFT = 3 * D_A + LORA_W + LORA_A + LORA_G
GRID_W = 64
ALPHA = (2.0 * DEPTH) ** 0.25
LN_EPS = 1e-5
GN_EPS = 64e-5
NRM_EPS = 1e-12

LANES = 128
VMEM_LIMIT = 56 * 1024 * 1024

LORA_G_PAD = 256
N_SHIFT_PAD = 3 * D_A + LORA_W + LORA_A + LORA_G_PAD
D_FF_TILE = 512
D_FF_PAD = ((D_FF + D_FF_TILE - 1) // D_FF_TILE) * D_FF_TILE
FFN_OUT_CHUNK = 512
LN_ROWS = 256
MM_ACT_CHUNK = 256
MM_COPY_CHUNK = 1152
SCAN_C = 64
SCAN_TB = 256
PAIR = 2 * HEAD
N_PAIR = H_A // 2


def _cparams(sem):
    return pltpu.CompilerParams(dimension_semantics=sem, vmem_limit_bytes=VMEM_LIMIT)


def _dot(a, b):
    return jnp.dot(a, b, preferred_element_type=F32)


def _dot_nt(a, b):
    return lax.dot_general(a, b, (((1,), (1,)), ((), ())), preferred_element_type=F32)


def _dot2(x, m):
    hi = x.astype(BF16)
    lo = (x - hi.astype(F32)).astype(BF16)
    return _dot(hi, m) + _dot(lo, m)


def _ada_kernel(c_ref, w_ref, b_ref, o_ref):
    c = c_ref[...]
    s = (c * jax.nn.sigmoid(c)).astype(BF16)
    o_ref[...] = _dot(s, w_ref[...].astype(BF16)) + b_ref[...]


def _ada(cond8, w_ada, b_ada, tn=1024):
    rows, d = cond8.shape
    n = w_ada.shape[1]
    return pl.pallas_call(
        _ada_kernel,
        grid=(n // tn,),
        in_specs=[pl.BlockSpec((rows, d), lambda j: (0, 0)),
                  pl.BlockSpec((d, tn), lambda j: (0, j)),
                  pl.BlockSpec((1, tn), lambda j: (0, j))],
        out_specs=pl.BlockSpec((rows, tn), lambda j: (0, j)),
        out_shape=jax.ShapeDtypeStruct((rows, n), F32),
        compiler_params=_cparams(("parallel",)),
        name="ada",
    )(cond8, w_ada, b_ada.reshape(1, n))


def _act_none(acc):
    return acc


def _act_gelu(acc):
    return 0.5 * acc * (1.0 + lax.erf(acc * (2.0 ** -0.5)))


def _act_sigmoid(acc):
    return jax.nn.sigmoid(acc)


def _mm_mod_kernel(x_ref, mod_ref, w_ref, o_ref, h_ref, *, which, act, act_tail, tail_from):
    j = pl.program_id(1)

    @pl.when(j == 0)
    def _():
        shift = mod_ref[0, 3 * which:3 * which + 1, :]
        scale = mod_ref[0, 3 * which + 1:3 * which + 2, :]
        h_ref[...] = (x_ref[...] * (1.0 + scale) + shift).astype(BF16)

    def tile(fn):
        step = MM_COPY_CHUNK if fn is _act_none else MM_ACT_CHUNK
        for n0 in range(0, o_ref.shape[1], step):
            o_ref[:, n0:n0 + step] = fn(_dot(h_ref[...], w_ref[:, n0:n0 + step])).astype(o_ref.dtype)

    if act_tail is None:
        tile(act)
    else:
        pl.when(j < tail_from)(lambda: tile(act))
        pl.when(j >= tail_from)(lambda: tile(act_tail))


def _mm_mod(x, mod3, w, group_of, *, which, act, tm, tn, tn_out, out_dtype, act_tail=None, tail_from=0):
    n, d = x.shape
    nj = w.shape[1] // tn
    w_mode = dict(pipeline_mode=pl.Buffered(1)) if nj == 1 else {}
    return pl.pallas_call(
        functools.partial(_mm_mod_kernel, which=which, act=act, act_tail=act_tail, tail_from=tail_from),
        grid=(n // tm, nj),
        in_specs=[pl.BlockSpec((tm, d), lambda i, j: (i, 0)),
                  pl.BlockSpec((1, 9, d), lambda i, j: (group_of(i), 0, 0)),
                  pl.BlockSpec((d, tn), lambda i, j: (0, j), **w_mode)],
        out_specs=pl.BlockSpec((tm, tn_out), lambda i, j: (i, j)),
        out_shape=jax.ShapeDtypeStruct((n, nj * tn_out), out_dtype),
        scratch_shapes=[pltpu.VMEM((tm, d), BF16)],
        compiler_params=_cparams(("parallel", "arbitrary")),
        name="mm_mod",
    )(x, mod3, w)


def _mm_res_ln_kernel(a_ref, w_ref, x_ref, mod_ref, g_ref, b_ref, o_ref, *, gate_row, coef, nk):
    k = pl.program_id(1)

    @pl.when(k == 0)
    def _():
        o_ref[...] = jnp.zeros_like(o_ref)

    for n0 in range(0, o_ref.shape[1], FFN_OUT_CHUNK):
        o_ref[:, n0:n0 + FFN_OUT_CHUNK] += _dot(a_ref[...], w_ref[:, n0:n0 + FFN_OUT_CHUNK])

    @pl.when(k == nk - 1)
    def _():
        gate = coef * mod_ref[0, gate_row:gate_row + 1, :]
        for r0 in range(0, o_ref.shape[0], LN_ROWS):
            rows = slice(r0, r0 + LN_ROWS)
            o_ref[rows, :] = _res_ln(x_ref[rows, :], gate * o_ref[rows, :], g_ref[...], b_ref[...])


def _mm_res_ln(a, w, x, mod3, ln_g, ln_b, group_of, *, gate_row, coef, tm, tk):
    n, kdim = a.shape
    d = w.shape[1]
    nk = kdim // tk
    return pl.pallas_call(
        functools.partial(_mm_res_ln_kernel, gate_row=gate_row, coef=coef, nk=nk),
        grid=(n // tm, nk),
        in_specs=[pl.BlockSpec((tm, tk), lambda i, k: (i, k)),
                  pl.BlockSpec((tk, d), lambda i, k: (k, 0)),
                  pl.BlockSpec((tm, d), lambda i, k: (i, 0)),
                  pl.BlockSpec((1, 9, d), lambda i, k: (group_of(i), 0, 0)),
                  pl.BlockSpec((1, d), lambda i, k: (0, 0)),
                  pl.BlockSpec((1, d), lambda i, k: (0, 0))],
        out_specs=pl.BlockSpec((tm, d), lambda i, k: (i, 0)),
        out_shape=jax.ShapeDtypeStruct((n, d), F32),
        compiler_params=_cparams(("parallel", "arbitrary")),
        name="mm_res_ln",
    )(a, w, x, mod3, ln_g.reshape(1, d), ln_b.reshape(1, d))


def _ffn_kernel(x_ref, mod_ref, w1_ref, w2_ref, g_ref, b_ref, o_ref, h_ref, *, which, nj):
    j = pl.program_id(1)
    assert nj >= 2
    half = h_ref.shape[0] // 2

    def step(first, last):
        for r0 in (0, half):
            rows = slice(r0, r0 + half)
            if first:
                shift = mod_ref[0, 3 * which:3 * which + 1, :]
                scale = mod_ref[0, 3 * which + 1:3 * which + 2, :]
                h_ref[rows, :] = (x_ref[rows, :] * (1.0 + scale) + shift).astype(BF16)
            h = h_ref[rows, :]
            a = _dot(h, w1_ref[0, :, 0:D_FF_TILE])
            hid = ((a * jax.nn.sigmoid(a)) * _dot(h, w1_ref[0, :, D_FF_TILE:2 * D_FF_TILE])).astype(BF16)
            for n0 in range(0, o_ref.shape[1], FFN_OUT_CHUNK):
                cols = slice(n0, n0 + FFN_OUT_CHUNK)
                part = _dot(hid, w2_ref[0, :, cols])
                if first:
                    o_ref[rows, cols] = part
                else:
                    o_ref[rows, cols] += part
            if last:
                gate = 0.5 * mod_ref[0, 3 * which + 2:3 * which + 3, :]
                for q0 in range(r0, r0 + half, LN_ROWS):
                    q = slice(q0, q0 + LN_ROWS)
                    o_ref[q, :] = _res_ln(x_ref[q, :], gate * o_ref[q, :], g_ref[...], b_ref[...])

    pl.when(j == 0)(lambda: step(True, False))
    pl.when(jnp.logical_and(j > 0, j < nj - 1))(lambda: step(False, False))
    pl.when(j == nj - 1)(lambda: step(False, True))


def _res_ln(x, f, g, b):
    y = ALPHA * x + f
    mu = jnp.mean(y, axis=-1, keepdims=True)
    yc = y - mu
    var = jnp.mean(yc * yc, axis=-1, keepdims=True)
    return yc * lax.rsqrt(var + LN_EPS) * g + b


def _ffn(x, mod3, w1, w2, ln_g, ln_b, group_of, *, which, tm, tile0=0, n_tiles=None):
    n, d = x.shape
    n_tiles = n // tm if n_tiles is None else n_tiles
    nj = w2.shape[1] // D_FF_TILE
    s = which // 2
    return pl.pallas_call(
        functools.partial(_ffn_kernel, which=which, nj=nj),
        grid=(n_tiles, nj),
        in_specs=[pl.BlockSpec((tm, d), lambda i, j: (tile0 + i, 0)),
                  pl.BlockSpec((1, 9, d), lambda i, j: (group_of(tile0 + i), 0, 0)),
                  pl.BlockSpec((1, d, 2 * D_FF_TILE), lambda i, j: (s, 0, j)),
                  pl.BlockSpec((1, D_FF_TILE, d), lambda i, j: (s, j, 0)),
                  pl.BlockSpec((1, d), lambda i, j: (0, 0)),
                  pl.BlockSpec((1, d), lambda i, j: (0, 0))],
        out_specs=pl.BlockSpec((tm, d), lambda i, j: (i, 0)),
        out_shape=jax.ShapeDtypeStruct((n_tiles * tm, d), F32),
        scratch_shapes=[pltpu.VMEM((tm, d), BF16)],
        compiler_params=_cparams(("parallel", "arbitrary")),
        name="ffn",
    )(x, mod3, w1, w2, ln_g.reshape(1, d), ln_b.reshape(1, d))


def _prep_kernel(z_ref, za_ref, zb_ref, nbw_ref, mu_ref, w2_ref, a2_ref, g2_ref, w0_ref, a0_ref, kk_ref_, ka_ref, rk_ref,
                 seg_ref, r_o, v_o, kk_o, lw0_o, kd0_o, bb0_o, lw1_o, kd1_o, bb1_o, bonus_o, g_o, zsh):
    z = z_ref[...]
    stacked = jnp.concatenate([za_ref[...], z, zb_ref[...]], axis=0)
    nbw = nbw_ref[0]
    t0 = stacked.astype(BF16)
    t1 = (stacked - t0.astype(F32)).astype(BF16)
    nb = _dot(nbw, t0) + _dot(nbw, t1)
    zsh[...] = z + mu_ref[...] * (nb - z)

    r = zsh[:, 0:D_A]
    k = zsh[:, D_A:2 * D_A]
    v = zsh[:, 2 * D_A:3 * D_A]
    xwa = zsh[:, 3 * D_A:3 * D_A + LANES]
    xg = zsh[:, 3 * D_A + LANES:3 * D_A + LANES + LORA_G_PAD]
    seg = seg_ref[...]
    r_o[...] = r
    v_o[...] = v

    g_o[...] = _dot(jax.nn.sigmoid(xg).astype(BF16), g2_ref[...])

    kk = k * kk_ref_[...]
    ksq = kk * kk
    ss = jnp.concatenate([_dot2(ksq[:, p * LANES:(p + 1) * LANES], seg) for p in range(N_PAIR)], axis=1)
    kk = kk / jnp.maximum(jnp.sqrt(ss), NRM_EPS)
    kk_o[...] = kk

    tw = jnp.tanh(xwa).astype(BF16)
    xa = xwa.astype(BF16)
    k_a = ka_ref[...]
    r_k = rk_ref[...]
    outs = ((lw0_o, kd0_o, bb0_o), (lw1_o, kd1_o, bb1_o))
    rkd = jnp.zeros_like(r)
    for d in range(2):
        lw_o, kd_o, bb_o = outs[d]
        wl = w0_ref[d:d + 1, :] + _dot(tw, w2_ref[d])
        wlog = -(jnp.maximum(-wl, 0.0) + jnp.log(1.0 + jnp.exp(-jnp.abs(wl)))) - 0.5
        lw_o[...] = -jnp.exp(wlog)
        a = jax.nn.sigmoid(a0_ref[d:d + 1, :] + _dot(xa, a2_ref[d]))
        kd = k * (1.0 + (a - 1.0) * k_a)
        kd_o[...] = kd
        bb_o[...] = kk * a
        rkd = rkd + r * kd * r_k
    rs = jnp.concatenate([_dot2(rkd[:, p * LANES:(p + 1) * LANES], seg) for p in range(N_PAIR)], axis=1)
    bonus_o[...] = rs * v


def _neighbour_weights(tm, lat_tiles):
    assert lat_tiles >= 2
    w = np.zeros((4, tm, tm + 2 * GRID_W), np.float32)
    for t in range(tm):
        me = GRID_W + t
        if t > 0:
            w[0, t, me - 1] = 0.5
        if t < tm - 1:
            w[0, t, me + 1] = 0.5
        for variant, (has_above, has_below) in ((1, (False, True)), (2, (True, True)), (3, (True, False))):
            if t % GRID_W > 0:
                w[variant, t, me - 1] = 0.25
            if t % GRID_W < GRID_W - 1:
                w[variant, t, me + 1] = 0.25
            if t >= GRID_W or has_above:
                w[variant, t, me - GRID_W] = 0.25
            if t < tm - GRID_W or has_below:
                w[variant, t, me + GRID_W] = 0.25
    return jnp.asarray(w, BF16)


def _rwkv_prep(zs, mu, w2p, a2p, g2p, w0, a0, k_k, k_a, r_k, *, n_ctx, t_ctx, t_lat):
    n, ncol = zs.shape
    tm = t_ctx
    assert t_lat % tm == 0 and tm % GRID_W == 0 and n_ctx % tm == 0
    halo_per_tile = tm // GRID_W
    n_halo = n // GRID_W
    n_ctx_tiles, lat_tiles = n_ctx // tm, t_lat // tm

    def variant(i):
        jj = lax.rem(jnp.maximum(i - n_ctx_tiles, 0), lat_tiles)
        return jnp.where(i < n_ctx_tiles, 0, jnp.where(jj == 0, 1, jnp.where(jj == lat_tiles - 1, 3, 2)))
    head_of_lane = jnp.arange(LANES) // HEAD
    seg = (head_of_lane[:, None] == head_of_lane[None, :]).astype(BF16)
    full = lambda shape: pl.BlockSpec(shape, lambda i: (0,) * len(shape))
    row = pl.BlockSpec((tm, D_A), lambda i: (i, 0))
    out = jax.ShapeDtypeStruct((n, D_A), F32)
    return pl.pallas_call(
        _prep_kernel,
        grid=(n // tm,),
        in_specs=[pl.BlockSpec((tm, ncol), lambda i: (i, 0)),
                  pl.BlockSpec((GRID_W, ncol), lambda i: (jnp.maximum(i * halo_per_tile - 1, 0), 0)),
                  pl.BlockSpec((GRID_W, ncol), lambda i: (jnp.minimum((i + 1) * halo_per_tile, n_halo - 1), 0)),
                  pl.BlockSpec((1, tm, tm + 2 * GRID_W), lambda i: (variant(i), 0, 0)),
                  full((1, ncol)),
                  full((2, LANES, D_A)), full((2, LANES, D_A)), full((LORA_G_PAD, D_A)),
                  full((2, D_A)), full((2, D_A)), full((1, D_A)), full((1, D_A)), full((1, D_A)),
                  full((LANES, LANES))],
        out_specs=[row] * 11,
        out_shape=[out] * 11,
        scratch_shapes=[pltpu.VMEM((tm, ncol), F32)],
        compiler_params=_cparams(("parallel",)),
        name="rwkv_prep",
    )(zs, zs, zs, _neighbour_weights(tm, lat_tiles), mu.reshape(1, ncol), w2p, a2p, g2p, w0, a0,
      k_k.reshape(1, D_A), k_a.reshape(1, D_A), r_k.reshape(1, D_A), seg)


def _blockdiag(x, lane_a):
    return jnp.concatenate([jnp.where(lane_a, x, 0.0), jnp.where(lane_a, 0.0, x)], axis=0)


def _pair_mm(a, b, lane_a):
    return _dot(a.astype(BF16), _blockdiag(b, lane_a).astype(BF16))


def _scan_chunks(chains):
    cs = SCAN_C
    ids = range(len(chains))
    lane_a = chains[0][6][0]
    bd_mask = chains[0][6][6]
    rows = [pl.ds(pl.multiple_of(ch[0] * cs, cs), cs) for ch in chains]
    lanes = [slice(ch[2] * PAIR, (ch[2] + 1) * PAIR) for ch in chains]
    load = lambda k: [chains[i][3][k][rows[i], lanes[i]] for i in ids]
    r, v, kk, lw, kd, bb = (load(k) for k in range(6))
    tri2, strict, incl, eye2, level_masks = ([chains[i][6][k] for i in ids] for k in range(1, 6))

    def split2(x):
        hi = x.astype(BF16)
        return jnp.concatenate([hi, (x - hi.astype(F32)).astype(BF16)], axis=0)

    lp = [_dot(tri2[i], split2(lw[i])) for i in ids]
    total = [lp[i][0:1, :] if chains[i][1] else lp[i][cs - 1:cs, :] for i in ids]
    p_inv = [jnp.exp(-lp[i]) for i in ids]
    lhs = [jnp.concatenate([kk[i] * jnp.exp(lp[i] - lw[i]), r[i] * jnp.exp(lp[i])], axis=0).astype(BF16)
           for i in ids]
    rhs = [jnp.concatenate([_blockdiag(bb[i] * p_inv[i], lane_a), _blockdiag(kd[i] * p_inv[i], lane_a)],
                           axis=0).astype(BF16) for i in ids]
    gram = [_dot_nt(lhs[i], rhs[i]) for i in ids]
    l_beta = [jnp.where(strict[i], gram[i][0:cs, 0:PAIR], 0.0) for i in ids]
    l_kappa = [jnp.where(strict[i], gram[i][0:cs, PAIR:2 * PAIR], 0.0).astype(BF16) for i in ids]
    m_both = [jnp.concatenate([jnp.where(incl[i], gram[i][cs:2 * cs, PAIR:2 * PAIR], 0.0),
                               jnp.where(incl[i], -gram[i][cs:2 * cs, 0:PAIR], 0.0)], axis=1).astype(BF16)
              for i in ids]

    dinv = [eye2[i] - jnp.where(level_masks[i][0], l_beta[i], 0.0) for i in ids]
    for lvl in range(1, len(level_masks[0])):
        x = [_pair_mm(jnp.where(level_masks[i][lvl], l_beta[i], 0.0), dinv[i], lane_a) for i in ids]
        dinv = [dinv[i] - _pair_mm(dinv[i], x[i], lane_a) for i in ids]

    s0 = [chains[i][4][chains[i][2]] for i in ids]
    from_state = [_dot_nt(lhs[i], s0[i].astype(BF16)) for i in ids]
    v_bd = [_blockdiag(v[i], lane_a).astype(BF16) for i in ids]
    w_rhs = [from_state[i][0:cs] + _dot(l_kappa[i], v_bd[i]) for i in ids]
    u = [_pair_mm(dinv[i], w_rhs[i], lane_a) for i in ids]
    y = [from_state[i][cs:2 * cs]
         + _dot(m_both[i], jnp.concatenate([v_bd[i], _blockdiag(u[i], lane_a).astype(BF16)], axis=0)) for i in ids]
    for i in ids:
        chains[i][5][rows[i], lanes[i]] = y[i]

    to_end = [jnp.exp(total[i] - lp[i]) for i in ids]
    zt = [jnp.concatenate([v[i], -u[i]], axis=0).T.astype(BF16) for i in ids]
    kb_end = [jnp.concatenate([kd[i] * to_end[i], bb[i] * to_end[i]], axis=0).astype(BF16) for i in ids]
    upd = [_dot(zt[i], kb_end[i]) for i in ids]
    for i in ids:
        chains[i][4][chains[i][2]] = s0[i] * jnp.exp(total[i]) + jnp.where(bd_mask, upd[i], 0.0)


def _scan_consts():
    cs = SCAN_C
    lane = lax.broadcasted_iota(jnp.int32, (cs, PAIR), 1)
    t = lax.broadcasted_iota(jnp.int32, (cs, PAIR), 0)
    i = jnp.bitwise_and(lane, HEAD - 1)
    lane_a = lane < HEAD
    out = {}
    for reverse in (False, True):
        before = (i > t) if reverse else (i < t)
        strict = before
        incl = jnp.logical_or(before, i == t)
        tri2 = incl.astype(BF16)
        eye2 = (i == t).astype(F32)
        level_masks = []
        s = 1
        while s < cs:
            sh_s = s.bit_length() - 1
            same_2s = lax.shift_right_logical(t, sh_s + 1) == lax.shift_right_logical(i, sh_s + 1)
            diff_s = lax.shift_right_logical(t, sh_s) != lax.shift_right_logical(i, sh_s)
            level_masks.append(jnp.logical_and(jnp.logical_and(same_2s, diff_s), strict))
            s *= 2
        out[reverse] = (tri2, strict, incl, eye2, tuple(level_masks))
    row = lax.broadcasted_iota(jnp.int32, (PAIR, PAIR), 0)
    col = lax.broadcasted_iota(jnp.int32, (PAIR, PAIR), 1)
    bd_mask = (row < HEAD) == (col < HEAD)
    return lane_a, out, bd_mask


def _scan_kernel(rf, vf, kkf, lw0, kd0, bb0, rb, vb, kkb, lw1, kd1, bb1, s0f_ref, s0b_ref,
                 yf_ref, yb_ref, sf_ref, sb_ref, sf_scr, sb_scr, *, n_ctx_seq, lat_tb):
    g = pl.program_id(0)
    is_ctx = g < n_ctx_seq
    j = lax.rem(jnp.maximum(g - n_ctx_seq, 0), lat_tb)
    n_chunk = SCAN_TB // SCAN_C
    lane_a, per_dir, bd_mask = _scan_consts()
    consts_f = (lane_a,) + per_dir[False] + (bd_mask,)
    consts_b = (lane_a,) + per_dir[True] + (bd_mask,)

    @pl.when(is_ctx)
    def _():
        sf_scr[...] = jnp.zeros_like(sf_scr)
        sb_scr[...] = jnp.zeros_like(sb_scr)

    @pl.when(jnp.logical_and(jnp.logical_not(is_ctx), j == 0))
    def _():
        sf_scr[...] = s0f_ref[0]
        sb_scr[...] = s0b_ref[0]

    def body(c, carry):
        chains = []
        for p in range(N_PAIR):
            chains.append((c, False, p, (rf, vf, kkf, lw0, kd0, bb0), sf_scr, yf_ref, consts_f))
            chains.append((n_chunk - 1 - c, True, p, (rb, vb, kkb, lw1, kd1, bb1), sb_scr, yb_ref, consts_b))
        _scan_chunks(chains)
        return carry

    lax.fori_loop(0, n_chunk, body, 0)

    @pl.when(jnp.logical_or(is_ctx, j == lat_tb - 1))
    def _():
        sf_ref[0] = sf_scr[...]
        sb_ref[0] = sb_scr[...]


def _rwkv_scan(prep, s0f, s0b, *, n_ctx_seq, n_lat_seq, t_lat):
    lat_tb = t_lat // SCAN_TB
    n_blocks = n_ctx_seq + n_lat_seq * lat_tb
    n_seq = n_ctx_seq + n_lat_seq

    def lat_idx(g):
        q = jnp.maximum(g - n_ctx_seq, 0)
        return q // lat_tb, lax.rem(q, lat_tb)

    def bwd_block(g):
        s, j = lat_idx(g)
        return jnp.where(g < n_ctx_seq, g, n_ctx_seq + s * lat_tb + (lat_tb - 1 - j))

    def seq_of(g):
        return jnp.where(g < n_ctx_seq, g, n_ctx_seq + lat_idx(g)[0])

    fwd = pl.BlockSpec((SCAN_TB, D_A), lambda g: (g, 0))
    bwd = pl.BlockSpec((SCAN_TB, D_A), lambda g: (bwd_block(g), 0))
    st_in = pl.BlockSpec((1, N_PAIR, PAIR, PAIR), lambda g: (lat_idx(g)[0], 0, 0, 0))
    st_out = pl.BlockSpec((1, N_PAIR, PAIR, PAIR), lambda g: (seq_of(g), 0, 0, 0))
    st_shape = jax.ShapeDtypeStruct((n_seq, N_PAIR, PAIR, PAIR), F32)
    y_shape = jax.ShapeDtypeStruct((n_blocks * SCAN_TB, D_A), F32)
    r, v, kk, lw0, kd0, bb0, lw1, kd1, bb1 = prep
    return pl.pallas_call(
        functools.partial(_scan_kernel, n_ctx_seq=n_ctx_seq, lat_tb=lat_tb),
        grid=(n_blocks,),
        in_specs=[fwd] * 6 + [bwd] * 6 + [st_in, st_in],
        out_specs=[fwd, bwd, st_out, st_out],
        out_shape=[y_shape, y_shape, st_shape, st_shape],
        scratch_shapes=[pltpu.VMEM((N_PAIR, PAIR, PAIR), F32), pltpu.VMEM((N_PAIR, PAIR, PAIR), F32)],
        compiler_params=_cparams(("arbitrary",)),
        name="rwkv_scan",
    )(r, v, kk, lw0, kd0, bb0, r, v, kk, lw1, kd1, bb1, s0f, s0b)


def _gmlp_kernel(uv_ref, lg_ref, lb_ref, ws_ref, bs_ref, o_ref, *, n_chunk):
    u = uv_ref[:, 0:D_B].astype(F32)
    v = uv_ref[:, D_B:2 * D_B].astype(F32)
    mu = jnp.mean(v, axis=-1, keepdims=True)
    vc = v - mu
    var = jnp.mean(vc * vc, axis=-1, keepdims=True)
    vn = ((vc * lax.rsqrt(var + LN_EPS)) * lg_ref[...] + lb_ref[...]).astype(BF16)
    for c in range(n_chunk):
        rows = slice(c * CHUNK, (c + 1) * CHUNK)
        for g in range(G_B):
            cols = slice(g * LANES, (g + 1) * LANES)
            s = _dot(ws_ref[g], vn[rows, cols]) + bs_ref[g]
            o_ref[rows, cols] = (u[rows, cols] * s).astype(o_ref.dtype)


def _gmlp(uv, ln_g, ln_b, ws, bs_b, *, n_chunk=4):
    n = uv.shape[0]
    tm = n_chunk * CHUNK
    full = lambda shape: pl.BlockSpec(shape, lambda i: (0,) * len(shape))
    return pl.pallas_call(
        functools.partial(_gmlp_kernel, n_chunk=n_chunk),
        grid=(n // tm,),
        in_specs=[pl.BlockSpec((tm, 2 * D_B), lambda i: (i, 0)),
                  full((1, D_B)), full((1, D_B)), full((G_B, CHUNK, CHUNK)), full((G_B, CHUNK, LANES))],
        out_specs=pl.BlockSpec((tm, D_B), lambda i: (i, 0)),
        out_shape=jax.ShapeDtypeStruct((n, D_B), BF16),
        compiler_params=_cparams(("parallel",)),
        name="gmlp",
    )(uv, ln_g.reshape(1, D_B), ln_b.reshape(1, D_B), ws, bs_b)


def _merge_kernel(yf_ref, yr_ref, bonus_ref, g_ref, gng_ref, gnb_ref, avg_ref, yb_ref, ga_ref, gb_ref, wpa_ref, wpb_ref,
                  o_ref):
    avg = avg_ref[...]
    parts = []
    for p in range(N_PAIR):
        cols = slice(p * LANES, (p + 1) * LANES)
        yp = yf_ref[:, cols] + yr_ref[:, cols]
        yc = yp - _dot2(yp, avg)
        yn = yc * lax.rsqrt(_dot2(yc * yc, avg) + GN_EPS)
        parts.append(((yn * gng_ref[:, cols] + gnb_ref[:, cols] + bonus_ref[:, cols]) * g_ref[:, cols]).astype(BF16))
    oa = jnp.concatenate(parts, axis=1)
    yb = yb_ref[...]
    for n0 in range(0, o_ref.shape[1], FFN_OUT_CHUNK):
        cols = slice(n0, n0 + FFN_OUT_CHUNK)
        o_ref[:, cols] = (ga_ref[:, cols].astype(F32) * _dot(oa, wpa_ref[:, cols])
                          + gb_ref[:, cols].astype(F32) * _dot(yb, wpb_ref[:, cols])).astype(o_ref.dtype)


def _merge(y_fwd, y_bwd, bonus, g, gn_g, gn_b, yb, act, w_pa, w_pb, *, tm=512):
    n = y_fwd.shape[0]
    head_of_lane = jnp.arange(LANES) // HEAD
    avg = ((head_of_lane[:, None] == head_of_lane[None, :]).astype(F32) / HEAD).astype(BF16)
    full = lambda shape: pl.BlockSpec(shape, lambda i: (0,) * len(shape))
    row = lambda w, c=0: pl.BlockSpec((tm, w), lambda i: (i, c))
    return pl.pallas_call(
        _merge_kernel,
        grid=(n // tm,),
        in_specs=[row(D_A), row(D_A), row(D_A), row(D_A), full((1, D_A)), full((1, D_A)), full((LANES, LANES)),
                  row(D_B), row(D_MODEL, 1), row(D_MODEL, 2), full((D_A, D_MODEL)), full((D_B, D_MODEL))],
        out_specs=row(D_MODEL),
        out_shape=jax.ShapeDtypeStruct((n, D_MODEL), BF16),
        compiler_params=_cparams(("parallel",)),
        name="merge",
    )(y_fwd, y_bwd, bonus, g, gn_g.reshape(1, D_A), gn_b.reshape(1, D_A), avg, yb, act, act, w_pa, w_pb)


def _w1_layout_kernel(a_ref, *refs):
    b_refs, o_ref = refs[:-1], refs[-1]
    valid = D_FF - pl.program_id(1) * D_FF_TILE
    keep = lax.broadcasted_iota(jnp.int32, (1, D_FF_TILE), 1) < valid
    a = jnp.where(keep, a_ref[0], 0.0)
    b = jnp.where(keep, jnp.concatenate([r[0] for r in b_refs], axis=1), 0.0)
    o_ref[0] = jnp.concatenate([a, b], axis=1).astype(BF16)


def _w2_layout_kernel(*refs):
    w_refs, o_ref = refs[:-1], refs[-1]
    valid = D_FF - pl.program_id(1) * D_FF_TILE
    keep = lax.broadcasted_iota(jnp.int32, (D_FF_TILE, 1), 0) < valid
    o_ref[0] = jnp.where(keep, jnp.concatenate([r[0] for r in w_refs], axis=0), 0.0).astype(BF16)


def _ffn_weight_layouts(w_in, w_out):
    s, d, _ = w_in.shape
    nj = D_FF_PAD // D_FF_TILE
    sub = D_FF_TILE // LANES
    first_b = D_FF // LANES
    last_blk = 2 * D_FF // LANES - 1
    b_spec = lambda k: pl.BlockSpec((1, d, LANES), lambda i, j: (i, 0, jnp.minimum(first_b + sub * j + k, last_blk)))
    w1 = pl.pallas_call(
        _w1_layout_kernel,
        grid=(s, nj),
        in_specs=[pl.BlockSpec((1, d, D_FF_TILE), lambda i, j: (i, 0, j))] + [b_spec(k) for k in range(sub)],
        out_specs=pl.BlockSpec((1, d, 2 * D_FF_TILE), lambda i, j: (i, 0, j)),
        out_shape=jax.ShapeDtypeStruct((s, d, 2 * D_FF_PAD), BF16),
        compiler_params=_cparams(("parallel", "parallel")),
        name="w1_layout",
    )(w_in, *([w_in] * sub))
    last_row_blk = D_FF // LANES - 1
    r_spec = lambda k: pl.BlockSpec((1, LANES, d), lambda i, j: (i, jnp.minimum(sub * j + k, last_row_blk), 0))
    w2 = pl.pallas_call(
        _w2_layout_kernel,
        grid=(s, nj),
        in_specs=[r_spec(k) for k in range(sub)],
        out_specs=pl.BlockSpec((1, D_FF_TILE, d), lambda i, j: (i, j, 0)),
        out_shape=jax.ShapeDtypeStruct((s, D_FF_PAD, d), BF16),
        compiler_params=_cparams(("parallel", "parallel")),
        name="w2_layout",
    )(*([w_out] * sub))
    return w1, w2


def _shifted_cols_kernel(*refs, lane0):
    in_refs, o_ref = refs[:-1], refs[-1]
    take_first = lax.broadcasted_iota(jnp.int32, (1, LANES), 1) < LANES - lane0
    rolled = [pltpu.roll(r[...], LANES - lane0, 1) for r in in_refs]
    o_ref[...] = jnp.concatenate([jnp.where(take_first, rolled[k], rolled[k + 1]) for k in range(len(in_refs) - 1)],
                                 axis=1).astype(BF16)


def _cast_cols_from(w, col0, *, tn=512):
    d, ncol = w.shape
    n_out = ncol - col0
    blk0, lane0 = divmod(col0, LANES)
    assert n_out % tn == 0 and lane0 > 0
    sub = tn // LANES
    last_blk = (ncol - 1) // LANES
    spec = lambda k: pl.BlockSpec((d, LANES), lambda j: (0, jnp.minimum(blk0 + sub * j + k, last_blk)))
    return pl.pallas_call(
        functools.partial(_shifted_cols_kernel, lane0=lane0),
        grid=(n_out // tn,),
        in_specs=[spec(k) for k in range(sub + 1)],
        out_specs=pl.BlockSpec((d, tn), lambda j: (0, j)),
        out_shape=jax.ShapeDtypeStruct((d, n_out), BF16),
        compiler_params=_cparams(("parallel",)),
        name="cast_cols",
    )(*([w] * (sub + 1)))


def _pair_states(s):
    b = s.shape[0]
    s = s.reshape(b, N_PAIR, 2, HEAD, HEAD)
    z = jnp.zeros_like(s[:, :, 0])
    top = jnp.concatenate([s[:, :, 0], z], axis=-1)
    bot = jnp.concatenate([z, s[:, :, 1]], axis=-1)
    return jnp.concatenate([top, bot], axis=-2)


def _unpair_states(sp):
    b = sp.shape[0]
    return jnp.stack([sp[:, :, :HEAD, :HEAD], sp[:, :, HEAD:, HEAD:]], axis=2).reshape(b, H_A, HEAD, HEAD)


def kernel(x_prompt, x_sample, c, state_rwkv_fwd, state_rwkv_bwd, c_ctx, w_ada, b_ada, ln_g, ln_b,
           ffn_w_in, ffn_w_out, w_in, shift_mu, rw_w0, rw_w2, rw_a0, rw_a2, rw_g2, rw_k_k, rw_k_a,
           rw_r_k, rw_gn_g, rw_gn_b, sg_ln_g, sg_ln_b, sg_w, sg_b, w_pa, w_pb, w_o):
    n_ctx_seq, t_ctx, d = x_prompt.shape
    n_lat_seq, t_lat, _ = x_sample.shape
    n_ctx = n_ctx_seq * t_ctx
    n_lat = n_lat_seq * t_lat
    tm = 1024
    ctx_tiles = n_ctx // tm

    def group_of_tile(rows):
        assert n_ctx % rows == 0 and t_lat % rows == 0
        return lambda i: jnp.where(i < n_ctx // rows, 0, 1 + (i - n_ctx // rows) // (t_lat // rows))

    group_of = group_of_tile(tm)

    x = jnp.concatenate([x_prompt.reshape(n_ctx, d), x_sample.reshape(n_lat, d)], axis=0)
    cond8 = jnp.zeros((16, d), F32).at[0].set(c_ctx).at[1:1 + n_lat_seq].set(c)
    new_f, new_b = [], []
    for l in range(DEPTH):
        w1, w2 = _ffn_weight_layouts(ffn_w_in[l], ffn_w_out[l])
        w_zs = jnp.pad(w_in[l][:, :N_SHIFT].astype(BF16), ((0, 0), (0, N_SHIFT_PAD - N_SHIFT)))
        w_act = _cast_cols_from(w_in[l], N_SHIFT)
        mu_p = jnp.pad(shift_mu[l], (0, N_SHIFT_PAD - N_SHIFT))
        zpad = jnp.zeros((2, LANES - LORA_W, D_A), F32)
        w2p = jnp.concatenate([rw_w2[l], zpad], axis=1).astype(BF16)
        a2p = jnp.concatenate([zpad, rw_a2[l]], axis=1).astype(BF16)
        g2p = jnp.pad(rw_g2[l], ((0, LORA_G_PAD - LORA_G), (0, 0))).astype(BF16)
        bs_b = jnp.broadcast_to(sg_b[l][:, :, None], (G_B, CHUNK, LANES))

        mod = _ada(cond8, w_ada[l], b_ada[l]).reshape(16, 9, d)

        x = _ffn(x, mod, w1, w2, ln_g[l, 0], ln_b[l, 0], group_of, which=0, tm=tm)

        zs = _mm_mod(x, mod, w_zs, group_of_tile(tm // 2), which=1, act=_act_none, tm=tm // 2, tn=N_SHIFT_PAD,
                     tn_out=N_SHIFT_PAD, out_dtype=F32)
        act = _mm_mod(x, mod, w_act, group_of, which=1, act=_act_gelu, act_tail=_act_sigmoid,
                      tail_from=2 * D_B // 1024, tm=tm, tn=1024, tn_out=1024, out_dtype=BF16)
        prep = _rwkv_prep(zs, mu_p, w2p, a2p, g2p, rw_w0[l], rw_a0[l], rw_k_k[l], rw_k_a[l], rw_r_k[l].reshape(D_A),
                          n_ctx=n_ctx, t_ctx=t_ctx, t_lat=t_lat)
        bonus, g_out = prep[9], prep[10]
        assert t_ctx == SCAN_TB
        y_fwd, y_bwd, sf, sb = _rwkv_scan(prep[:9], _pair_states(state_rwkv_fwd[:, l]),
                                          _pair_states(state_rwkv_bwd[:, l]),
                                          n_ctx_seq=n_ctx_seq, n_lat_seq=n_lat_seq, t_lat=t_lat)
        new_f.append(_unpair_states(sf[:n_ctx_seq]))
        new_b.append(_unpair_states(sb[:n_ctx_seq]))
        assert 2 * D_B == D_MODEL
        yb = _gmlp(act, sg_ln_g[l], sg_ln_b[l], sg_w[l].astype(BF16), bs_b)
        merged = _merge(y_fwd, y_bwd, bonus, g_out, rw_gn_g[l], rw_gn_b[l], yb, act,
                        w_pa[l].astype(BF16), w_pb[l].astype(BF16))
        x = _mm_res_ln(merged, w_o[l].astype(BF16), x, mod, ln_g[l, 1], ln_b[l, 1], group_of, gate_row=5, coef=1.0,
                       tm=tm, tk=1024)

        ffn2 = functools.partial(_ffn, x, mod, w1, w2, ln_g[l, 2], ln_b[l, 2], group_of, which=2, tm=tm)
        x_ctx = ffn2(tile0=0, n_tiles=ctx_tiles)
        x_lat = ffn2(tile0=ctx_tiles, n_tiles=n_lat // tm)
        if l + 1 < DEPTH:
            x = jnp.concatenate([x_ctx, x_lat], axis=0)

    y_prompt = x_ctx.reshape(n_ctx_seq, t_ctx, d)
    y_sample = x_lat.reshape(n_lat_seq, t_lat, d)
    new_state_fwd = jnp.stack(new_f, axis=1).astype(x_prompt.dtype)
    new_state_bwd = jnp.stack(new_b, axis=1).astype(x_prompt.dtype)
    return (y_prompt, y_sample, new_state_fwd, new_state_bwd)
```

```python
import functools

import numpy as np
import jax
import jax.numpy as jnp
from jax import lax
from jax.experimental import pallas as pl
from jax.experimental.pallas import tpu as pltpu

F32 = jnp.float32
BF16 = jnp.bfloat16

D_MODEL = 2048
DEPTH = 1
D_A = 1024
HEAD = 64
H_A = D_A // HEAD
LORA_W = 64
LORA_A = 64
LORA_G = 160
D_B = 1024
CHUNK = 128
G_B = 8
D_FF = 5504
N_SHIFT = 3 * D_A + LORA_W + LORA_A + LORA_G
GRID_W = 64
ALPHA = (2.0 * DEPTH) ** 0.25
LN_EPS = 1e-5
GN_EPS = 64e-5
NRM_EPS = 1e-12

LANES = 128
VMEM_LIMIT = 56 * 1024 * 1024

LORA_G_PAD = 256
N_SHIFT_PAD = 3 * D_A + LORA_W + LORA_A + LORA_G_PAD
D_FF_TILE = 512
D_FF_PAD = ((D_FF + D_FF_TILE - 1) // D_FF_TILE) * D_FF_TILE
FFN_OUT_CHUNK = 512
LN_ROWS = 256
MM_ACT_CHUNK = 256
MM_COPY_CHUNK = 1152
ROW_TILE = 1024
HALF_ROW_TILE = ROW_TILE // 2
ADA_TN = 1024
ACT_TN = 1024
WO_TK = 1024
GMLP_CHUNKS = 4
CAST_HEAD_TN = 384
CAST_SHIFT_TN = 512
SCAN_C = 64
SCAN_TB = 256
PAIR = 2 * HEAD
N_PAIR = H_A // 2


def _cparams(sem):
    return pltpu.CompilerParams(dimension_semantics=sem, vmem_limit_bytes=VMEM_LIMIT)


def _dot(a, b):
    return jnp.dot(a, b, preferred_element_type=F32)


def _dot_nt(a, b):
    return lax.dot_general(a, b, (((1,), (1,)), ((), ())), preferred_element_type=F32)


def _dot2(x, m):
    hi = x.astype(BF16)
    lo = (x - hi.astype(F32)).astype(BF16)
    return _dot(hi, m) + _dot(lo, m)


def _ada_kernel(c_ref, w_ref, b_ref, o_ref):
    c = c_ref[...]
    s = (c * jax.nn.sigmoid(c)).astype(BF16)
    o_ref[...] = _dot(s, w_ref[...].astype(BF16)) + b_ref[...]


def _ada(cond8, w_ada, b_ada, tn=ADA_TN):
    rows, d = cond8.shape
    n = w_ada.shape[1]
    return pl.pallas_call(
        _ada_kernel,
        grid=(n // tn,),
        in_specs=[pl.BlockSpec((rows, d), lambda j: (0, 0)),
                  pl.BlockSpec((d, tn), lambda j: (0, j)),
                  pl.BlockSpec((1, tn), lambda j: (0, j))],
        out_specs=pl.BlockSpec((rows, tn), lambda j: (0, j)),
        out_shape=jax.ShapeDtypeStruct((rows, n), F32),
        compiler_params=_cparams(("parallel",)),
        name="ada",
    )(cond8, w_ada, b_ada.reshape(1, n))


def _act_none(acc):
    return acc


def _act_gelu(acc):
    return 0.5 * acc * (1.0 + lax.erf(acc * (2.0 ** -0.5)))


def _act_sigmoid(acc):
    return jax.nn.sigmoid(acc)


def _mm_mod_kernel(x_ref, mod_ref, w_ref, o_ref, h_ref, *, which, act, act_tail, tail_from):
    j = pl.program_id(1)

    @pl.when(j == 0)
    def _():
        shift = mod_ref[0, 3 * which:3 * which + 1, :]
        scale = mod_ref[0, 3 * which + 1:3 * which + 2, :]
        h_ref[...] = (x_ref[...] * (1.0 + scale) + shift).astype(BF16)

    def tile(fn):
        step = MM_COPY_CHUNK if fn is _act_none else MM_ACT_CHUNK
        for n0 in range(0, o_ref.shape[1], step):
            o_ref[:, n0:n0 + step] = fn(_dot(h_ref[...], w_ref[:, n0:n0 + step])).astype(o_ref.dtype)

    if act_tail is None:
        tile(act)
    else:
        pl.when(j < tail_from)(lambda: tile(act))
        pl.when(j >= tail_from)(lambda: tile(act_tail))


def _mm_mod(x, mod3, w, group_of, *, which, act, tm, tn, tn_out, out_dtype, act_tail=None, tail_from=0):
    n, d = x.shape
    nj = w.shape[1] // tn
    w_mode = dict(pipeline_mode=pl.Buffered(1)) if nj == 1 else {}
    return pl.pallas_call(
        functools.partial(_mm_mod_kernel, which=which, act=act, act_tail=act_tail, tail_from=tail_from),
        grid=(n // tm, nj),
        in_specs=[pl.BlockSpec((tm, d), lambda i, j: (i, 0)),
                  pl.BlockSpec((1, 9, d), lambda i, j: (group_of(i), 0, 0)),
                  pl.BlockSpec((d, tn), lambda i, j: (0, j), **w_mode)],
        out_specs=pl.BlockSpec((tm, tn_out), lambda i, j: (i, j)),
        out_shape=jax.ShapeDtypeStruct((n, nj * tn_out), out_dtype),
        scratch_shapes=[pltpu.VMEM((tm, d), BF16)],
        compiler_params=_cparams(("parallel", "arbitrary")),
        name="mm_mod",
    )(x, mod3, w)


def _mm_res_ln_kernel(a_ref, w_ref, x_ref, mod_ref, g_ref, b_ref, o_ref, *, gate_row, coef, nk):
    k = pl.program_id(1)

    @pl.when(k == 0)
    def _():
        o_ref[...] = jnp.zeros_like(o_ref)

    for n0 in range(0, o_ref.shape[1], FFN_OUT_CHUNK):
        o_ref[:, n0:n0 + FFN_OUT_CHUNK] += _dot(a_ref[...], w_ref[:, n0:n0 + FFN_OUT_CHUNK])

    @pl.when(k == nk - 1)
    def _():
        gate = coef * mod_ref[0, gate_row:gate_row + 1, :]
        for r0 in range(0, o_ref.shape[0], LN_ROWS):
            rows = slice(r0, r0 + LN_ROWS)
            o_ref[rows, :] = _res_ln(x_ref[rows, :], gate * o_ref[rows, :], g_ref[...], b_ref[...])


def _mm_res_ln(a, w, x, mod3, ln_g, ln_b, group_of, *, gate_row, coef, tm, tk):
    n, kdim = a.shape
    d = w.shape[1]
    nk = kdim // tk
    return pl.pallas_call(
        functools.partial(_mm_res_ln_kernel, gate_row=gate_row, coef=coef, nk=nk),
        grid=(n // tm, nk),
        in_specs=[pl.BlockSpec((tm, tk), lambda i, k: (i, k)),
                  pl.BlockSpec((tk, d), lambda i, k: (k, 0)),
                  pl.BlockSpec((tm, d), lambda i, k: (i, 0)),
                  pl.BlockSpec((1, 9, d), lambda i, k: (group_of(i), 0, 0)),
                  pl.BlockSpec((1, d), lambda i, k: (0, 0)),
                  pl.BlockSpec((1, d), lambda i, k: (0, 0))],
        out_specs=pl.BlockSpec((tm, d), lambda i, k: (i, 0)),
        out_shape=jax.ShapeDtypeStruct((n, d), F32),
        compiler_params=_cparams(("parallel", "arbitrary")),
        name="mm_res_ln",
    )(a, w, x, mod3, ln_g.reshape(1, d), ln_b.reshape(1, d))


def _ffn_kernel(x_ref, mod_ref, w1_ref, w2_ref, g_ref, b_ref, o_ref, h_ref, *, which, nj):
    j = pl.program_id(1)
    assert nj >= 2
    half = h_ref.shape[0] // 2

    def step(first, last):
        for r0 in (0, half):
            rows = slice(r0, r0 + half)
            if first:
                shift = mod_ref[0, 3 * which:3 * which + 1, :]
                scale = mod_ref[0, 3 * which + 1:3 * which + 2, :]
                h_ref[rows, :] = (x_ref[rows, :] * (1.0 + scale) + shift).astype(BF16)
            h = h_ref[rows, :]
            a = _dot(h, w1_ref[0, :, 0:D_FF_TILE])
            hid = ((a * jax.nn.sigmoid(a)) * _dot(h, w1_ref[0, :, D_FF_TILE:2 * D_FF_TILE])).astype(BF16)
            for n0 in range(0, o_ref.shape[1], FFN_OUT_CHUNK):
                cols = slice(n0, n0 + FFN_OUT_CHUNK)
                part = _dot(hid, w2_ref[0, :, cols])
                if first:
                    o_ref[rows, cols] = part
                else:
                    o_ref[rows, cols] += part
            if last:
                gate = 0.5 * mod_ref[0, 3 * which + 2:3 * which + 3, :]
                for q0 in range(r0, r0 + half, LN_ROWS):
                    q = slice(q0, q0 + LN_ROWS)
                    o_ref[q, :] = _res_ln(x_ref[q, :], gate * o_ref[q, :], g_ref[...], b_ref[...])

    pl.when(j == 0)(lambda: step(True, False))
    pl.when(jnp.logical_and(j > 0, j < nj - 1))(lambda: step(False, False))
    pl.when(j == nj - 1)(lambda: step(False, True))


def _res_ln(x, f, g, b):
    y = ALPHA * x + f
    mu = jnp.mean(y, axis=-1, keepdims=True)
    yc = y - mu
    var = jnp.mean(yc * yc, axis=-1, keepdims=True)
    return yc * lax.rsqrt(var + LN_EPS) * g + b


def _ffn(x, mod3, w1, w2, ln_g, ln_b, group_of, *, which, tm, tile0=0, n_tiles=None):
    n, d = x.shape
    n_tiles = n // tm if n_tiles is None else n_tiles
    nj = w2.shape[1] // D_FF_TILE
    s = which // 2
    return pl.pallas_call(
        functools.partial(_ffn_kernel, which=which, nj=nj),
        grid=(n_tiles, nj),
        in_specs=[pl.BlockSpec((tm, d), lambda i, j: (tile0 + i, 0)),
                  pl.BlockSpec((1, 9, d), lambda i, j: (group_of(tile0 + i), 0, 0)),
                  pl.BlockSpec((1, d, 2 * D_FF_TILE), lambda i, j: (s, 0, j)),
                  pl.BlockSpec((1, D_FF_TILE, d), lambda i, j: (s, j, 0)),
                  pl.BlockSpec((1, d), lambda i, j: (0, 0)),
                  pl.BlockSpec((1, d), lambda i, j: (0, 0))],
        out_specs=pl.BlockSpec((tm, d), lambda i, j: (i, 0)),
        out_shape=jax.ShapeDtypeStruct((n_tiles * tm, d), F32),
        scratch_shapes=[pltpu.VMEM((tm, d), BF16)],
        compiler_params=_cparams(("parallel", "arbitrary")),
        name="ffn",
    )(x, mod3, w1, w2, ln_g.reshape(1, d), ln_b.reshape(1, d))


def _prep_kernel(z_ref, za_ref, zb_ref, nbw_ref, mu_ref, w2_ref, a2_ref, g2_ref, w0_ref, a0_ref, kk_ref_, ka_ref, rk_ref,
                 seg_ref, r_o, v_o, kk_o, lw0_o, kd0_o, bb0_o, lw1_o, kd1_o, bb1_o, bonus_o, g_o, zsh):
    z = z_ref[...]
    stacked = jnp.concatenate([za_ref[...], z, zb_ref[...]], axis=0)
    nbw = nbw_ref[0]
    t0 = stacked.astype(BF16)
    t1 = (stacked - t0.astype(F32)).astype(BF16)
    nb = _dot(nbw, t0) + _dot(nbw, t1)
    zsh[...] = z + mu_ref[...] * (nb - z)

    r = zsh[:, 0:D_A]
    k = zsh[:, D_A:2 * D_A]
    v = zsh[:, 2 * D_A:3 * D_A]
    xwa = zsh[:, 3 * D_A:3 * D_A + LANES]
    xg = zsh[:, 3 * D_A + LANES:3 * D_A + LANES + LORA_G_PAD]
    seg = seg_ref[...]
    r_o[...] = r
    v_o[...] = v

    g_o[...] = _dot(jax.nn.sigmoid(xg).astype(BF16), g2_ref[...])

    kk = k * kk_ref_[...]
    ksq = kk * kk
    ss = jnp.concatenate([_dot2(ksq[:, p * LANES:(p + 1) * LANES], seg) for p in range(N_PAIR)], axis=1)
    kk = kk / jnp.maximum(jnp.sqrt(ss), NRM_EPS)
    kk_o[...] = kk

    tw = jnp.tanh(xwa).astype(BF16)
    xa = xwa.astype(BF16)
    k_a = ka_ref[...]
    r_k = rk_ref[...]
    outs = ((lw0_o, kd0_o, bb0_o), (lw1_o, kd1_o, bb1_o))
    rkd = jnp.zeros_like(r)
    for d in range(2):
        lw_o, kd_o, bb_o = outs[d]
        wl = w0_ref[d:d + 1, :] + _dot(tw, w2_ref[d])
        wlog = -(jnp.maximum(-wl, 0.0) + jnp.log(1.0 + jnp.exp(-jnp.abs(wl)))) - 0.5
        lw_o[...] = -jnp.exp(wlog)
        a = jax.nn.sigmoid(a0_ref[d:d + 1, :] + _dot(xa, a2_ref[d]))
        kd = k * (1.0 + (a - 1.0) * k_a)
        kd_o[...] = kd
        bb_o[...] = kk * a
        rkd = rkd + r * kd * r_k
    rs = jnp.concatenate([_dot2(rkd[:, p * LANES:(p + 1) * LANES], seg) for p in range(N_PAIR)], axis=1)
    bonus_o[...] = rs * v


def _neighbour_weights(tm, lat_tiles):
    assert lat_tiles >= 2
    w = np.zeros((4, tm, tm + 2 * GRID_W), np.float32)
    for t in range(tm):
        me = GRID_W + t
        if t > 0:
            w[0, t, me - 1] = 0.5
        if t < tm - 1:
            w[0, t, me + 1] = 0.5
        for variant, (has_above, has_below) in ((1, (False, True)), (2, (True, True)), (3, (True, False))):
            if t % GRID_W > 0:
                w[variant, t, me - 1] = 0.25
            if t % GRID_W < GRID_W - 1:
                w[variant, t, me + 1] = 0.25
            if t >= GRID_W or has_above:
                w[variant, t, me - GRID_W] = 0.25
            if t < tm - GRID_W or has_below:
                w[variant, t, me + GRID_W] = 0.25
    return jnp.asarray(w, BF16)


def _rwkv_prep(zs, mu, w2p, a2p, g2p, w0, a0, k_k, k_a, r_k, *, n_ctx, t_ctx, t_lat):
    n, ncol = zs.shape
    tm = t_ctx
    assert t_lat % tm == 0 and tm % GRID_W == 0 and n_ctx % tm == 0
    halo_per_tile = tm // GRID_W
    n_halo = n // GRID_W
    n_ctx_tiles, lat_tiles = n_ctx // tm, t_lat // tm

    def variant(i):
        jj = lax.rem(jnp.maximum(i - n_ctx_tiles, 0), lat_tiles)
        return jnp.where(i < n_ctx_tiles, 0, jnp.where(jj == 0, 1, jnp.where(jj == lat_tiles - 1, 3, 2)))
    head_of_lane = jnp.arange(LANES) // HEAD
    seg = (head_of_lane[:, None] == head_of_lane[None, :]).astype(BF16)
    full = lambda shape: pl.BlockSpec(shape, lambda i: (0,) * len(shape))
    row = pl.BlockSpec((tm, D_A), lambda i: (i, 0))
    out = jax.ShapeDtypeStruct((n, D_A), F32)
    return pl.pallas_call(
        _prep_kernel,
        grid=(n // tm,),
        in_specs=[pl.BlockSpec((tm, ncol), lambda i: (i, 0)),
                  pl.BlockSpec((GRID_W, ncol), lambda i: (jnp.maximum(i * halo_per_tile - 1, 0), 0)),
                  pl.BlockSpec((GRID_W, ncol), lambda i: (jnp.minimum((i + 1) * halo_per_tile, n_halo - 1), 0)),
                  pl.BlockSpec((1, tm, tm + 2 * GRID_W), lambda i: (variant(i), 0, 0)),
                  full((1, ncol)),
                  full((2, LANES, D_A)), full((2, LANES, D_A)), full((LORA_G_PAD, D_A)),
                  full((2, D_A)), full((2, D_A)), full((1, D_A)), full((1, D_A)), full((1, D_A)),
                  full((LANES, LANES))],
        out_specs=[row] * 11,
        out_shape=[out] * 11,
        scratch_shapes=[pltpu.VMEM((tm, ncol), F32)],
        compiler_params=_cparams(("parallel",)),
        name="rwkv_prep",
    )(zs, zs, zs, _neighbour_weights(tm, lat_tiles), mu.reshape(1, ncol), w2p, a2p, g2p, w0, a0,
      k_k.reshape(1, D_A), k_a.reshape(1, D_A), r_k.reshape(1, D_A), seg)


def _blockdiag(x, lane_a):
    return jnp.concatenate([jnp.where(lane_a, x, 0.0), jnp.where(lane_a, 0.0, x)], axis=0)


def _pair_mm(a, b, lane_a):
    return _dot(a.astype(BF16), _blockdiag(b, lane_a).astype(BF16))


def _scan_chunks(chains):
    cs = SCAN_C
    ids = range(len(chains))
    lane_a = chains[0][6][0]
    bd_mask = chains[0][6][6]
    rows = [pl.ds(pl.multiple_of(ch[0] * cs, cs), cs) for ch in chains]
    lanes = [slice(ch[2] * PAIR, (ch[2] + 1) * PAIR) for ch in chains]
    load = lambda k: [chains[i][3][k][rows[i], lanes[i]] for i in ids]
    r, v, kk, lw, kd, bb = (load(k) for k in range(6))
    tri2, strict, incl, eye2, level_masks = ([chains[i][6][k] for i in ids] for k in range(1, 6))

    def split2(x):
        hi = x.astype(BF16)
        return jnp.concatenate([hi, (x - hi.astype(F32)).astype(BF16)], axis=0)

    lp = [_dot(tri2[i], split2(lw[i])) for i in ids]
    total = [lp[i][0:1, :] if chains[i][1] else lp[i][cs - 1:cs, :] for i in ids]
    p_inv = [jnp.exp(-lp[i]) for i in ids]
    lhs = [jnp.concatenate([kk[i] * jnp.exp(lp[i] - lw[i]), r[i] * jnp.exp(lp[i])], axis=0).astype(BF16)
           for i in ids]
    rhs = [jnp.concatenate([_blockdiag(bb[i] * p_inv[i], lane_a), _blockdiag(kd[i] * p_inv[i], lane_a)],
                           axis=0).astype(BF16) for i in ids]
    gram = [_dot_nt(lhs[i], rhs[i]) for i in ids]
    l_beta = [jnp.where(strict[i], gram[i][0:cs, 0:PAIR], 0.0) for i in ids]
    l_kappa = [jnp.where(strict[i], gram[i][0:cs, PAIR:2 * PAIR], 0.0).astype(BF16) for i in ids]
    m_both = [jnp.concatenate([jnp.where(incl[i], gram[i][cs:2 * cs, PAIR:2 * PAIR], 0.0),
                               jnp.where(incl[i], -gram[i][cs:2 * cs, 0:PAIR], 0.0)], axis=1).astype(BF16)
              for i in ids]

    dinv = [eye2[i] - jnp.where(level_masks[i][0], l_beta[i], 0.0) for i in ids]
    for lvl in range(1, len(level_masks[0])):
        x = [_pair_mm(jnp.where(level_masks[i][lvl], l_beta[i], 0.0), dinv[i], lane_a) for i in ids]
        dinv = [dinv[i] - _pair_mm(dinv[i], x[i], lane_a) for i in ids]

    s0 = [chains[i][4][chains[i][2]] for i in ids]
    from_state = [_dot_nt(lhs[i], s0[i].astype(BF16)) for i in ids]
    v_bd = [_blockdiag(v[i], lane_a).astype(BF16) for i in ids]
    w_rhs = [from_state[i][0:cs] + _dot(l_kappa[i], v_bd[i]) for i in ids]
    u = [_pair_mm(dinv[i], w_rhs[i], lane_a) for i in ids]
    y = [from_state[i][cs:2 * cs]
         + _dot(m_both[i], jnp.concatenate([v_bd[i], _blockdiag(u[i], lane_a).astype(BF16)], axis=0)) for i in ids]
    for i in ids:
        chains[i][5][rows[i], lanes[i]] = y[i]

    to_end = [jnp.exp(total[i] - lp[i]) for i in ids]
    zt = [jnp.concatenate([v[i], -u[i]], axis=0).T.astype(BF16) for i in ids]
    kb_end = [jnp.concatenate([kd[i] * to_end[i], bb[i] * to_end[i]], axis=0).astype(BF16) for i in ids]
    upd = [_dot(zt[i], kb_end[i]) for i in ids]
    for i in ids:
        chains[i][4][chains[i][2]] = s0[i] * jnp.exp(total[i]) + jnp.where(bd_mask, upd[i], 0.0)


def _scan_consts():
    cs = SCAN_C
    lane = lax.broadcasted_iota(jnp.int32, (cs, PAIR), 1)
    t = lax.broadcasted_iota(jnp.int32, (cs, PAIR), 0)
    i = jnp.bitwise_and(lane, HEAD - 1)
    lane_a = lane < HEAD
    out = {}
    for reverse in (False, True):
        before = (i > t) if reverse else (i < t)
        strict = before
        incl = jnp.logical_or(before, i == t)
        tri2 = incl.astype(BF16)
        eye2 = (i == t).astype(F32)
        level_masks = []
        s = 1
        while s < cs:
            sh_s = s.bit_length() - 1
            same_2s = lax.shift_right_logical(t, sh_s + 1) == lax.shift_right_logical(i, sh_s + 1)
            diff_s = lax.shift_right_logical(t, sh_s) != lax.shift_right_logical(i, sh_s)
            level_masks.append(jnp.logical_and(jnp.logical_and(same_2s, diff_s), strict))
            s *= 2
        out[reverse] = (tri2, strict, incl, eye2, tuple(level_masks))
    row = lax.broadcasted_iota(jnp.int32, (PAIR, PAIR), 0)
    col = lax.broadcasted_iota(jnp.int32, (PAIR, PAIR), 1)
    bd_mask = (row < HEAD) == (col < HEAD)
    return lane_a, out, bd_mask


def _scan_kernel(rf, vf, kkf, lw0, kd0, bb0, rb, vb, kkb, lw1, kd1, bb1, s0f_ref, s0b_ref,
                 yf_ref, yb_ref, sf_ref, sb_ref, sf_scr, sb_scr, *, n_ctx_seq, lat_tb):
    g = pl.program_id(0)
    is_ctx = g < n_ctx_seq
    j = lax.rem(jnp.maximum(g - n_ctx_seq, 0), lat_tb)
    n_chunk = SCAN_TB // SCAN_C
    lane_a, per_dir, bd_mask = _scan_consts()
    consts_f = (lane_a,) + per_dir[False] + (bd_mask,)
    consts_b = (lane_a,) + per_dir[True] + (bd_mask,)

    @pl.when(is_ctx)
    def _():
        sf_scr[...] = jnp.zeros_like(sf_scr)
        sb_scr[...] = jnp.zeros_like(sb_scr)

    @pl.when(jnp.logical_and(jnp.logical_not(is_ctx), j == 0))
    def _():
        sf_scr[...] = s0f_ref[0]
        sb_scr[...] = s0b_ref[0]

    def body(c, carry):
        chains = []
        for p in range(N_PAIR):
            chains.append((c, False, p, (rf, vf, kkf, lw0, kd0, bb0), sf_scr, yf_ref, consts_f))
            chains.append((n_chunk - 1 - c, True, p, (rb, vb, kkb, lw1, kd1, bb1), sb_scr, yb_ref, consts_b))
        _scan_chunks(chains)
        return carry

    lax.fori_loop(0, n_chunk, body, 0)

    @pl.when(jnp.logical_or(is_ctx, j == lat_tb - 1))
    def _():
        sf_ref[0] = sf_scr[...]
        sb_ref[0] = sb_scr[...]


def _rwkv_scan(prep, s0f, s0b, *, n_ctx_seq, n_lat_seq, t_lat):
    lat_tb = t_lat // SCAN_TB
    n_blocks = n_ctx_seq + n_lat_seq * lat_tb
    n_seq = n_ctx_seq + n_lat_seq

    def lat_idx(g):
        q = jnp.maximum(g - n_ctx_seq, 0)
        return q // lat_tb, lax.rem(q, lat_tb)

    def bwd_block(g):
        s, j = lat_idx(g)
        return jnp.where(g < n_ctx_seq, g, n_ctx_seq + s * lat_tb + (lat_tb - 1 - j))

    def seq_of(g):
        return jnp.where(g < n_ctx_seq, g, n_ctx_seq + lat_idx(g)[0])

    fwd = pl.BlockSpec((SCAN_TB, D_A), lambda g: (g, 0))
    bwd = pl.BlockSpec((SCAN_TB, D_A), lambda g: (bwd_block(g), 0))
    st_in = pl.BlockSpec((1, N_PAIR, PAIR, PAIR), lambda g: (lat_idx(g)[0], 0, 0, 0))
    st_out = pl.BlockSpec((1, N_PAIR, PAIR, PAIR), lambda g: (seq_of(g), 0, 0, 0))
    st_shape = jax.ShapeDtypeStruct((n_seq, N_PAIR, PAIR, PAIR), F32)
    y_shape = jax.ShapeDtypeStruct((n_blocks * SCAN_TB, D_A), F32)
    r, v, kk, lw0, kd0, bb0, lw1, kd1, bb1 = prep
    return pl.pallas_call(
        functools.partial(_scan_kernel, n_ctx_seq=n_ctx_seq, lat_tb=lat_tb),
        grid=(n_blocks,),
        in_specs=[fwd] * 6 + [bwd] * 6 + [st_in, st_in],
        out_specs=[fwd, bwd, st_out, st_out],
        out_shape=[y_shape, y_shape, st_shape, st_shape],
        scratch_shapes=[pltpu.VMEM((N_PAIR, PAIR, PAIR), F32), pltpu.VMEM((N_PAIR, PAIR, PAIR), F32)],
        compiler_params=_cparams(("arbitrary",)),
        name="rwkv_scan",
    )(r, v, kk, lw0, kd0, bb0, r, v, kk, lw1, kd1, bb1, s0f, s0b)


def _gmlp_kernel(uv_ref, lg_ref, lb_ref, ws_ref, bs_ref, o_ref, *, n_chunk):
    u = uv_ref[:, 0:D_B].astype(F32)
    v = uv_ref[:, D_B:2 * D_B].astype(F32)
    mu = jnp.mean(v, axis=-1, keepdims=True)
    vc = v - mu
    var = jnp.mean(vc * vc, axis=-1, keepdims=True)
    vn = ((vc * lax.rsqrt(var + LN_EPS)) * lg_ref[...] + lb_ref[...]).astype(BF16)
    for c in range(n_chunk):
        rows = slice(c * CHUNK, (c + 1) * CHUNK)
        for g in range(G_B):
            cols = slice(g * LANES, (g + 1) * LANES)
            s = _dot(ws_ref[g], vn[rows, cols]) + bs_ref[g]
            o_ref[rows, cols] = (u[rows, cols] * s).astype(o_ref.dtype)


def _gmlp(uv, ln_g, ln_b, ws, bs_b, *, n_chunk=GMLP_CHUNKS):
    n = uv.shape[0]
    tm = n_chunk * CHUNK
    full = lambda shape: pl.BlockSpec(shape, lambda i: (0,) * len(shape))
    return pl.pallas_call(
        functools.partial(_gmlp_kernel, n_chunk=n_chunk),
        grid=(n // tm,),
        in_specs=[pl.BlockSpec((tm, 2 * D_B), lambda i: (i, 0)),
                  full((1, D_B)), full((1, D_B)), full((G_B, CHUNK, CHUNK)), full((G_B, CHUNK, LANES))],
        out_specs=pl.BlockSpec((tm, D_B), lambda i: (i, 0)),
        out_shape=jax.ShapeDtypeStruct((n, D_B), BF16),
        compiler_params=_cparams(("parallel",)),
        name="gmlp",
    )(uv, ln_g.reshape(1, D_B), ln_b.reshape(1, D_B), ws, bs_b)


def _merge_kernel(yf_ref, yr_ref, bonus_ref, g_ref, gng_ref, gnb_ref, avg_ref, yb_ref, ga_ref, gb_ref, wpa_ref, wpb_ref,
                  o_ref):
    avg = avg_ref[...]
    parts = []
    for p in range(N_PAIR):
        cols = slice(p * LANES, (p + 1) * LANES)
        yp = yf_ref[:, cols] + yr_ref[:, cols]
        yc = yp - _dot2(yp, avg)
        yn = yc * lax.rsqrt(_dot2(yc * yc, avg) + GN_EPS)
        parts.append(((yn * gng_ref[:, cols] + gnb_ref[:, cols] + bonus_ref[:, cols]) * g_ref[:, cols]).astype(BF16))
    oa = jnp.concatenate(parts, axis=1)
    yb = yb_ref[...]
    for n0 in range(0, o_ref.shape[1], FFN_OUT_CHUNK):
        cols = slice(n0, n0 + FFN_OUT_CHUNK)
        o_ref[:, cols] = (ga_ref[:, cols].astype(F32) * _dot(oa, wpa_ref[:, cols])
                          + gb_ref[:, cols].astype(F32) * _dot(yb, wpb_ref[:, cols])).astype(o_ref.dtype)


def _merge(y_fwd, y_bwd, bonus, g, gn_g, gn_b, yb, act, w_pa, w_pb, *, tm=HALF_ROW_TILE):
    n = y_fwd.shape[0]
    head_of_lane = jnp.arange(LANES) // HEAD
    avg = ((head_of_lane[:, None] == head_of_lane[None, :]).astype(F32) / HEAD).astype(BF16)
    full = lambda shape: pl.BlockSpec(shape, lambda i: (0,) * len(shape))
    row = lambda w, c=0: pl.BlockSpec((tm, w), lambda i: (i, c))
    return pl.pallas_call(
        _merge_kernel,
        grid=(n // tm,),
        in_specs=[row(D_A), row(D_A), row(D_A), row(D_A), full((1, D_A)), full((1, D_A)), full((LANES, LANES)),
                  row(D_B), row(D_MODEL, 1), row(D_MODEL, 2), full((D_A, D_MODEL)), full((D_B, D_MODEL))],
        out_specs=row(D_MODEL),
        out_shape=jax.ShapeDtypeStruct((n, D_MODEL), BF16),
        compiler_params=_cparams(("parallel",)),
        name="merge",
    )(y_fwd, y_bwd, bonus, g, gn_g.reshape(1, D_A), gn_b.reshape(1, D_A), avg, yb, act, act, w_pa, w_pb)


def _w1_layout_kernel(a_ref, *refs):
    b_refs, o_ref = refs[:-1], refs[-1]
    valid = D_FF - pl.program_id(1) * D_FF_TILE
    keep = lax.broadcasted_iota(jnp.int32, (1, D_FF_TILE), 1) < valid
    a = jnp.where(keep, a_ref[0], 0.0)
    b = jnp.where(keep, jnp.concatenate([r[0] for r in b_refs], axis=1), 0.0)
    o_ref[0] = jnp.concatenate([a, b], axis=1).astype(BF16)


def _w2_layout_kernel(*refs):
    w_refs, o_ref = refs[:-1], refs[-1]
    valid = D_FF - pl.program_id(1) * D_FF_TILE
    keep = lax.broadcasted_iota(jnp.int32, (D_FF_TILE, 1), 0) < valid
    o_ref[0] = jnp.where(keep, jnp.concatenate([r[0] for r in w_refs], axis=0), 0.0).astype(BF16)


def _ffn_weight_layouts(w_in, w_out):
    s, d, _ = w_in.shape
    nj = D_FF_PAD // D_FF_TILE
    sub = D_FF_TILE // LANES
    first_b = D_FF // LANES
    last_blk = 2 * D_FF // LANES - 1
    b_spec = lambda k: pl.BlockSpec((1, d, LANES), lambda i, j: (i, 0, jnp.minimum(first_b + sub * j + k, last_blk)))
    w1 = pl.pallas_call(
        _w1_layout_kernel,
        grid=(s, nj),
        in_specs=[pl.BlockSpec((1, d, D_FF_TILE), lambda i, j: (i, 0, j))] + [b_spec(k) for k in range(sub)],
        out_specs=pl.BlockSpec((1, d, 2 * D_FF_TILE), lambda i, j: (i, 0, j)),
        out_shape=jax.ShapeDtypeStruct((s, d, 2 * D_FF_PAD), BF16),
        compiler_params=_cparams(("parallel", "parallel")),
        name="w1_layout",
    )(w_in, *([w_in] * sub))
    last_row_blk = D_FF // LANES - 1
    r_spec = lambda k: pl.BlockSpec((1, LANES, d), lambda i, j: (i, jnp.minimum(sub * j + k, last_row_blk), 0))
    w2 = pl.pallas_call(
        _w2_layout_kernel,
        grid=(s, nj),
        in_specs=[r_spec(k) for k in range(sub)],
        out_specs=pl.BlockSpec((1, D_FF_TILE, d), lambda i, j: (i, j, 0)),
        out_shape=jax.ShapeDtypeStruct((s, D_FF_PAD, d), BF16),
        compiler_params=_cparams(("parallel", "parallel")),
        name="w2_layout",
    )(*([w_out] * sub))
    return w1, w2


def _head_cols_kernel(w_ref, o_ref, *, n_valid):
    tn = o_ref.shape[1]
    col = pl.program_id(0) * tn + lax.broadcasted_iota(jnp.int32, (1, tn), 1)
    o_ref[...] = jnp.where(col < n_valid, w_ref[...], 0.0).astype(BF16)


def _cast_cols_head(w, n_valid, n_out, *, tn=CAST_HEAD_TN):
    d, ncol = w.shape
    assert n_out % tn == 0 and n_valid <= n_out <= ncol
    return pl.pallas_call(
        functools.partial(_head_cols_kernel, n_valid=n_valid),
        grid=(n_out // tn,),
        in_specs=[pl.BlockSpec((d, tn), lambda j: (0, j))],
        out_specs=pl.BlockSpec((d, tn), lambda j: (0, j)),
        out_shape=jax.ShapeDtypeStruct((d, n_out), BF16),
        compiler_params=_cparams(("parallel",)),
        name="cast_head",
    )(w)


def _shifted_cols_kernel(*refs, lane0):
    in_refs, o_ref = refs[:-1], refs[-1]
    take_first = lax.broadcasted_iota(jnp.int32, (1, LANES), 1) < LANES - lane0
    rolled = [pltpu.roll(r[...], LANES - lane0, 1) for r in in_refs]
    o_ref[...] = jnp.concatenate([jnp.where(take_first, rolled[k], rolled[k + 1]) for k in range(len(in_refs) - 1)],
                                 axis=1).astype(BF16)


def _cast_cols_from(w, col0, *, tn=CAST_SHIFT_TN):
    d, ncol = w.shape
    n_out = ncol - col0
    blk0, lane0 = divmod(col0, LANES)
    assert n_out % tn == 0 and lane0 > 0
    sub = tn // LANES
    last_blk = (ncol - 1) // LANES
    spec = lambda k: pl.BlockSpec((d, LANES), lambda j: (0, jnp.minimum(blk0 + sub * j + k, last_blk)))
    return pl.pallas_call(
        functools.partial(_shifted_cols_kernel, lane0=lane0),
        grid=(n_out // tn,),
        in_specs=[spec(k) for k in range(sub + 1)],
        out_specs=pl.BlockSpec((d, tn), lambda j: (0, j)),
        out_shape=jax.ShapeDtypeStruct((d, n_out), BF16),
        compiler_params=_cparams(("parallel",)),
        name="cast_cols",
    )(*([w] * (sub + 1)))


def _pair_states(s):
    b = s.shape[0]
    s = s.reshape(b, N_PAIR, 2, HEAD, HEAD)
    z = jnp.zeros_like(s[:, :, 0])
    top = jnp.concatenate([s[:, :, 0], z], axis=-1)
    bot = jnp.concatenate([z, s[:, :, 1]], axis=-1)
    return jnp.concatenate([top, bot], axis=-2)


def _unpair_states(sp):
    b = sp.shape[0]
    return jnp.stack([sp[:, :, :HEAD, :HEAD], sp[:, :, HEAD:, HEAD:]], axis=2).reshape(b, H_A, HEAD, HEAD)


def kernel(x_prompt, x_sample, c, state_rwkv_fwd, state_rwkv_bwd, c_ctx, w_ada, b_ada, ln_g, ln_b,
           ffn_w_in, ffn_w_out, w_in, shift_mu, rw_w0, rw_w2, rw_a0, rw_a2, rw_g2, rw_k_k, rw_k_a,
           rw_r_k, rw_gn_g, rw_gn_b, sg_ln_g, sg_ln_b, sg_w, sg_b, w_pa, w_pb, w_o):
    n_ctx_seq, t_ctx, d = x_prompt.shape
    n_lat_seq, t_lat, _ = x_sample.shape
    n_ctx = n_ctx_seq * t_ctx
    n_lat = n_lat_seq * t_lat
    tm = ROW_TILE
    ctx_tiles = n_ctx // tm

    def group_of_tile(rows):
        assert n_ctx % rows == 0 and t_lat % rows == 0
        return lambda i: jnp.where(i < n_ctx // rows, 0, 1 + (i - n_ctx // rows) // (t_lat // rows))

    group_of = group_of_tile(tm)

    x = jnp.concatenate([x_prompt.reshape(n_ctx, d), x_sample.reshape(n_lat, d)], axis=0)
    cond8 = jnp.zeros((16, d), F32).at[0].set(c_ctx).at[1:1 + n_lat_seq].set(c)
    new_f, new_b = [], []
    for l in range(DEPTH):
        w1, w2 = _ffn_weight_layouts(ffn_w_in[l], ffn_w_out[l])
        w_zs = _cast_cols_head(w_in[l], N_SHIFT, N_SHIFT_PAD)
        w_act = _cast_cols_from(w_in[l], N_SHIFT)
        mu_p = jnp.pad(shift_mu[l], (0, N_SHIFT_PAD - N_SHIFT))
        zpad = jnp.zeros((2, LANES - LORA_W, D_A), F32)
        w2p = jnp.concatenate([rw_w2[l], zpad], axis=1).astype(BF16)
        a2p = jnp.concatenate([zpad, rw_a2[l]], axis=1).astype(BF16)
        g2p = jnp.pad(rw_g2[l], ((0, LORA_G_PAD - LORA_G), (0, 0))).astype(BF16)
        bs_b = jnp.broadcast_to(sg_b[l][:, :, None], (G_B, CHUNK, LANES))

        mod = _ada(cond8, w_ada[l], b_ada[l]).reshape(16, 9, d)

        x = _ffn(x, mod, w1, w2, ln_g[l, 0], ln_b[l, 0], group_of, which=0, tm=tm)

        zs = _mm_mod(x, mod, w_zs, group_of_tile(HALF_ROW_TILE), which=1, act=_act_none, tm=HALF_ROW_TILE, tn=N_SHIFT_PAD,
                     tn_out=N_SHIFT_PAD, out_dtype=F32)
        act = _mm_mod(x, mod, w_act, group_of, which=1, act=_act_gelu, act_tail=_act_sigmoid,
                      tail_from=2 * D_B // ACT_TN, tm=tm, tn=ACT_TN, tn_out=ACT_TN, out_dtype=BF16)
        prep = _rwkv_prep(zs, mu_p, w2p, a2p, g2p, rw_w0[l], rw_a0[l], rw_k_k[l], rw_k_a[l], rw_r_k[l].reshape(D_A),
                          n_ctx=n_ctx, t_ctx=t_ctx, t_lat=t_lat)
        bonus, g_out = prep[9], prep[10]
        assert t_ctx == SCAN_TB
        y_fwd, y_bwd, sf, sb = _rwkv_scan(prep[:9], _pair_states(state_rwkv_fwd[:, l]),
                                          _pair_states(state_rwkv_bwd[:, l]),
                                          n_ctx_seq=n_ctx_seq, n_lat_seq=n_lat_seq, t_lat=t_lat)
        new_f.append(_unpair_states(sf[:n_ctx_seq]))
        new_b.append(_unpair_states(sb[:n_ctx_seq]))
        assert 2 * D_B == D_MODEL
        yb = _gmlp(act, sg_ln_g[l], sg_ln_b[l], sg_w[l].astype(BF16), bs_b)
        merged = _merge(y_fwd, y_bwd, bonus, g_out, rw_gn_g[l], rw_gn_b[l], yb, act,
                        w_pa[l].astype(BF16), w_pb[l].astype(BF16))
        x = _mm_res_ln(merged, w_o[l].astype(BF16), x, mod, ln_g[l, 1], ln_b[l, 1], group_of, gate_row=5, coef=1.0,
                       tm=tm, tk=WO_TK)

        ffn2 = functools.partial(_ffn, x, mod, w1, w2, ln_g[l, 2], ln_b[l, 2], group_of, which=2, tm=tm)
        x_ctx = ffn2(tile0=0, n_tiles=ctx_tiles)
        x_lat = ffn2(tile0=ctx_tiles, n_tiles=n_lat // tm)
        if l + 1 < DEPTH:
            x = jnp.concatenate([x_ctx, x_lat], axis=0)

    y_prompt = x_ctx.reshape(n_ctx_seq, t_ctx, d)
    y_sample = x_lat.reshape(n_lat_seq, t_lat, d)
    new_state_fwd = jnp.stack(new_f, axis=1).astype(x_prompt.dtype)
    new_state_bwd = jnp.stack(new_b, axis=1).astype(x_prompt.dtype)
    return (y_prompt, y_sample, new_state_fwd, new_state_bwd)
```

```python
import functools

import numpy as np
import jax
import jax.numpy as jnp
from jax import lax
from jax.experimental import pallas as pl
from jax.experimental.pallas import tpu as pltpu

F32 = jnp.float32
BF16 = jnp.bfloat16

D_MODEL = 2048
DEPTH = 1
D_A = 1024
HEAD = 64
H_A = D_A // HEAD
LORA_W = 64
LORA_A = 64
LORA_G = 160
D_B = 1024
CHUNK = 128
G_B = 8
D_FF = 5504
N_SHIFT = 3 * D_A + LORA_W + LORA_A + LORA_G
GRID_W = 64
ALPHA = (2.0 * DEPTH) ** 0.25
LN_EPS = 1e-5
GN_EPS = 64e-5
NRM_EPS = 1e-12

LANES = 128
VMEM_LIMIT = 56 * 1024 * 1024

LORA_G_PAD = 256
N_SHIFT_PAD = 3 * D_A + LORA_W + LORA_A + LORA_G_PAD
D_FF_TILE = 512
D_FF_PAD = ((D_FF + D_FF_TILE - 1) // D_FF_TILE) * D_FF_TILE
FFN_OUT_CHUNK = 512
LN_ROWS = 256
MM_ACT_CHUNK = 256
MM_COPY_CHUNK = 1152
ROW_TILE = 1024
HALF_ROW_TILE = ROW_TILE // 2
ADA_TN = 1024
ACT_TN = 1024
WO_TK = 1024
GMLP_CHUNKS = 4
CAST_HEAD_TN = 384
CAST_SHIFT_TN = 512
SCAN_C = 64
SCAN_TB = 256
PAIR = 2 * HEAD
N_PAIR = H_A // 2


def _cparams(sem):
    return pltpu.CompilerParams(dimension_semantics=sem, vmem_limit_bytes=VMEM_LIMIT)


def _dot(a, b):
    return jnp.dot(a, b, preferred_element_type=F32)


def _dot_nt(a, b):
    return lax.dot_general(a, b, (((1,), (1,)), ((), ())), preferred_element_type=F32)


def _dot2(x, m):
    hi = x.astype(BF16)
    lo = (x - hi.astype(F32)).astype(BF16)
    return _dot(hi, m) + _dot(lo, m)


def _ada_kernel(c_ref, w_ref, b_ref, o_ref):
    c = c_ref[...]
    s = (c * jax.nn.sigmoid(c)).astype(BF16)
    o_ref[...] = _dot(s, w_ref[...].astype(BF16)) + b_ref[...]


def _ada(cond8, w_ada, b_ada, tn=ADA_TN):
    rows, d = cond8.shape
    n = w_ada.shape[1]
    return pl.pallas_call(
        _ada_kernel,
        grid=(n // tn,),
        in_specs=[pl.BlockSpec((rows, d), lambda j: (0, 0)),
                  pl.BlockSpec((d, tn), lambda j: (0, j)),
                  pl.BlockSpec((1, tn), lambda j: (0, j))],
        out_specs=pl.BlockSpec((rows, tn), lambda j: (0, j)),
        out_shape=jax.ShapeDtypeStruct((rows, n), F32),
        compiler_params=_cparams(("parallel",)),
        name="ada",
    )(cond8, w_ada, b_ada.reshape(1, n))


def _act_none(acc):
    return acc


def _act_gelu(acc):
    return 0.5 * acc * (1.0 + lax.erf(acc * (2.0 ** -0.5)))


def _act_sigmoid(acc):
    return jax.nn.sigmoid(acc)


def _mm_mod_kernel(x_ref, mod_ref, w_ref, o_ref, h_ref, *, which, act, act_tail, tail_from):
    j = pl.program_id(1)

    @pl.when(j == 0)
    def _():
        shift = mod_ref[0, 3 * which:3 * which + 1, :]
        scale = mod_ref[0, 3 * which + 1:3 * which + 2, :]
        h_ref[...] = (x_ref[...] * (1.0 + scale) + shift).astype(BF16)

    def tile(fn):
        step = MM_COPY_CHUNK if fn is _act_none else MM_ACT_CHUNK
        for n0 in range(0, o_ref.shape[1], step):
            o_ref[:, n0:n0 + step] = fn(_dot(h_ref[...], w_ref[:, n0:n0 + step])).astype(o_ref.dtype)

    if act_tail is None:
        tile(act)
    else:
        pl.when(j < tail_from)(lambda: tile(act))
        pl.when(j >= tail_from)(lambda: tile(act_tail))


def _mm_mod(x, mod3, w, group_of, *, which, act, tm, tn, tn_out, out_dtype, act_tail=None, tail_from=0):
    n, d = x.shape
    nj = w.shape[1] // tn
    w_mode = dict(pipeline_mode=pl.Buffered(1)) if nj == 1 else {}
    return pl.pallas_call(
        functools.partial(_mm_mod_kernel, which=which, act=act, act_tail=act_tail, tail_from=tail_from),
        grid=(n // tm, nj),
        in_specs=[pl.BlockSpec((tm, d), lambda i, j: (i, 0)),
                  pl.BlockSpec((1, 9, d), lambda i, j: (group_of(i), 0, 0)),
                  pl.BlockSpec((d, tn), lambda i, j: (0, j), **w_mode)],
        out_specs=pl.BlockSpec((tm, tn_out), lambda i, j: (i, j)),
        out_shape=jax.ShapeDtypeStruct((n, nj * tn_out), out_dtype),
        scratch_shapes=[pltpu.VMEM((tm, d), BF16)],
        compiler_params=_cparams(("parallel", "arbitrary")),
        name="mm_mod",
    )(x, mod3, w)


def _mm_res_ln_kernel(a_ref, w_ref, x_ref, mod_ref, g_ref, b_ref, o_ref, *, gate_row, coef, nk):
    k = pl.program_id(1)

    @pl.when(k == 0)
    def _():
        o_ref[...] = jnp.zeros_like(o_ref)

    for n0 in range(0, o_ref.shape[1], FFN_OUT_CHUNK):
        o_ref[:, n0:n0 + FFN_OUT_CHUNK] += _dot(a_ref[...], w_ref[:, n0:n0 + FFN_OUT_CHUNK])

    @pl.when(k == nk - 1)
    def _():
        gate = coef * mod_ref[0, gate_row:gate_row + 1, :]
        for r0 in range(0, o_ref.shape[0], LN_ROWS):
            rows = slice(r0, r0 + LN_ROWS)
            o_ref[rows, :] = _res_ln(x_ref[rows, :], gate * o_ref[rows, :], g_ref[...], b_ref[...])


def _mm_res_ln(a, w, x, mod3, ln_g, ln_b, group_of, *, gate_row, coef, tm, tk):
    n, kdim = a.shape
    d = w.shape[1]
    nk = kdim // tk
    return pl.pallas_call(
        functools.partial(_mm_res_ln_kernel, gate_row=gate_row, coef=coef, nk=nk),
        grid=(n // tm, nk),
        in_specs=[pl.BlockSpec((tm, tk), lambda i, k: (i, k)),
                  pl.BlockSpec((tk, d), lambda i, k: (k, 0)),
                  pl.BlockSpec((tm, d), lambda i, k: (i, 0)),
                  pl.BlockSpec((1, 9, d), lambda i, k: (group_of(i), 0, 0)),
                  pl.BlockSpec((1, d), lambda i, k: (0, 0)),
                  pl.BlockSpec((1, d), lambda i, k: (0, 0))],
        out_specs=pl.BlockSpec((tm, d), lambda i, k: (i, 0)),
        out_shape=jax.ShapeDtypeStruct((n, d), F32),
        compiler_params=_cparams(("parallel", "arbitrary")),
        name="mm_res_ln",
    )(a, w, x, mod3, ln_g.reshape(1, d), ln_b.reshape(1, d))


def _ffn_kernel(x_ref, mod_ref, w1_ref, w2_ref, g_ref, b_ref, o_ref, h_ref, *, which, nj):
    j = pl.program_id(1)
    assert nj >= 2
    half = h_ref.shape[0] // 2

    def step(first, last):
        for r0 in (0, half):
            rows = slice(r0, r0 + half)
            if first:
                shift = mod_ref[0, 3 * which:3 * which + 1, :]
                scale = mod_ref[0, 3 * which + 1:3 * which + 2, :]
                h_ref[rows, :] = (x_ref[rows, :] * (1.0 + scale) + shift).astype(BF16)
            h = h_ref[rows, :]
            a = _dot(h, w1_ref[0, :, 0:D_FF_TILE])
            hid = ((a * jax.nn.sigmoid(a)) * _dot(h, w1_ref[0, :, D_FF_TILE:2 * D_FF_TILE])).astype(BF16)
            for n0 in range(0, o_ref.shape[1], FFN_OUT_CHUNK):
                cols = slice(n0, n0 + FFN_OUT_CHUNK)
                part = _dot(hid, w2_ref[0, :, cols])
                if first:
                    o_ref[rows, cols] = part
                else:
                    o_ref[rows, cols] += part
            if last:
                gate = 0.5 * mod_ref[0, 3 * which + 2:3 * which + 3, :]
                for q0 in range(r0, r0 + half, LN_ROWS):
                    q = slice(q0, q0 + LN_ROWS)
                    o_ref[q, :] = _res_ln(x_ref[q, :], gate * o_ref[q, :], g_ref[...], b_ref[...])

    pl.when(j == 0)(lambda: step(True, False))
    pl.when(jnp.logical_and(j > 0, j < nj - 1))(lambda: step(False, False))
    pl.when(j == nj - 1)(lambda: step(False, True))


def _res_ln(x, f, g, b):
    y = ALPHA * x + f
    mu = jnp.mean(y, axis=-1, keepdims=True)
    yc = y - mu
    var = jnp.mean(yc * yc, axis=-1, keepdims=True)
    return yc * lax.rsqrt(var + LN_EPS) * g + b


def _ffn(x, mod3, w1, w2, ln_g, ln_b, group_of, *, which, tm, tile0=0, n_tiles=None):
    n, d = x.shape
    n_tiles = n // tm if n_tiles is None else n_tiles
    nj = w2.shape[1] // D_FF_TILE
    s = which // 2
    return pl.pallas_call(
        functools.partial(_ffn_kernel, which=which, nj=nj),
        grid=(n_tiles, nj),
        in_specs=[pl.BlockSpec((tm, d), lambda i, j: (tile0 + i, 0)),
                  pl.BlockSpec((1, 9, d), lambda i, j: (group_of(tile0 + i), 0, 0)),
                  pl.BlockSpec((1, d, 2 * D_FF_TILE), lambda i, j: (s, 0, j)),
                  pl.BlockSpec((1, D_FF_TILE, d), lambda i, j: (s, j, 0)),
                  pl.BlockSpec((1, d), lambda i, j: (0, 0)),
                  pl.BlockSpec((1, d), lambda i, j: (0, 0))],
        out_specs=pl.BlockSpec((tm, d), lambda i, j: (i, 0)),
        out_shape=jax.ShapeDtypeStruct((n_tiles * tm, d), F32),
        scratch_shapes=[pltpu.VMEM((tm, d), BF16)],
        compiler_params=_cparams(("parallel", "arbitrary")),
        name="ffn",
    )(x, mod3, w1, w2, ln_g.reshape(1, d), ln_b.reshape(1, d))


def _prep_kernel(z_ref, za_ref, zb_ref, nbw_ref, mu_ref, w2_ref, a2_ref, g2_ref, w0_ref, a0_ref, kk_ref_, ka_ref, rk_ref,
                 seg_ref, r_o, v_o, kk_o, lw0_o, kd0_o, bb0_o, lw1_o, kd1_o, bb1_o, bonus_o, g_o, zsh):
    z = z_ref[...]
    stacked = jnp.concatenate([za_ref[...], z, zb_ref[...]], axis=0)
    nbw = nbw_ref[0]
    t0 = stacked.astype(BF16)
    t1 = (stacked - t0.astype(F32)).astype(BF16)
    nb = _dot(nbw, t0) + _dot(nbw, t1)
    zsh[...] = z + mu_ref[...] * (nb - z)

    r = zsh[:, 0:D_A]
    k = zsh[:, D_A:2 * D_A]
    v = zsh[:, 2 * D_A:3 * D_A]
    xwa = zsh[:, 3 * D_A:3 * D_A + LANES]
    xg = zsh[:, 3 * D_A + LANES:3 * D_A + LANES + LORA_G_PAD]
    seg = seg_ref[...]
    r_o[...] = r
    v_o[...] = v.astype(v_o.dtype)

    g_o[...] = _dot(jax.nn.sigmoid(xg).astype(BF16), g2_ref[...])

    kk = k * kk_ref_[...]
    ksq = kk * kk
    ss = jnp.concatenate([_dot2(ksq[:, p * LANES:(p + 1) * LANES], seg) for p in range(N_PAIR)], axis=1)
    kk = kk / jnp.maximum(jnp.sqrt(ss), NRM_EPS)
    kk_o[...] = kk

    tw = jnp.tanh(xwa).astype(BF16)
    xa = xwa.astype(BF16)
    k_a = ka_ref[...]
    r_k = rk_ref[...]
    outs = ((lw0_o, kd0_o, bb0_o), (lw1_o, kd1_o, bb1_o))
    rkd = jnp.zeros_like(r)
    for d in range(2):
        lw_o, kd_o, bb_o = outs[d]
        wl = w0_ref[d:d + 1, :] + _dot(tw, w2_ref[d])
        wlog = -(jnp.maximum(-wl, 0.0) + jnp.log(1.0 + jnp.exp(-jnp.abs(wl)))) - 0.5
        lw_o[...] = -jnp.exp(wlog)
        a = jax.nn.sigmoid(a0_ref[d:d + 1, :] + _dot(xa, a2_ref[d]))
        kd = k * (1.0 + (a - 1.0) * k_a)
        kd_o[...] = kd
        bb_o[...] = kk * a
        rkd = rkd + r * kd * r_k
    rs = jnp.concatenate([_dot2(rkd[:, p * LANES:(p + 1) * LANES], seg) for p in range(N_PAIR)], axis=1)
    bonus_o[...] = rs * v


def _neighbour_weights(tm, lat_tiles):
    assert lat_tiles >= 2
    w = np.zeros((4, tm, tm + 2 * GRID_W), np.float32)
    for t in range(tm):
        me = GRID_W + t
        if t > 0:
            w[0, t, me - 1] = 0.5
        if t < tm - 1:
            w[0, t, me + 1] = 0.5
        for variant, (has_above, has_below) in ((1, (False, True)), (2, (True, True)), (3, (True, False))):
            if t % GRID_W > 0:
                w[variant, t, me - 1] = 0.25
            if t % GRID_W < GRID_W - 1:
                w[variant, t, me + 1] = 0.25
            if t >= GRID_W or has_above:
                w[variant, t, me - GRID_W] = 0.25
            if t < tm - GRID_W or has_below:
                w[variant, t, me + GRID_W] = 0.25
    return jnp.asarray(w, BF16)


def _rwkv_prep(zs, mu, w2p, a2p, g2p, w0, a0, k_k, k_a, r_k, *, n_ctx, t_ctx, t_lat):
    n, ncol = zs.shape
    tm = t_ctx
    assert t_lat % tm == 0 and tm % GRID_W == 0 and n_ctx % tm == 0
    halo_per_tile = tm // GRID_W
    n_halo = n // GRID_W
    n_ctx_tiles, lat_tiles = n_ctx // tm, t_lat // tm

    def variant(i):
        jj = lax.rem(jnp.maximum(i - n_ctx_tiles, 0), lat_tiles)
        return jnp.where(i < n_ctx_tiles, 0, jnp.where(jj == 0, 1, jnp.where(jj == lat_tiles - 1, 3, 2)))
    head_of_lane = jnp.arange(LANES) // HEAD
    seg = (head_of_lane[:, None] == head_of_lane[None, :]).astype(BF16)
    full = lambda shape: pl.BlockSpec(shape, lambda i: (0,) * len(shape))
    row = pl.BlockSpec((tm, D_A), lambda i: (i, 0))
    out = jax.ShapeDtypeStruct((n, D_A), F32)
    return pl.pallas_call(
        _prep_kernel,
        grid=(n // tm,),
        in_specs=[pl.BlockSpec((tm, ncol), lambda i: (i, 0)),
                  pl.BlockSpec((GRID_W, ncol), lambda i: (jnp.maximum(i * halo_per_tile - 1, 0), 0)),
                  pl.BlockSpec((GRID_W, ncol), lambda i: (jnp.minimum((i + 1) * halo_per_tile, n_halo - 1), 0)),
                  pl.BlockSpec((1, tm, tm + 2 * GRID_W), lambda i: (variant(i), 0, 0)),
                  full((1, ncol)),
                  full((2, LANES, D_A)), full((2, LANES, D_A)), full((LORA_G_PAD, D_A)),
                  full((2, D_A)), full((2, D_A)), full((1, D_A)), full((1, D_A)), full((1, D_A)),
                  full((LANES, LANES))],
        out_specs=[row] * 11,
        out_shape=[out, jax.ShapeDtypeStruct((n, D_A), BF16)] + [out] * 9,
        scratch_shapes=[pltpu.VMEM((tm, ncol), F32)],
        compiler_params=_cparams(("parallel",)),
        name="rwkv_prep",
    )(zs, zs, zs, _neighbour_weights(tm, lat_tiles), mu.reshape(1, ncol), w2p, a2p, g2p, w0, a0,
      k_k.reshape(1, D_A), k_a.reshape(1, D_A), r_k.reshape(1, D_A), seg)


def _blockdiag(x, lane_a):
    return jnp.concatenate([jnp.where(lane_a, x, 0.0), jnp.where(lane_a, 0.0, x)], axis=0)


def _pair_mm(a, b, lane_a):
    return _dot(a.astype(BF16), _blockdiag(b, lane_a).astype(BF16))


def _scan_chunks(chains):
    cs = SCAN_C
    ids = range(len(chains))
    lane_a = chains[0][6][0]
    bd_mask = chains[0][6][6]
    rows = [pl.ds(pl.multiple_of(ch[0] * cs, cs), cs) for ch in chains]
    lanes = [slice(ch[2] * PAIR, (ch[2] + 1) * PAIR) for ch in chains]
    load = lambda k: [chains[i][3][k][rows[i], lanes[i]] for i in ids]
    r, v, kk, lw, kd, bb = (load(k) for k in range(6))
    v = [x.astype(F32) for x in v]
    tri2, strict, incl, eye2, level_masks = ([chains[i][6][k] for i in ids] for k in range(1, 6))

    def split2(x):
        hi = x.astype(BF16)
        return jnp.concatenate([hi, (x - hi.astype(F32)).astype(BF16)], axis=0)

    lp = [_dot(tri2[i], split2(lw[i])) for i in ids]
    total = [lp[i][0:1, :] if chains[i][1] else lp[i][cs - 1:cs, :] for i in ids]
    p_inv = [jnp.exp(-lp[i]) for i in ids]
    lhs = [jnp.concatenate([kk[i] * jnp.exp(lp[i] - lw[i]), r[i] * jnp.exp(lp[i])], axis=0).astype(BF16)
           for i in ids]
    rhs = [jnp.concatenate([_blockdiag(bb[i] * p_inv[i], lane_a), _blockdiag(kd[i] * p_inv[i], lane_a)],
                           axis=0).astype(BF16) for i in ids]
    gram = [_dot_nt(lhs[i], rhs[i]) for i in ids]
    l_beta = [jnp.where(strict[i], gram[i][0:cs, 0:PAIR], 0.0) for i in ids]
    l_kappa = [jnp.where(strict[i], gram[i][0:cs, PAIR:2 * PAIR], 0.0).astype(BF16) for i in ids]
    m_both = [jnp.concatenate([jnp.where(incl[i], gram[i][cs:2 * cs, PAIR:2 * PAIR], 0.0),
                               jnp.where(incl[i], -gram[i][cs:2 * cs, 0:PAIR], 0.0)], axis=1).astype(BF16)
              for i in ids]

    dinv = [eye2[i] - jnp.where(level_masks[i][0], l_beta[i], 0.0) for i in ids]
    for lvl in range(1, len(level_masks[0])):
        x = [_pair_mm(jnp.where(level_masks[i][lvl], l_beta[i], 0.0), dinv[i], lane_a) for i in ids]
        dinv = [dinv[i] - _pair_mm(dinv[i], x[i], lane_a) for i in ids]

    s0 = [chains[i][4][chains[i][2]] for i in ids]
    from_state = [_dot_nt(lhs[i], s0[i].astype(BF16)) for i in ids]
    v_bd = [_blockdiag(v[i], lane_a).astype(BF16) for i in ids]
    w_rhs = [from_state[i][0:cs] + _dot(l_kappa[i], v_bd[i]) for i in ids]
    u = [_pair_mm(dinv[i], w_rhs[i], lane_a) for i in ids]
    y = [from_state[i][cs:2 * cs]
         + _dot(m_both[i], jnp.concatenate([v_bd[i], _blockdiag(u[i], lane_a).astype(BF16)], axis=0)) for i in ids]
    for i in ids:
        chains[i][5][rows[i], lanes[i]] = y[i]

    to_end = [jnp.exp(total[i] - lp[i]) for i in ids]
    zt = [jnp.concatenate([v[i], -u[i]], axis=0).T.astype(BF16) for i in ids]
    kb_end = [jnp.concatenate([kd[i] * to_end[i], bb[i] * to_end[i]], axis=0).astype(BF16) for i in ids]
    upd = [_dot(zt[i], kb_end[i]) for i in ids]
    for i in ids:
        chains[i][4][chains[i][2]] = s0[i] * jnp.exp(total[i]) + jnp.where(bd_mask, upd[i], 0.0)


def _scan_consts():
    cs = SCAN_C
    lane = lax.broadcasted_iota(jnp.int32, (cs, PAIR), 1)
    t = lax.broadcasted_iota(jnp.int32, (cs, PAIR), 0)
    i = jnp.bitwise_and(lane, HEAD - 1)
    lane_a = lane < HEAD
    out = {}
    for reverse in (False, True):
        before = (i > t) if reverse else (i < t)
        strict = before
        incl = jnp.logical_or(before, i == t)
        tri2 = incl.astype(BF16)
        eye2 = (i == t).astype(F32)
        level_masks = []
        s = 1
        while s < cs:
            sh_s = s.bit_length() - 1
            same_2s = lax.shift_right_logical(t, sh_s + 1) == lax.shift_right_logical(i, sh_s + 1)
            diff_s = lax.shift_right_logical(t, sh_s) != lax.shift_right_logical(i, sh_s)
            level_masks.append(jnp.logical_and(jnp.logical_and(same_2s, diff_s), strict))
            s *= 2
        out[reverse] = (tri2, strict, incl, eye2, tuple(level_masks))
    row = lax.broadcasted_iota(jnp.int32, (PAIR, PAIR), 0)
    col = lax.broadcasted_iota(jnp.int32, (PAIR, PAIR), 1)
    bd_mask = (row < HEAD) == (col < HEAD)
    return lane_a, out, bd_mask


def _scan_kernel(rf, vf, kkf, lw0, kd0, bb0, rb, vb, kkb, lw1, kd1, bb1, s0f_ref, s0b_ref,
                 yf_ref, yb_ref, sf_ref, sb_ref, sf_scr, sb_scr, *, n_ctx_seq, lat_tb):
    g = pl.program_id(0)
    is_ctx = g < n_ctx_seq
    j = lax.rem(jnp.maximum(g - n_ctx_seq, 0), lat_tb)
    n_chunk = SCAN_TB // SCAN_C
    lane_a, per_dir, bd_mask = _scan_consts()
    consts_f = (lane_a,) + per_dir[False] + (bd_mask,)
    consts_b = (lane_a,) + per_dir[True] + (bd_mask,)

    @pl.when(is_ctx)
    def _():
        sf_scr[...] = jnp.zeros_like(sf_scr)
        sb_scr[...] = jnp.zeros_like(sb_scr)

    @pl.when(jnp.logical_and(jnp.logical_not(is_ctx), j == 0))
    def _():
        sf_scr[...] = s0f_ref[0]
        sb_scr[...] = s0b_ref[0]

    def body(c, carry):
        chains = []
        for p in range(N_PAIR):
            chains.append((c, False, p, (rf, vf, kkf, lw0, kd0, bb0), sf_scr, yf_ref, consts_f))
            chains.append((n_chunk - 1 - c, True, p, (rb, vb, kkb, lw1, kd1, bb1), sb_scr, yb_ref, consts_b))
        _scan_chunks(chains)
        return carry

    lax.fori_loop(0, n_chunk, body, 0)

    @pl.when(jnp.logical_or(is_ctx, j == lat_tb - 1))
    def _():
        sf_ref[0] = sf_scr[...]
        sb_ref[0] = sb_scr[...]


def _rwkv_scan(prep, s0f, s0b, *, n_ctx_seq, n_lat_seq, t_lat):
    lat_tb = t_lat // SCAN_TB
    n_blocks = n_ctx_seq + n_lat_seq * lat_tb
    n_seq = n_ctx_seq + n_lat_seq

    def lat_idx(g):
        q = jnp.maximum(g - n_ctx_seq, 0)
        return q // lat_tb, lax.rem(q, lat_tb)

    def bwd_block(g):
        s, j = lat_idx(g)
        return jnp.where(g < n_ctx_seq, g, n_ctx_seq + s * lat_tb + (lat_tb - 1 - j))

    def seq_of(g):
        return jnp.where(g < n_ctx_seq, g, n_ctx_seq + lat_idx(g)[0])

    fwd = pl.BlockSpec((SCAN_TB, D_A), lambda g: (g, 0))
    bwd = pl.BlockSpec((SCAN_TB, D_A), lambda g: (bwd_block(g), 0))
    st_in = pl.BlockSpec((1, N_PAIR, PAIR, PAIR), lambda g: (lat_idx(g)[0], 0, 0, 0))
    st_out = pl.BlockSpec((1, N_PAIR, PAIR, PAIR), lambda g: (seq_of(g), 0, 0, 0))
    st_shape = jax.ShapeDtypeStruct((n_seq, N_PAIR, PAIR, PAIR), F32)
    y_shape = jax.ShapeDtypeStruct((n_blocks * SCAN_TB, D_A), F32)
    r, v, kk, lw0, kd0, bb0, lw1, kd1, bb1 = prep
    return pl.pallas_call(
        functools.partial(_scan_kernel, n_ctx_seq=n_ctx_seq, lat_tb=lat_tb),
        grid=(n_blocks,),
        in_specs=[fwd] * 6 + [bwd] * 6 + [st_in, st_in],
        out_specs=[fwd, bwd, st_out, st_out],
        out_shape=[y_shape, y_shape, st_shape, st_shape],
        scratch_shapes=[pltpu.VMEM((N_PAIR, PAIR, PAIR), F32), pltpu.VMEM((N_PAIR, PAIR, PAIR), F32)],
        compiler_params=_cparams(("arbitrary",)),
        name="rwkv_scan",
    )(r, v, kk, lw0, kd0, bb0, r, v, kk, lw1, kd1, bb1, s0f, s0b)


def _gmlp_kernel(uv_ref, lg_ref, lb_ref, ws_ref, bs_ref, o_ref, *, n_chunk):
    u = uv_ref[:, 0:D_B].astype(F32)
    v = uv_ref[:, D_B:2 * D_B].astype(F32)
    mu = jnp.mean(v, axis=-1, keepdims=True)
    vc = v - mu
    var = jnp.mean(vc * vc, axis=-1, keepdims=True)
    vn = ((vc * lax.rsqrt(var + LN_EPS)) * lg_ref[...] + lb_ref[...]).astype(BF16)
    for c in range(n_chunk):
        rows = slice(c * CHUNK, (c + 1) * CHUNK)
        for g in range(G_B):
            cols = slice(g * LANES, (g + 1) * LANES)
            s = _dot(ws_ref[g], vn[rows, cols]) + bs_ref[g]
            o_ref[rows, cols] = (u[rows, cols] * s).astype(o_ref.dtype)


def _gmlp(uv, ln_g, ln_b, ws, bs_b, *, n_chunk=GMLP_CHUNKS):
    n = uv.shape[0]
    tm = n_chunk * CHUNK
    full = lambda shape: pl.BlockSpec(shape, lambda i: (0,) * len(shape))
    return pl.pallas_call(
        functools.partial(_gmlp_kernel, n_chunk=n_chunk),
        grid=(n // tm,),
        in_specs=[pl.BlockSpec((tm, 2 * D_B), lambda i: (i, 0)),
                  full((1, D_B)), full((1, D_B)), full((G_B, CHUNK, CHUNK)), full((G_B, CHUNK, LANES))],
        out_specs=pl.BlockSpec((tm, D_B), lambda i: (i, 0)),
        out_shape=jax.ShapeDtypeStruct((n, D_B), BF16),
        compiler_params=_cparams(("parallel",)),
        name="gmlp",
    )(uv, ln_g.reshape(1, D_B), ln_b.reshape(1, D_B), ws, bs_b)


def _merge_kernel(yf_ref, yr_ref, bonus_ref, g_ref, gng_ref, gnb_ref, avg_ref, yb_ref, ga_ref, gb_ref, wpa_ref, wpb_ref,
                  o_ref):
    avg = avg_ref[...]
    parts = []
    for p in range(N_PAIR):
        cols = slice(p * LANES, (p + 1) * LANES)
        yp = yf_ref[:, cols] + yr_ref[:, cols]
        yc = yp - _dot2(yp, avg)
        yn = yc * lax.rsqrt(_dot2(yc * yc, avg) + GN_EPS)
        parts.append(((yn * gng_ref[:, cols] + gnb_ref[:, cols] + bonus_ref[:, cols]) * g_ref[:, cols]).astype(BF16))
    oa = jnp.concatenate(parts, axis=1)
    yb = yb_ref[...]
    for n0 in range(0, o_ref.shape[1], FFN_OUT_CHUNK):
        cols = slice(n0, n0 + FFN_OUT_CHUNK)
        o_ref[:, cols] = (ga_ref[:, cols].astype(F32) * _dot(oa, wpa_ref[:, cols])
                          + gb_ref[:, cols].astype(F32) * _dot(yb, wpb_ref[:, cols])).astype(o_ref.dtype)


def _merge(y_fwd, y_bwd, bonus, g, gn_g, gn_b, yb, act, w_pa, w_pb, *, tm=HALF_ROW_TILE):
    n = y_fwd.shape[0]
    head_of_lane = jnp.arange(LANES) // HEAD
    avg = ((head_of_lane[:, None] == head_of_lane[None, :]).astype(F32) / HEAD).astype(BF16)
    full = lambda shape: pl.BlockSpec(shape, lambda i: (0,) * len(shape))
    row = lambda w, c=0: pl.BlockSpec((tm, w), lambda i: (i, c))
    return pl.pallas_call(
        _merge_kernel,
        grid=(n // tm,),
        in_specs=[row(D_A), row(D_A), row(D_A), row(D_A), full((1, D_A)), full((1, D_A)), full((LANES, LANES)),
                  row(D_B), row(D_MODEL, 1), row(D_MODEL, 2), full((D_A, D_MODEL)), full((D_B, D_MODEL))],
        out_specs=row(D_MODEL),
        out_shape=jax.ShapeDtypeStruct((n, D_MODEL), BF16),
        compiler_params=_cparams(("parallel",)),
        name="merge",
    )(y_fwd, y_bwd, bonus, g, gn_g.reshape(1, D_A), gn_b.reshape(1, D_A), avg, yb, act, act, w_pa, w_pb)


def _w1_layout_kernel(a_ref, *refs):
    b_refs, o_ref = refs[:-1], refs[-1]
    valid = D_FF - pl.program_id(1) * D_FF_TILE
    keep = lax.broadcasted_iota(jnp.int32, (1, D_FF_TILE), 1) < valid
    a = jnp.where(keep, a_ref[0], 0.0)
    b = jnp.where(keep, jnp.concatenate([r[0] for r in b_refs], axis=1), 0.0)
    o_ref[0] = jnp.concatenate([a, b], axis=1).astype(BF16)


def _w2_layout_kernel(*refs):
    w_refs, o_ref = refs[:-1], refs[-1]
    valid = D_FF - pl.program_id(1) * D_FF_TILE
    keep = lax.broadcasted_iota(jnp.int32, (D_FF_TILE, 1), 0) < valid
    o_ref[0] = jnp.where(keep, jnp.concatenate([r[0] for r in w_refs], axis=0), 0.0).astype(BF16)


def _ffn_weight_layouts(w_in, w_out):
    s, d, _ = w_in.shape
    nj = D_FF_PAD // D_FF_TILE
    sub = D_FF_TILE // LANES
    first_b = D_FF // LANES
    last_blk = 2 * D_FF // LANES - 1
    b_spec = lambda k: pl.BlockSpec((1, d, LANES), lambda i, j: (i, 0, jnp.minimum(first_b + sub * j + k, last_blk)))
    w1 = pl.pallas_call(
        _w1_layout_kernel,
        grid=(s, nj),
        in_specs=[pl.BlockSpec((1, d, D_FF_TILE), lambda i, j: (i, 0, j))] + [b_spec(k) for k in range(sub)],
        out_specs=pl.BlockSpec((1, d, 2 * D_FF_TILE), lambda i, j: (i, 0, j)),
        out_shape=jax.ShapeDtypeStruct((s, d, 2 * D_FF_PAD), BF16),
        compiler_params=_cparams(("parallel", "parallel")),
        name="w1_layout",
    )(w_in, *([w_in] * sub))
    last_row_blk = D_FF // LANES - 1
    r_spec = lambda k: pl.BlockSpec((1, LANES, d), lambda i, j: (i, jnp.minimum(sub * j + k, last_row_blk), 0))
    w2 = pl.pallas_call(
        _w2_layout_kernel,
        grid=(s, nj),
        in_specs=[r_spec(k) for k in range(sub)],
        out_specs=pl.BlockSpec((1, D_FF_TILE, d), lambda i, j: (i, j, 0)),
        out_shape=jax.ShapeDtypeStruct((s, D_FF_PAD, d), BF16),
        compiler_params=_cparams(("parallel", "parallel")),
        name="w2_layout",
    )(*([w_out] * sub))
    return w1, w2


def _cast_rows_t_kernel(*refs, row_off, row0, n_valid):
    in_refs, o_ref = refs[:-1], refs[-1]
    tn = o_ref.shape[1]
    rows = in_refs[0][...]
    if row_off:
        rows = jnp.concatenate([rows[row_off:tn, :], in_refs[1][0:row_off, :]], axis=0)
    col = pl.program_id(0) * tn + lax.broadcasted_iota(jnp.int32, (tn, 1), 0)
    o_ref[...] = jnp.where(col < n_valid, rows, 0.0).T.astype(BF16)


def _cast_rows_t(wt, row0, n_valid, n_out, *, tn):
    nrow, d = wt.shape
    blk0, row_off = divmod(row0, tn)
    assert n_out % tn == 0 and row_off % 8 == 0 and row0 + n_out <= pl.cdiv(nrow, tn) * tn
    last_blk = pl.cdiv(nrow, tn) - 1
    spec = lambda k: pl.BlockSpec((tn, d), lambda j: (jnp.minimum(blk0 + j + k, last_blk), 0))
    n_in = 2 if row_off else 1
    return pl.pallas_call(
        functools.partial(_cast_rows_t_kernel, row_off=row_off, row0=row0, n_valid=n_valid),
        grid=(n_out // tn,),
        in_specs=[spec(k) for k in range(n_in)],
        out_specs=pl.BlockSpec((d, tn), lambda j: (0, j)),
        out_shape=jax.ShapeDtypeStruct((d, n_out), BF16),
        compiler_params=_cparams(("parallel",)),
        name="cast_rows_t",
    )(*([wt] * n_in))


def _pair_states(s):
    b = s.shape[0]
    s = s.reshape(b, N_PAIR, 2, HEAD, HEAD)
    z = jnp.zeros_like(s[:, :, 0])
    top = jnp.concatenate([s[:, :, 0], z], axis=-1)
    bot = jnp.concatenate([z, s[:, :, 1]], axis=-1)
    return jnp.concatenate([top, bot], axis=-2)


def _unpair_states(sp):
    b = sp.shape[0]
    return jnp.stack([sp[:, :, :HEAD, :HEAD], sp[:, :, HEAD:, HEAD:]], axis=2).reshape(b, H_A, HEAD, HEAD)


def kernel(x_prompt, x_sample, c, state_rwkv_fwd, state_rwkv_bwd, c_ctx, w_ada, b_ada, ln_g, ln_b,
           ffn_w_in, ffn_w_out, w_in, shift_mu, rw_w0, rw_w2, rw_a0, rw_a2, rw_g2, rw_k_k, rw_k_a,
           rw_r_k, rw_gn_g, rw_gn_b, sg_ln_g, sg_ln_b, sg_w, sg_b, w_pa, w_pb, w_o):
    n_ctx_seq, t_ctx, d = x_prompt.shape
    n_lat_seq, t_lat, _ = x_sample.shape
    n_ctx = n_ctx_seq * t_ctx
    n_lat = n_lat_seq * t_lat
    tm = ROW_TILE
    ctx_tiles = n_ctx // tm

    def group_of_tile(rows):
        assert n_ctx % rows == 0 and t_lat % rows == 0
        return lambda i: jnp.where(i < n_ctx // rows, 0, 1 + (i - n_ctx // rows) // (t_lat // rows))

    group_of = group_of_tile(tm)

    x = jnp.concatenate([x_prompt.reshape(n_ctx, d), x_sample.reshape(n_lat, d)], axis=0)
    cond8 = jnp.zeros((16, d), F32).at[0].set(c_ctx).at[1:1 + n_lat_seq].set(c)
    new_f, new_b = [], []
    for l in range(DEPTH):
        w1, w2 = _ffn_weight_layouts(ffn_w_in[l], ffn_w_out[l])
        wt = w_in[l].T
        w_zs = _cast_rows_t(wt, 0, N_SHIFT, N_SHIFT_PAD, tn=CAST_HEAD_TN)
        w_act = _cast_rows_t(wt, N_SHIFT, wt.shape[0] - N_SHIFT, wt.shape[0] - N_SHIFT, tn=CAST_SHIFT_TN)
        mu_p = jnp.pad(shift_mu[l], (0, N_SHIFT_PAD - N_SHIFT))
        zpad = jnp.zeros((2, LANES - LORA_W, D_A), F32)
        w2p = jnp.concatenate([rw_w2[l], zpad], axis=1).astype(BF16)
        a2p = jnp.concatenate([zpad, rw_a2[l]], axis=1).astype(BF16)
        g2p = jnp.pad(rw_g2[l], ((0, LORA_G_PAD - LORA_G), (0, 0))).astype(BF16)
        bs_b = jnp.broadcast_to(sg_b[l][:, :, None], (G_B, CHUNK, LANES))

        mod = _ada(cond8, w_ada[l], b_ada[l]).reshape(16, 9, d)

        x = _ffn(x, mod, w1, w2, ln_g[l, 0], ln_b[l, 0], group_of, which=0, tm=tm)

        zs = _mm_mod(x, mod, w_zs, group_of_tile(HALF_ROW_TILE), which=1, act=_act_none, tm=HALF_ROW_TILE, tn=N_SHIFT_PAD,
                     tn_out=N_SHIFT_PAD, out_dtype=F32)
        act = _mm_mod(x, mod, w_act, group_of, which=1, act=_act_gelu, act_tail=_act_sigmoid,
                      tail_from=2 * D_B // ACT_TN, tm=tm, tn=ACT_TN, tn_out=ACT_TN, out_dtype=BF16)
        prep = _rwkv_prep(zs, mu_p, w2p, a2p, g2p, rw_w0[l], rw_a0[l], rw_k_k[l], rw_k_a[l], rw_r_k[l].reshape(D_A),
                          n_ctx=n_ctx, t_ctx=t_ctx, t_lat=t_lat)
        bonus, g_out = prep[9], prep[10]
        assert t_ctx == SCAN_TB
        y_fwd, y_bwd, sf, sb = _rwkv_scan(prep[:9], _pair_states(state_rwkv_fwd[:, l]),
                                          _pair_states(state_rwkv_bwd[:, l]),
                                          n_ctx_seq=n_ctx_seq, n_lat_seq=n_lat_seq, t_lat=t_lat)
        new_f.append(_unpair_states(sf[:n_ctx_seq]))
        new_b.append(_unpair_states(sb[:n_ctx_seq]))
        assert 2 * D_B == D_MODEL
        yb = _gmlp(act, sg_ln_g[l], sg_ln_b[l], sg_w[l].astype(BF16), bs_b)
        merged = _merge(y_fwd, y_bwd, bonus, g_out, rw_gn_g[l], rw_gn_b[l], yb, act,
                        w_pa[l].astype(BF16), w_pb[l].astype(BF16))
        x = _mm_res_ln(merged, w_o[l].astype(BF16), x, mod, ln_g[l, 1], ln_b[l, 1], group_of, gate_row=5, coef=1.0,
                       tm=tm, tk=WO_TK)

        ffn2 = functools.partial(_ffn, x, mod, w1, w2, ln_g[l, 2], ln_b[l, 2], group_of, which=2, tm=tm)
        x_ctx = ffn2(tile0=0, n_tiles=ctx_tiles)
        x_lat = ffn2(tile0=ctx_tiles, n_tiles=n_lat // tm)
        if l + 1 < DEPTH:
            x = jnp.concatenate([x_ctx, x_lat], axis=0)

    y_prompt = x_ctx.reshape(n_ctx_seq, t_ctx, d)
    y_sample = x_lat.reshape(n_lat_seq, t_lat, d)
    new_state_fwd = jnp.stack(new_f, axis=1).astype(x_prompt.dtype)
    new_state_bwd = jnp.stack(new_b, axis=1).astype(x_prompt.dtype)
    return (y_prompt, y_sample, new_state_fwd, new_state_bwd)
```

```python
import functools

import numpy as np
import jax
import jax.numpy as jnp
from jax import lax
from jax.experimental import pallas as pl
from jax.experimental.pallas import tpu as pltpu

F32 = jnp.float32
BF16 = jnp.bfloat16

D_MODEL = 2048
DEPTH = 1
D_A = 1024
HEAD = 64
H_A = D_A // HEAD
LORA_W = 64
LORA_A = 64
LORA_G = 160
D_B = 1024
CHUNK = 128
G_B = 8
D_FF = 5504
N_SHIFT = 3 * D_A + LORA_W + LORA_A + LORA_G
GRID_W = 64
ALPHA = (2.0 * DEPTH) ** 0.25
LN_EPS = 1e-5
GN_EPS = 64e-5
NRM_EPS = 1e-12

LANES = 128
VMEM_LIMIT = 56 * 1024 * 1024

LORA_G_PAD = 256
N_SHIFT_PAD = 3 * D_A + LORA_W + LORA_A + LORA_G_PAD
D_FF_TILE = 512
D_FF_PAD = ((D_FF + D_FF_TILE - 1) // D_FF_TILE) * D_FF_TILE
FFN_OUT_CHUNK = 512
LN_ROWS = 256
MM_ACT_CHUNK = 256
MM_COPY_CHUNK = 1152
ROW_TILE = 1024
HALF_ROW_TILE = ROW_TILE // 2
ADA_TN = 1024
ACT_TN = 1024
WO_TK = 1024
GMLP_CHUNKS = 4
CAST_HEAD_TN = 384
CAST_SHIFT_TN = 512
SCAN_C = 64
SCAN_TB = 256
PAIR = 2 * HEAD
N_PAIR = H_A // 2


def _cparams(sem):
    return pltpu.CompilerParams(dimension_semantics=sem, vmem_limit_bytes=VMEM_LIMIT)


def _dot(a, b):
    return jnp.dot(a, b, preferred_element_type=F32)


def _dot_nt(a, b):
    return lax.dot_general(a, b, (((1,), (1,)), ((), ())), preferred_element_type=F32)


def _dot2(x, m):
    hi = x.astype(BF16)
    lo = (x - hi.astype(F32)).astype(BF16)
    return _dot(hi, m) + _dot(lo, m)


def _ada_kernel(c_ref, w_ref, b_ref, o_ref):
    c = c_ref[...]
    s = (c * jax.nn.sigmoid(c)).astype(BF16)
    o_ref[...] = _dot(s, w_ref[...].astype(BF16)) + b_ref[...]


def _ada(cond8, w_ada, b_ada, tn=ADA_TN):
    rows, d = cond8.shape
    n = w_ada.shape[1]
    return pl.pallas_call(
        _ada_kernel,
        grid=(n // tn,),
        in_specs=[pl.BlockSpec((rows, d), lambda j: (0, 0)),
                  pl.BlockSpec((d, tn), lambda j: (0, j)),
                  pl.BlockSpec((1, tn), lambda j: (0, j))],
        out_specs=pl.BlockSpec((rows, tn), lambda j: (0, j)),
        out_shape=jax.ShapeDtypeStruct((rows, n), F32),
        compiler_params=_cparams(("parallel",)),
        name="ada",
    )(cond8, w_ada, b_ada.reshape(1, n))


def _act_none(acc):
    return acc


def _act_gelu(acc):
    return 0.5 * acc * (1.0 + lax.erf(acc * (2.0 ** -0.5)))


def _act_sigmoid(acc):
    return 0.5 * jnp.tanh(0.5 * acc) + 0.5


def _mm_mod_kernel(x_ref, mod_ref, w_ref, o_ref, h_ref, *, which, act, act_tail, tail_from):
    j = pl.program_id(1)

    @pl.when(j == 0)
    def _():
        shift = mod_ref[0, 3 * which:3 * which + 1, :]
        scale = mod_ref[0, 3 * which + 1:3 * which + 2, :]
        h_ref[...] = (x_ref[...] * (1.0 + scale) + shift).astype(BF16)

    def tile(fn):
        step = MM_COPY_CHUNK if fn is _act_none else MM_ACT_CHUNK
        for n0 in range(0, o_ref.shape[1], step):
            o_ref[:, n0:n0 + step] = fn(_dot(h_ref[...], w_ref[:, n0:n0 + step])).astype(o_ref.dtype)

    if act_tail is None:
        tile(act)
    else:
        pl.when(j < tail_from)(lambda: tile(act))
        pl.when(j >= tail_from)(lambda: tile(act_tail))


def _mm_mod(x, mod3, w, group_of, *, which, act, tm, tn, tn_out, out_dtype, act_tail=None, tail_from=0):
    n, d = x.shape
    nj = w.shape[1] // tn
    w_mode = dict(pipeline_mode=pl.Buffered(1)) if nj == 1 else {}
    return pl.pallas_call(
        functools.partial(_mm_mod_kernel, which=which, act=act, act_tail=act_tail, tail_from=tail_from),
        grid=(n // tm, nj),
        in_specs=[pl.BlockSpec((tm, d), lambda i, j: (i, 0)),
                  pl.BlockSpec((1, 9, d), lambda i, j: (group_of(i), 0, 0)),
                  pl.BlockSpec((d, tn), lambda i, j: (0, j), **w_mode)],
        out_specs=pl.BlockSpec((tm, tn_out), lambda i, j: (i, j)),
        out_shape=jax.ShapeDtypeStruct((n, nj * tn_out), out_dtype),
        scratch_shapes=[pltpu.VMEM((tm, d), BF16)],
        compiler_params=_cparams(("parallel", "arbitrary")),
        name="mm_mod",
    )(x, mod3, w)


def _mm_res_ln_kernel(a_ref, w_ref, x_ref, mod_ref, g_ref, b_ref, o_ref, *, gate_row, coef, nk):
    k = pl.program_id(1)
    assert nk >= 2
    half = o_ref.shape[0] // 2

    def step(first, last):
        for r0 in (0, half):
            rows = slice(r0, r0 + half)
            for n0 in range(0, o_ref.shape[1], FFN_OUT_CHUNK):
                cols = slice(n0, n0 + FFN_OUT_CHUNK)
                part = _dot(a_ref[rows, :], w_ref[:, cols])
                if first:
                    o_ref[rows, cols] = part
                else:
                    o_ref[rows, cols] += part
            if last:
                gate = coef * mod_ref[0, gate_row:gate_row + 1, :]
                for q0 in range(r0, r0 + half, LN_ROWS):
                    q = slice(q0, q0 + LN_ROWS)
                    o_ref[q, :] = _res_ln(x_ref[q, :], gate * o_ref[q, :], g_ref[...], b_ref[...])

    pl.when(k == 0)(lambda: step(True, False))
    if nk > 2:
        pl.when(jnp.logical_and(k > 0, k < nk - 1))(lambda: step(False, False))
    pl.when(k == nk - 1)(lambda: step(False, True))


def _mm_res_ln(a, w, x, mod3, ln_g, ln_b, group_of, *, gate_row, coef, tm, tk):
    n, kdim = a.shape
    d = w.shape[1]
    nk = kdim // tk
    return pl.pallas_call(
        functools.partial(_mm_res_ln_kernel, gate_row=gate_row, coef=coef, nk=nk),
        grid=(n // tm, nk),
        in_specs=[pl.BlockSpec((tm, tk), lambda i, k: (i, k)),
                  pl.BlockSpec((tk, d), lambda i, k: (k, 0)),
                  pl.BlockSpec((tm, d), lambda i, k: (i, 0)),
                  pl.BlockSpec((1, 9, d), lambda i, k: (group_of(i), 0, 0)),
                  pl.BlockSpec((1, d), lambda i, k: (0, 0)),
                  pl.BlockSpec((1, d), lambda i, k: (0, 0))],
        out_specs=pl.BlockSpec((tm, d), lambda i, k: (i, 0)),
        out_shape=jax.ShapeDtypeStruct((n, d), F32),
        compiler_params=_cparams(("parallel", "arbitrary")),
        name="mm_res_ln",
    )(a, w, x, mod3, ln_g.reshape(1, d), ln_b.reshape(1, d))


def _ffn_kernel(x_ref, mod_ref, w1_ref, w2_ref, g_ref, b_ref, o_ref, h_ref, *, which, nj):
    j = pl.program_id(1)
    assert nj >= 2
    half = h_ref.shape[0] // 2

    def step(first, last):
        for r0 in (0, half):
            rows = slice(r0, r0 + half)
            if first:
                shift = mod_ref[0, 3 * which:3 * which + 1, :]
                scale = mod_ref[0, 3 * which + 1:3 * which + 2, :]
                h_ref[rows, :] = (x_ref[rows, :] * (1.0 + scale) + shift).astype(BF16)
            h = h_ref[rows, :]
            a = _dot(h, w1_ref[0, :, 0:D_FF_TILE])
            hid = ((a * jax.nn.sigmoid(a)) * _dot(h, w1_ref[0, :, D_FF_TILE:2 * D_FF_TILE])).astype(BF16)
            for n0 in range(0, o_ref.shape[1], FFN_OUT_CHUNK):
                cols = slice(n0, n0 + FFN_OUT_CHUNK)
                part = _dot(hid, w2_ref[0, :, cols])
                if first:
                    o_ref[rows, cols] = part
                else:
                    o_ref[rows, cols] += part
            if last:
                gate = 0.5 * mod_ref[0, 3 * which + 2:3 * which + 3, :]
                for q0 in range(r0, r0 + half, LN_ROWS):
                    q = slice(q0, q0 + LN_ROWS)
                    o_ref[q, :] = _res_ln(x_ref[q, :], gate * o_ref[q, :], g_ref[...], b_ref[...])

    pl.when(j == 0)(lambda: step(True, False))
    pl.when(jnp.logical_and(j > 0, j < nj - 1))(lambda: step(False, False))
    pl.when(j == nj - 1)(lambda: step(False, True))


def _res_ln(x, f, g, b):
    y = ALPHA * x + f
    mu = jnp.mean(y, axis=-1, keepdims=True)
    yc = y - mu
    var = jnp.mean(yc * yc, axis=-1, keepdims=True)
    return yc * lax.rsqrt(var + LN_EPS) * g + b


def _ffn(x, mod3, w1, w2, ln_g, ln_b, group_of, *, which, tm, tile0=0, n_tiles=None):
    n, d = x.shape
    n_tiles = n // tm if n_tiles is None else n_tiles
    nj = w2.shape[1] // D_FF_TILE
    s = which // 2
    return pl.pallas_call(
        functools.partial(_ffn_kernel, which=which, nj=nj),
        grid=(n_tiles, nj),
        in_specs=[pl.BlockSpec((tm, d), lambda i, j: (tile0 + i, 0)),
                  pl.BlockSpec((1, 9, d), lambda i, j: (group_of(tile0 + i), 0, 0)),
                  pl.BlockSpec((1, d, 2 * D_FF_TILE), lambda i, j: (s, 0, j)),
                  pl.BlockSpec((1, D_FF_TILE, d), lambda i, j: (s, j, 0)),
                  pl.BlockSpec((1, d), lambda i, j: (0, 0)),
                  pl.BlockSpec((1, d), lambda i, j: (0, 0))],
        out_specs=pl.BlockSpec((tm, d), lambda i, j: (i, 0)),
        out_shape=jax.ShapeDtypeStruct((n_tiles * tm, d), F32),
        scratch_shapes=[pltpu.VMEM((tm, d), BF16)],
        compiler_params=_cparams(("parallel", "arbitrary")),
        name="ffn",
    )(x, mod3, w1, w2, ln_g.reshape(1, d), ln_b.reshape(1, d))


def _prep_kernel(z_ref, za_ref, zb_ref, nbw_ref, mu_ref, w2_ref, a2_ref, g2_ref, w0_ref, a0_ref, kk_ref_, ka_ref, rk_ref,
                 seg_ref, r_o, v_o, kk_o, lw0_o, kd0_o, bb0_o, lw1_o, kd1_o, bb1_o, bonus_o, g_o, zsh):
    z = z_ref[...]
    stacked = jnp.concatenate([za_ref[...], z, zb_ref[...]], axis=0)
    nbw = nbw_ref[0]
    t0 = stacked.astype(BF16)
    t1 = (stacked - t0.astype(F32)).astype(BF16)
    nb = _dot(nbw, t0) + _dot(nbw, t1)
    zsh[...] = z + mu_ref[...] * (nb - z)

    r = zsh[:, 0:D_A]
    k = zsh[:, D_A:2 * D_A]
    v = zsh[:, 2 * D_A:3 * D_A]
    xwa = zsh[:, 3 * D_A:3 * D_A + LANES]
    xg = zsh[:, 3 * D_A + LANES:3 * D_A + LANES + LORA_G_PAD]
    seg = seg_ref[...]
    r_o[...] = r
    v_o[...] = v.astype(v_o.dtype)

    g_o[...] = _dot(jax.nn.sigmoid(xg).astype(BF16), g2_ref[...])

    kk = k * kk_ref_[...]
    ksq = kk * kk
    ss = jnp.concatenate([_dot2(ksq[:, p * LANES:(p + 1) * LANES], seg) for p in range(N_PAIR)], axis=1)
    kk = kk / jnp.maximum(jnp.sqrt(ss), NRM_EPS)
    kk_o[...] = kk

    tw = jnp.tanh(xwa).astype(BF16)
    xa = xwa.astype(BF16)
    k_a = ka_ref[...]
    r_k = rk_ref[...]
    outs = ((lw0_o, kd0_o, bb0_o), (lw1_o, kd1_o, bb1_o))
    rkd = jnp.zeros_like(r)
    for d in range(2):
        lw_o, kd_o, bb_o = outs[d]
        wl = w0_ref[d:d + 1, :] + _dot(tw, w2_ref[d])
        wlog = -(jnp.maximum(-wl, 0.0) + jnp.log(1.0 + jnp.exp(-jnp.abs(wl)))) - 0.5
        lw_o[...] = -jnp.exp(wlog)
        a = jax.nn.sigmoid(a0_ref[d:d + 1, :] + _dot(xa, a2_ref[d]))
        kd = k * (1.0 + (a - 1.0) * k_a)
        kd_o[...] = kd
        bb_o[...] = kk * a
        rkd = rkd + r * kd * r_k
    rs = jnp.concatenate([_dot2(rkd[:, p * LANES:(p + 1) * LANES], seg) for p in range(N_PAIR)], axis=1)
    bonus_o[...] = rs * v


def _neighbour_weights(tm, lat_tiles):
    assert lat_tiles >= 2
    w = np.zeros((4, tm, tm + 2 * GRID_W), np.float32)
    for t in range(tm):
        me = GRID_W + t
        if t > 0:
            w[0, t, me - 1] = 0.5
        if t < tm - 1:
            w[0, t, me + 1] = 0.5
        for variant, (has_above, has_below) in ((1, (False, True)), (2, (True, True)), (3, (True, False))):
            if t % GRID_W > 0:
                w[variant, t, me - 1] = 0.25
            if t % GRID_W < GRID_W - 1:
                w[variant, t, me + 1] = 0.25
            if t >= GRID_W or has_above:
                w[variant, t, me - GRID_W] = 0.25
            if t < tm - GRID_W or has_below:
                w[variant, t, me + GRID_W] = 0.25
    return jnp.asarray(w, BF16)


def _rwkv_prep(zs, mu, w2p, a2p, g2p, w0, a0, k_k, k_a, r_k, *, n_ctx, t_ctx, t_lat):
    n, ncol = zs.shape
    tm = t_ctx
    assert t_lat % tm == 0 and tm % GRID_W == 0 and n_ctx % tm == 0
    halo_per_tile = tm // GRID_W
    n_halo = n // GRID_W
    n_ctx_tiles, lat_tiles = n_ctx // tm, t_lat // tm

    def variant(i):
        jj = lax.rem(jnp.maximum(i - n_ctx_tiles, 0), lat_tiles)
        return jnp.where(i < n_ctx_tiles, 0, jnp.where(jj == 0, 1, jnp.where(jj == lat_tiles - 1, 3, 2)))
    head_of_lane = jnp.arange(LANES) // HEAD
    seg = (head_of_lane[:, None] == head_of_lane[None, :]).astype(BF16)
    full = lambda shape: pl.BlockSpec(shape, lambda i: (0,) * len(shape))
    row = pl.BlockSpec((tm, D_A), lambda i: (i, 0))
    out = jax.ShapeDtypeStruct((n, D_A), F32)
    return pl.pallas_call(
        _prep_kernel,
        grid=(n // tm,),
        in_specs=[pl.BlockSpec((tm, ncol), lambda i: (i, 0)),
                  pl.BlockSpec((GRID_W, ncol), lambda i: (jnp.maximum(i * halo_per_tile - 1, 0), 0)),
                  pl.BlockSpec((GRID_W, ncol), lambda i: (jnp.minimum((i + 1) * halo_per_tile, n_halo - 1), 0)),
                  pl.BlockSpec((1, tm, tm + 2 * GRID_W), lambda i: (variant(i), 0, 0)),
                  full((1, ncol)),
                  full((2, LANES, D_A)), full((2, LANES, D_A)), full((LORA_G_PAD, D_A)),
                  full((2, D_A)), full((2, D_A)), full((1, D_A)), full((1, D_A)), full((1, D_A)),
                  full((LANES, LANES))],
        out_specs=[row] * 11,
        out_shape=[out, jax.ShapeDtypeStruct((n, D_A), BF16)] + [out] * 9,
        scratch_shapes=[pltpu.VMEM((tm, ncol), F32)],
        compiler_params=_cparams(("parallel",)),
        name="rwkv_prep",
    )(zs, zs, zs, _neighbour_weights(tm, lat_tiles), mu.reshape(1, ncol), w2p, a2p, g2p, w0, a0,
      k_k.reshape(1, D_A), k_a.reshape(1, D_A), r_k.reshape(1, D_A), seg)


def _blockdiag(x, lane_a):
    return jnp.concatenate([jnp.where(lane_a, x, 0.0), jnp.where(lane_a, 0.0, x)], axis=0)


def _pair_mm(a, b, lane_a):
    return _dot(a.astype(BF16), _blockdiag(b, lane_a).astype(BF16))


def _scan_chunks(chains):
    cs = SCAN_C
    ids = range(len(chains))
    lane_a = chains[0][6][0]
    bd_mask = chains[0][6][6]
    rows = [pl.ds(pl.multiple_of(ch[0] * cs, cs), cs) for ch in chains]
    lanes = [slice(ch[2] * PAIR, (ch[2] + 1) * PAIR) for ch in chains]
    load = lambda k: [chains[i][3][k][rows[i], lanes[i]] for i in ids]
    r, v, kk, lw, kd, bb = (load(k) for k in range(6))
    v = [x.astype(F32) for x in v]
    tri2, strict, incl, eye2, level_masks = ([chains[i][6][k] for i in ids] for k in range(1, 6))

    def split2(x):
        hi = x.astype(BF16)
        return jnp.concatenate([hi, (x - hi.astype(F32)).astype(BF16)], axis=0)

    lp_dir = {}
    for i in ids:
        if chains[i][1] not in lp_dir:
            lp_dir[chains[i][1]] = _dot(tri2[i], split2(chains[i][3][3][rows[i], :]))
    lp = [lp_dir[chains[i][1]][:, lanes[i]] for i in ids]
    total = [lp[i][0:1, :] if chains[i][1] else lp[i][cs - 1:cs, :] for i in ids]
    p_inv = [jnp.exp(-lp[i]) for i in ids]
    lhs = [jnp.concatenate([kk[i] * jnp.exp(lp[i] - lw[i]), r[i] * jnp.exp(lp[i])], axis=0).astype(BF16)
           for i in ids]
    rhs = [jnp.concatenate([_blockdiag(bb[i] * p_inv[i], lane_a), _blockdiag(kd[i] * p_inv[i], lane_a)],
                           axis=0).astype(BF16) for i in ids]
    gram = [_dot_nt(lhs[i], rhs[i]) for i in ids]
    l_beta = [jnp.where(strict[i], gram[i][0:cs, 0:PAIR], 0.0) for i in ids]
    l_kappa = [jnp.where(strict[i], gram[i][0:cs, PAIR:2 * PAIR], 0.0).astype(BF16) for i in ids]
    m_both = [jnp.concatenate([jnp.where(incl[i], gram[i][cs:2 * cs, PAIR:2 * PAIR], 0.0),
                               jnp.where(incl[i], -gram[i][cs:2 * cs, 0:PAIR], 0.0)], axis=1).astype(BF16)
              for i in ids]

    dinv = [eye2[i] - jnp.where(level_masks[i][0], l_beta[i], 0.0) for i in ids]
    for lvl in range(1, len(level_masks[0])):
        x = [_pair_mm(jnp.where(level_masks[i][lvl], l_beta[i], 0.0), dinv[i], lane_a) for i in ids]
        dinv = [dinv[i] - _pair_mm(dinv[i], x[i], lane_a) for i in ids]

    s0 = [chains[i][4][chains[i][2]] for i in ids]
    from_state = [_dot_nt(lhs[i], s0[i].astype(BF16)) for i in ids]
    v_bd = [_blockdiag(v[i], lane_a).astype(BF16) for i in ids]
    w_rhs = [from_state[i][0:cs] + _dot(l_kappa[i], v_bd[i]) for i in ids]
    u = [_pair_mm(dinv[i], w_rhs[i], lane_a) for i in ids]
    y = [from_state[i][cs:2 * cs]
         + _dot(m_both[i], jnp.concatenate([v_bd[i], _blockdiag(u[i], lane_a).astype(BF16)], axis=0)) for i in ids]
    for i in ids:
        chains[i][5][rows[i], lanes[i]] = y[i]

    to_end = [jnp.exp(total[i] - lp[i]) for i in ids]
    zt = [jnp.concatenate([v[i], -u[i]], axis=0).T.astype(BF16) for i in ids]
    kb_end = [jnp.concatenate([kd[i] * to_end[i], bb[i] * to_end[i]], axis=0).astype(BF16) for i in ids]
    upd = [_dot(zt[i], kb_end[i]) for i in ids]
    for i in ids:
        chains[i][4][chains[i][2]] = s0[i] * jnp.exp(total[i]) + jnp.where(bd_mask, upd[i], 0.0)


def _scan_consts():
    cs = SCAN_C
    lane = lax.broadcasted_iota(jnp.int32, (cs, PAIR), 1)
    t = lax.broadcasted_iota(jnp.int32, (cs, PAIR), 0)
    i = jnp.bitwise_and(lane, HEAD - 1)
    lane_a = lane < HEAD
    out = {}
    for reverse in (False, True):
        before = (i > t) if reverse else (i < t)
        strict = before
        incl = jnp.logical_or(before, i == t)
        tri2 = incl.astype(BF16)
        eye2 = (i == t).astype(F32)
        level_masks = []
        s = 1
        while s < cs:
            sh_s = s.bit_length() - 1
            same_2s = lax.shift_right_logical(t, sh_s + 1) == lax.shift_right_logical(i, sh_s + 1)
            diff_s = lax.shift_right_logical(t, sh_s) != lax.shift_right_logical(i, sh_s)
            level_masks.append(jnp.logical_and(jnp.logical_and(same_2s, diff_s), strict))
            s *= 2
        out[reverse] = (tri2, strict, incl, eye2, tuple(level_masks))
    row = lax.broadcasted_iota(jnp.int32, (PAIR, PAIR), 0)
    col = lax.broadcasted_iota(jnp.int32, (PAIR, PAIR), 1)
    bd_mask = (row < HEAD) == (col < HEAD)
    return lane_a, out, bd_mask


def _scan_kernel(rf, vf, kkf, lw0, kd0, bb0, rb, vb, kkb, lw1, kd1, bb1, s0f_ref, s0b_ref,
                 yf_ref, yb_ref, sf_ref, sb_ref, sf_scr, sb_scr, *, n_ctx_seq, lat_tb):
    g = pl.program_id(0)
    is_ctx = g < n_ctx_seq
    j = lax.rem(jnp.maximum(g - n_ctx_seq, 0), lat_tb)
    n_chunk = SCAN_TB // SCAN_C
    lane_a, per_dir, bd_mask = _scan_consts()
    consts_f = (lane_a,) + per_dir[False] + (bd_mask,)
    consts_b = (lane_a,) + per_dir[True] + (bd_mask,)

    @pl.when(is_ctx)
    def _():
        sf_scr[...] = jnp.zeros_like(sf_scr)
        sb_scr[...] = jnp.zeros_like(sb_scr)

    @pl.when(jnp.logical_and(jnp.logical_not(is_ctx), j == 0))
    def _():
        sf_scr[...] = s0f_ref[0]
        sb_scr[...] = s0b_ref[0]

    def body(c, carry):
        chains = []
        for p in range(N_PAIR):
            chains.append((c, False, p, (rf, vf, kkf, lw0, kd0, bb0), sf_scr, yf_ref, consts_f))
            chains.append((n_chunk - 1 - c, True, p, (rb, vb, kkb, lw1, kd1, bb1), sb_scr, yb_ref, consts_b))
        _scan_chunks(chains)
        return carry

    lax.fori_loop(0, n_chunk, body, 0)

    @pl.when(jnp.logical_or(is_ctx, j == lat_tb - 1))
    def _():
        sf_ref[0] = sf_scr[...]
        sb_ref[0] = sb_scr[...]


def _rwkv_scan(prep, s0f, s0b, *, n_ctx_seq, n_lat_seq, t_lat):
    lat_tb = t_lat // SCAN_TB
    n_blocks = n_ctx_seq + n_lat_seq * lat_tb
    n_seq = n_ctx_seq + n_lat_seq

    def lat_idx(g):
        q = jnp.maximum(g - n_ctx_seq, 0)
        return q // lat_tb, lax.rem(q, lat_tb)

    def bwd_block(g):
        s, j = lat_idx(g)
        return jnp.where(g < n_ctx_seq, g, n_ctx_seq + s * lat_tb + (lat_tb - 1 - j))

    def seq_of(g):
        return jnp.where(g < n_ctx_seq, g, n_ctx_seq + lat_idx(g)[0])

    fwd = pl.BlockSpec((SCAN_TB, D_A), lambda g: (g, 0))
    bwd = pl.BlockSpec((SCAN_TB, D_A), lambda g: (bwd_block(g), 0))
    st_in = pl.BlockSpec((1, N_PAIR, PAIR, PAIR), lambda g: (lat_idx(g)[0], 0, 0, 0))
    st_out = pl.BlockSpec((1, N_PAIR, PAIR, PAIR), lambda g: (seq_of(g), 0, 0, 0))
    st_shape = jax.ShapeDtypeStruct((n_seq, N_PAIR, PAIR, PAIR), F32)
    y_shape = jax.ShapeDtypeStruct((n_blocks * SCAN_TB, D_A), F32)
    r, v, kk, lw0, kd0, bb0, lw1, kd1, bb1 = prep
    return pl.pallas_call(
        functools.partial(_scan_kernel, n_ctx_seq=n_ctx_seq, lat_tb=lat_tb),
        grid=(n_blocks,),
        in_specs=[fwd] * 6 + [bwd] * 6 + [st_in, st_in],
        out_specs=[fwd, bwd, st_out, st_out],
        out_shape=[y_shape, y_shape, st_shape, st_shape],
        scratch_shapes=[pltpu.VMEM((N_PAIR, PAIR, PAIR), F32), pltpu.VMEM((N_PAIR, PAIR, PAIR), F32)],
        compiler_params=_cparams(("arbitrary",)),
        name="rwkv_scan",
    )(r, v, kk, lw0, kd0, bb0, r, v, kk, lw1, kd1, bb1, s0f, s0b)


def _gmlp_kernel(uv_ref, lg_ref, lb_ref, ws_ref, bs_ref, o_ref, *, n_chunk):
    u = uv_ref[:, 0:D_B].astype(F32)
    v = uv_ref[:, D_B:2 * D_B].astype(F32)
    mu = jnp.mean(v, axis=-1, keepdims=True)
    vc = v - mu
    var = jnp.mean(vc * vc, axis=-1, keepdims=True)
    vn = ((vc * lax.rsqrt(var + LN_EPS)) * lg_ref[...] + lb_ref[...]).astype(BF16)
    for c in range(n_chunk):
        rows = slice(c * CHUNK, (c + 1) * CHUNK)
        for g in range(G_B):
            cols = slice(g * LANES, (g + 1) * LANES)
            s = _dot(ws_ref[g], vn[rows, cols]) + bs_ref[g]
            o_ref[rows, cols] = (u[rows, cols] * s).astype(o_ref.dtype)


def _gmlp(uv, ln_g, ln_b, ws, bs_b, *, n_chunk=GMLP_CHUNKS):
    n = uv.shape[0]
    tm = n_chunk * CHUNK
    full = lambda shape: pl.BlockSpec(shape, lambda i: (0,) * len(shape))
    return pl.pallas_call(
        functools.partial(_gmlp_kernel, n_chunk=n_chunk),
        grid=(n // tm,),
        in_specs=[pl.BlockSpec((tm, 2 * D_B), lambda i: (i, 0)),
                  full((1, D_B)), full((1, D_B)), full((G_B, CHUNK, CHUNK)), full((G_B, CHUNK, LANES))],
        out_specs=pl.BlockSpec((tm, D_B), lambda i: (i, 0)),
        out_shape=jax.ShapeDtypeStruct((n, D_B), BF16),
        compiler_params=_cparams(("parallel",)),
        name="gmlp",
    )(uv, ln_g.reshape(1, D_B), ln_b.reshape(1, D_B), ws, bs_b)


def _merge_kernel(yf_ref, yr_ref, bonus_ref, g_ref, gng_ref, gnb_ref, avg_ref, yb_ref, ga_ref, gb_ref, wpa_ref, wpb_ref,
                  o_ref):
    avg = avg_ref[...]
    parts = []
    for p in range(N_PAIR):
        cols = slice(p * LANES, (p + 1) * LANES)
        yp = yf_ref[:, cols] + yr_ref[:, cols]
        yc = yp - _dot2(yp, avg)
        yn = yc * lax.rsqrt(_dot2(yc * yc, avg) + GN_EPS)
        parts.append(((yn * gng_ref[:, cols] + gnb_ref[:, cols] + bonus_ref[:, cols]) * g_ref[:, cols]).astype(BF16))
    oa = jnp.concatenate(parts, axis=1)
    yb = yb_ref[...]
    for n0 in range(0, o_ref.shape[1], FFN_OUT_CHUNK):
        cols = slice(n0, n0 + FFN_OUT_CHUNK)
        o_ref[:, cols] = (ga_ref[:, cols].astype(F32) * _dot(oa, wpa_ref[:, cols])
                          + gb_ref[:, cols].astype(F32) * _dot(yb, wpb_ref[:, cols])).astype(o_ref.dtype)


def _merge(y_fwd, y_bwd, bonus, g, gn_g, gn_b, yb, act, w_pa, w_pb, *, tm=HALF_ROW_TILE):
    n = y_fwd.shape[0]
    head_of_lane = jnp.arange(LANES) // HEAD
    avg = ((head_of_lane[:, None] == head_of_lane[None, :]).astype(F32) / HEAD).astype(BF16)
    full = lambda shape: pl.BlockSpec(shape, lambda i: (0,) * len(shape))
    row = lambda w, c=0: pl.BlockSpec((tm, w), lambda i: (i, c))
    return pl.pallas_call(
        _merge_kernel,
        grid=(n // tm,),
        in_specs=[row(D_A), row(D_A), row(D_A), row(D_A), full((1, D_A)), full((1, D_A)), full((LANES, LANES)),
                  row(D_B), row(D_MODEL, 1), row(D_MODEL, 2), full((D_A, D_MODEL)), full((D_B, D_MODEL))],
        out_specs=row(D_MODEL),
        out_shape=jax.ShapeDtypeStruct((n, D_MODEL), BF16),
        compiler_params=_cparams(("parallel",)),
        name="merge",
    )(y_fwd, y_bwd, bonus, g, gn_g.reshape(1, D_A), gn_b.reshape(1, D_A), avg, yb, act, act, w_pa, w_pb)


def _w1_layout_kernel(a_ref, *refs):
    b_refs, o_ref = refs[:-1], refs[-1]
    valid = D_FF - pl.program_id(1) * D_FF_TILE
    keep = lax.broadcasted_iota(jnp.int32, (1, D_FF_TILE), 1) < valid
    a = jnp.where(keep, a_ref[0], 0.0)
    b = jnp.where(keep, jnp.concatenate([r[0] for r in b_refs], axis=1), 0.0)
    o_ref[0] = jnp.concatenate([a, b], axis=1).astype(BF16)


def _w2_layout_kernel(*refs):
    w_refs, o_ref = refs[:-1], refs[-1]
    valid = D_FF - pl.program_id(1) * D_FF_TILE
    keep = lax.broadcasted_iota(jnp.int32, (D_FF_TILE, 1), 0) < valid
    o_ref[0] = jnp.where(keep, jnp.concatenate([r[0] for r in w_refs], axis=0), 0.0).astype(BF16)


def _ffn_weight_layouts(w_in, w_out):
    s, d, _ = w_in.shape
    nj = D_FF_PAD // D_FF_TILE
    sub = D_FF_TILE // LANES
    first_b = D_FF // LANES
    last_blk = 2 * D_FF // LANES - 1
    b_spec = lambda k: pl.BlockSpec((1, d, LANES), lambda i, j: (i, 0, jnp.minimum(first_b + sub * j + k, last_blk)))
    w1 = pl.pallas_call(
        _w1_layout_kernel,
        grid=(s, nj),
        in_specs=[pl.BlockSpec((1, d, D_FF_TILE), lambda i, j: (i, 0, j))] + [b_spec(k) for k in range(sub)],
        out_specs=pl.BlockSpec((1, d, 2 * D_FF_TILE), lambda i, j: (i, 0, j)),
        out_shape=jax.ShapeDtypeStruct((s, d, 2 * D_FF_PAD), BF16),
        compiler_params=_cparams(("parallel", "parallel")),
        name="w1_layout",
    )(w_in, *([w_in] * sub))
    last_row_blk = D_FF // LANES - 1
    r_spec = lambda k: pl.BlockSpec((1, LANES, d), lambda i, j: (i, jnp.minimum(sub * j + k, last_row_blk), 0))
    w2 = pl.pallas_call(
        _w2_layout_kernel,
        grid=(s, nj),
        in_specs=[r_spec(k) for k in range(sub)],
        out_specs=pl.BlockSpec((1, D_FF_TILE, d), lambda i, j: (i, j, 0)),
        out_shape=jax.ShapeDtypeStruct((s, D_FF_PAD, d), BF16),
        compiler_params=_cparams(("parallel", "parallel")),
        name="w2_layout",
    )(*([w_out] * sub))
    return w1, w2


def _cast_rows_t_kernel(*refs, row_off, row0, n_valid):
    in_refs, o_ref = refs[:-1], refs[-1]
    tn = o_ref.shape[1]
    rows = in_refs[0][...]
    if row_off:
        rows = jnp.concatenate([rows[row_off:tn, :], in_refs[1][0:row_off, :]], axis=0)
    col = pl.program_id(0) * tn + lax.broadcasted_iota(jnp.int32, (tn, 1), 0)
    o_ref[...] = jnp.where(col < n_valid, rows, 0.0).T.astype(BF16)


def _cast_rows_t(wt, row0, n_valid, n_out, *, tn):
    nrow, d = wt.shape
    blk0, row_off = divmod(row0, tn)
    assert n_out % tn == 0 and row_off % 8 == 0 and row0 + n_out <= pl.cdiv(nrow, tn) * tn
    last_blk = pl.cdiv(nrow, tn) - 1
    spec = lambda k: pl.BlockSpec((tn, d), lambda j: (jnp.minimum(blk0 + j + k, last_blk), 0))
    n_in = 2 if row_off else 1
    return pl.pallas_call(
        functools.partial(_cast_rows_t_kernel, row_off=row_off, row0=row0, n_valid=n_valid),
        grid=(n_out // tn,),
        in_specs=[spec(k) for k in range(n_in)],
        out_specs=pl.BlockSpec((d, tn), lambda j: (0, j)),
        out_shape=jax.ShapeDtypeStruct((d, n_out), BF16),
        compiler_params=_cparams(("parallel",)),
        name="cast_rows_t",
    )(*([wt] * n_in))


def _pair_states(s):
    b = s.shape[0]
    s = s.reshape(b, N_PAIR, 2, HEAD, HEAD)
    z = jnp.zeros_like(s[:, :, 0])
    top = jnp.concatenate([s[:, :, 0], z], axis=-1)
    bot = jnp.concatenate([z, s[:, :, 1]], axis=-1)
    return jnp.concatenate([top, bot], axis=-2)


def _unpair_states(sp):
    b = sp.shape[0]
    return jnp.stack([sp[:, :, :HEAD, :HEAD], sp[:, :, HEAD:, HEAD:]], axis=2).reshape(b, H_A, HEAD, HEAD)


def kernel(x_prompt, x_sample, c, state_rwkv_fwd, state_rwkv_bwd, c_ctx, w_ada, b_ada, ln_g, ln_b,
           ffn_w_in, ffn_w_out, w_in, shift_mu, rw_w0, rw_w2, rw_a0, rw_a2, rw_g2, rw_k_k, rw_k_a,
           rw_r_k, rw_gn_g, rw_gn_b, sg_ln_g, sg_ln_b, sg_w, sg_b, w_pa, w_pb, w_o):
    n_ctx_seq, t_ctx, d = x_prompt.shape
    n_lat_seq, t_lat, _ = x_sample.shape
    n_ctx = n_ctx_seq * t_ctx
    n_lat = n_lat_seq * t_lat
    tm = ROW_TILE
    ctx_tiles = n_ctx // tm

    def group_of_tile(rows):
        assert n_ctx % rows == 0 and t_lat % rows == 0
        return lambda i: jnp.where(i < n_ctx // rows, 0, 1 + (i - n_ctx // rows) // (t_lat // rows))

    group_of = group_of_tile(tm)

    x = jnp.concatenate([x_prompt.reshape(n_ctx, d), x_sample.reshape(n_lat, d)], axis=0)
    cond8 = jnp.zeros((16, d), F32).at[0].set(c_ctx).at[1:1 + n_lat_seq].set(c)
    new_f, new_b = [], []
    for l in range(DEPTH):
        w1, w2 = _ffn_weight_layouts(ffn_w_in[l], ffn_w_out[l])
        wt = w_in[l].T
        w_zs = _cast_rows_t(wt, 0, N_SHIFT, N_SHIFT_PAD, tn=CAST_HEAD_TN)
        w_act = _cast_rows_t(wt, N_SHIFT, wt.shape[0] - N_SHIFT, wt.shape[0] - N_SHIFT, tn=CAST_SHIFT_TN)
        mu_p = jnp.pad(shift_mu[l], (0, N_SHIFT_PAD - N_SHIFT))
        zpad = jnp.zeros((2, LANES - LORA_W, D_A), F32)
        w2p = jnp.concatenate([rw_w2[l], zpad], axis=1).astype(BF16)
        a2p = jnp.concatenate([zpad, rw_a2[l]], axis=1).astype(BF16)
        g2p = jnp.pad(rw_g2[l], ((0, LORA_G_PAD - LORA_G), (0, 0))).astype(BF16)
        bs_b = jnp.broadcast_to(sg_b[l][:, :, None], (G_B, CHUNK, LANES))

        mod = _ada(cond8, w_ada[l], b_ada[l]).reshape(16, 9, d)

        x = _ffn(x, mod, w1, w2, ln_g[l, 0], ln_b[l, 0], group_of, which=0, tm=tm)

        zs = _mm_mod(x, mod, w_zs, group_of_tile(HALF_ROW_TILE), which=1, act=_act_none, tm=HALF_ROW_TILE, tn=N_SHIFT_PAD,
                     tn_out=N_SHIFT_PAD, out_dtype=F32)
        act = _mm_mod(x, mod, w_act, group_of, which=1, act=_act_gelu, act_tail=_act_sigmoid,
                      tail_from=2 * D_B // ACT_TN, tm=tm, tn=ACT_TN, tn_out=ACT_TN, out_dtype=BF16)
        prep = _rwkv_prep(zs, mu_p, w2p, a2p, g2p, rw_w0[l], rw_a0[l], rw_k_k[l], rw_k_a[l], rw_r_k[l].reshape(D_A),
                          n_ctx=n_ctx, t_ctx=t_ctx, t_lat=t_lat)
        bonus, g_out = prep[9], prep[10]
        assert t_ctx == SCAN_TB
        y_fwd, y_bwd, sf, sb = _rwkv_scan(prep[:9], _pair_states(state_rwkv_fwd[:, l]),
                                          _pair_states(state_rwkv_bwd[:, l]),
                                          n_ctx_seq=n_ctx_seq, n_lat_seq=n_lat_seq, t_lat=t_lat)
        new_f.append(_unpair_states(sf[:n_ctx_seq]))
        new_b.append(_unpair_states(sb[:n_ctx_seq]))
        assert 2 * D_B == D_MODEL
        yb = _gmlp(act, sg_ln_g[l], sg_ln_b[l], sg_w[l].astype(BF16), bs_b)
        merged = _merge(y_fwd, y_bwd, bonus, g_out, rw_gn_g[l], rw_gn_b[l], yb, act,
                        w_pa[l].astype(BF16), w_pb[l].astype(BF16))
        x = _mm_res_ln(merged, w_o[l].astype(BF16), x, mod, ln_g[l, 1], ln_b[l, 1], group_of, gate_row=5, coef=1.0,
                       tm=tm, tk=WO_TK)

        ffn2 = functools.partial(_ffn, x, mod, w1, w2, ln_g[l, 2], ln_b[l, 2], group_of, which=2, tm=tm)
        x_ctx = ffn2(tile0=0, n_tiles=ctx_tiles)
        x_lat = ffn2(tile0=ctx_tiles, n_tiles=n_lat // tm)
        if l + 1 < DEPTH:
            x = jnp.concatenate([x_ctx, x_lat], axis=0)

    y_prompt = x_ctx.reshape(n_ctx_seq, t_ctx, d)
    y_sample = x_lat.reshape(n_lat_seq, t_lat, d)
    new_state_fwd = jnp.stack(new_f, axis=1).astype(x_prompt.dtype)
    new_state_bwd = jnp.stack(new_b, axis=1).astype(x_prompt.dtype)
    return (y_prompt, y_sample, new_state_fwd, new_state_bwd)
```

```python
import functools

import numpy as np
import jax
import jax.numpy as jnp
from jax import lax
from jax.experimental import pallas as pl
from jax.experimental.pallas import tpu as pltpu

F32 = jnp.float32
BF16 = jnp.bfloat16

D_MODEL = 2048
DEPTH = 1
D_A = 1024
HEAD = 64
H_A = D_A // HEAD
LORA_W = 64
LORA_A = 64
LORA_G = 160
D_B = 1024
CHUNK = 128
G_B = 8
D_FF = 5504
N_SHIFT = 3 * D_A + LORA_W + LORA_A + LORA_G
GRID_W = 64
ALPHA = (2.0 * DEPTH) ** 0.25
LN_EPS = 1e-5
GN_EPS = 64e-5
NRM_EPS = 1e-12
NEG_EXP_HALF = -float(np.exp(-0.5))

LANES = 128
VMEM_LIMIT = 56 * 1024 * 1024

LORA_G_PAD = 256
N_SHIFT_PAD = 3 * D_A + LORA_W + LORA_A + LORA_G_PAD
D_FF_TILE = 512
D_FF_PAD = ((D_FF + D_FF_TILE - 1) // D_FF_TILE) * D_FF_TILE
FFN_OUT_CHUNK = 512
LN_ROWS = 256
MM_ACT_CHUNK = 256
MM_COPY_CHUNK = 1152
ROW_TILE = 1024
HALF_ROW_TILE = ROW_TILE // 2
ADA_TN = 1024
ACT_TN = 1024
WO_TK = 1024
GMLP_CHUNKS = 4
CAST_HEAD_TN = 384
CAST_SHIFT_TN = 512
SCAN_C = 64
SCAN_TB = 256
PAIR = 2 * HEAD
N_PAIR = H_A // 2


def _cparams(sem):
    return pltpu.CompilerParams(dimension_semantics=sem, vmem_limit_bytes=VMEM_LIMIT)


def _dot(a, b):
    return jnp.dot(a, b, preferred_element_type=F32)


def _dot_nt(a, b):
    return lax.dot_general(a, b, (((1,), (1,)), ((), ())), preferred_element_type=F32)


def _dot2(x, m):
    hi = x.astype(BF16)
    lo = (x - hi.astype(F32)).astype(BF16)
    return _dot(hi, m) + _dot(lo, m)


def _ada_kernel(c_ref, w_ref, b_ref, o_ref):
    c = c_ref[...]
    s = (c * jax.nn.sigmoid(c)).astype(BF16)
    o_ref[...] = _dot(s, w_ref[...].astype(BF16)) + b_ref[...]


def _ada(cond8, w_ada, b_ada, tn=ADA_TN):
    rows, d = cond8.shape
    n = w_ada.shape[1]
    return pl.pallas_call(
        _ada_kernel,
        grid=(n // tn,),
        in_specs=[pl.BlockSpec((rows, d), lambda j: (0, 0)),
                  pl.BlockSpec((d, tn), lambda j: (0, j)),
                  pl.BlockSpec((1, tn), lambda j: (0, j))],
        out_specs=pl.BlockSpec((rows, tn), lambda j: (0, j)),
        out_shape=jax.ShapeDtypeStruct((rows, n), F32),
        compiler_params=_cparams(("parallel",)),
        name="ada",
    )(cond8, w_ada, b_ada.reshape(1, n))


def _act_none(acc):
    return acc


def _act_gelu(acc):
    return 0.5 * acc * (1.0 + lax.erf(acc * (2.0 ** -0.5)))


def _act_sigmoid(acc):
    return 0.5 * jnp.tanh(0.5 * acc) + 0.5


def _mm_mod_kernel(x_ref, mod_ref, w_ref, o_ref, h_ref, *, which, act, act_tail, tail_from):
    j = pl.program_id(1)

    @pl.when(j == 0)
    def _():
        shift = mod_ref[0, 3 * which:3 * which + 1, :]
        scale = mod_ref[0, 3 * which + 1:3 * which + 2, :]
        h_ref[...] = (x_ref[...] * (1.0 + scale) + shift).astype(BF16)

    def tile(fn):
        step = MM_COPY_CHUNK if fn is _act_none else MM_ACT_CHUNK
        for n0 in range(0, o_ref.shape[1], step):
            o_ref[:, n0:n0 + step] = fn(_dot(h_ref[...], w_ref[:, n0:n0 + step])).astype(o_ref.dtype)

    if act_tail is None:
        tile(act)
    else:
        pl.when(j < tail_from)(lambda: tile(act))
        pl.when(j >= tail_from)(lambda: tile(act_tail))


def _mm_mod(x, mod3, w, group_of, *, which, act, tm, tn, tn_out, out_dtype, act_tail=None, tail_from=0):
    n, d = x.shape
    nj = w.shape[1] // tn
    w_mode = dict(pipeline_mode=pl.Buffered(1)) if nj == 1 else {}
    return pl.pallas_call(
        functools.partial(_mm_mod_kernel, which=which, act=act, act_tail=act_tail, tail_from=tail_from),
        grid=(n // tm, nj),
        in_specs=[pl.BlockSpec((tm, d), lambda i, j: (i, 0)),
                  pl.BlockSpec((1, 9, d), lambda i, j: (group_of(i), 0, 0)),
                  pl.BlockSpec((d, tn), lambda i, j: (0, j), **w_mode)],
        out_specs=pl.BlockSpec((tm, tn_out), lambda i, j: (i, j)),
        out_shape=jax.ShapeDtypeStruct((n, nj * tn_out), out_dtype),
        scratch_shapes=[pltpu.VMEM((tm, d), BF16)],
        compiler_params=_cparams(("parallel", "arbitrary")),
        name="mm_mod",
    )(x, mod3, w)


def _mm_res_ln_kernel(a_ref, w_ref, x_ref, mod_ref, g_ref, b_ref, o_ref, *, gate_row, coef, nk):
    k = pl.program_id(1)
    assert nk >= 2
    half = o_ref.shape[0] // 2

    def step(first, last):
        for r0 in (0, half):
            rows = slice(r0, r0 + half)
            for n0 in range(0, o_ref.shape[1], FFN_OUT_CHUNK):
                cols = slice(n0, n0 + FFN_OUT_CHUNK)
                part = _dot(a_ref[rows, :], w_ref[:, cols])
                if first:
                    o_ref[rows, cols] = part
                else:
                    o_ref[rows, cols] += part
            if last:
                gate = coef * mod_ref[0, gate_row:gate_row + 1, :]
                for q0 in range(r0, r0 + half, LN_ROWS):
                    q = slice(q0, q0 + LN_ROWS)
                    o_ref[q, :] = _res_ln(x_ref[q, :], gate * o_ref[q, :], g_ref[...], b_ref[...])

    pl.when(k == 0)(lambda: step(True, False))
    if nk > 2:
        pl.when(jnp.logical_and(k > 0, k < nk - 1))(lambda: step(False, False))
    pl.when(k == nk - 1)(lambda: step(False, True))


def _mm_res_ln(a, w, x, mod3, ln_g, ln_b, group_of, *, gate_row, coef, tm, tk):
    n, kdim = a.shape
    d = w.shape[1]
    nk = kdim // tk
    return pl.pallas_call(
        functools.partial(_mm_res_ln_kernel, gate_row=gate_row, coef=coef, nk=nk),
        grid=(n // tm, nk),
        in_specs=[pl.BlockSpec((tm, tk), lambda i, k: (i, k)),
                  pl.BlockSpec((tk, d), lambda i, k: (k, 0)),
                  pl.BlockSpec((tm, d), lambda i, k: (i, 0)),
                  pl.BlockSpec((1, 9, d), lambda i, k: (group_of(i), 0, 0)),
                  pl.BlockSpec((1, d), lambda i, k: (0, 0)),
                  pl.BlockSpec((1, d), lambda i, k: (0, 0))],
        out_specs=pl.BlockSpec((tm, d), lambda i, k: (i, 0)),
        out_shape=jax.ShapeDtypeStruct((n, d), F32),
        compiler_params=_cparams(("parallel", "arbitrary")),
        name="mm_res_ln",
    )(a, w, x, mod3, ln_g.reshape(1, d), ln_b.reshape(1, d))


def _ffn_kernel(x_ref, mod_ref, w1_ref, w2_ref, g_ref, b_ref, o_ref, h_ref, *, which, nj):
    j = pl.program_id(1)
    assert nj >= 2
    half = h_ref.shape[0] // 2

    def step(first, last):
        for r0 in (0, half):
            rows = slice(r0, r0 + half)
            if first:
                shift = mod_ref[0, 3 * which:3 * which + 1, :]
                scale = mod_ref[0, 3 * which + 1:3 * which + 2, :]
                h_ref[rows, :] = (x_ref[rows, :] * (1.0 + scale) + shift).astype(BF16)
            h = h_ref[rows, :]
            a = _dot(h, w1_ref[0, :, 0:D_FF_TILE])
            hid = ((a * jax.nn.sigmoid(a)) * _dot(h, w1_ref[0, :, D_FF_TILE:2 * D_FF_TILE])).astype(BF16)
            for n0 in range(0, o_ref.shape[1], FFN_OUT_CHUNK):
                cols = slice(n0, n0 + FFN_OUT_CHUNK)
                part = _dot(hid, w2_ref[0, :, cols])
                if first:
                    o_ref[rows, cols] = part
                else:
                    o_ref[rows, cols] += part
            if last:
                gate = 0.5 * mod_ref[0, 3 * which + 2:3 * which + 3, :]
                for q0 in range(r0, r0 + half, LN_ROWS):
                    q = slice(q0, q0 + LN_ROWS)
                    o_ref[q, :] = _res_ln(x_ref[q, :], gate * o_ref[q, :], g_ref[...], b_ref[...])

    pl.when(j == 0)(lambda: step(True, False))
    pl.when(jnp.logical_and(j > 0, j < nj - 1))(lambda: step(False, False))
    pl.when(j == nj - 1)(lambda: step(False, True))


def _res_ln(x, f, g, b):
    y = ALPHA * x + f
    mu = jnp.mean(y, axis=-1, keepdims=True)
    yc = y - mu
    var = jnp.mean(yc * yc, axis=-1, keepdims=True)
    return yc * lax.rsqrt(var + LN_EPS) * g + b


def _ffn(x, mod3, w1, w2, ln_g, ln_b, group_of, *, which, tm, tile0=0, n_tiles=None):
    n, d = x.shape
    n_tiles = n // tm if n_tiles is None else n_tiles
    nj = w2.shape[1] // D_FF_TILE
    s = which // 2
    return pl.pallas_call(
        functools.partial(_ffn_kernel, which=which, nj=nj),
        grid=(n_tiles, nj),
        in_specs=[pl.BlockSpec((tm, d), lambda i, j: (tile0 + i, 0)),
                  pl.BlockSpec((1, 9, d), lambda i, j: (group_of(tile0 + i), 0, 0)),
                  pl.BlockSpec((1, d, 2 * D_FF_TILE), lambda i, j: (s, 0, j)),
                  pl.BlockSpec((1, D_FF_TILE, d), lambda i, j: (s, j, 0)),
                  pl.BlockSpec((1, d), lambda i, j: (0, 0)),
                  pl.BlockSpec((1, d), lambda i, j: (0, 0))],
        out_specs=pl.BlockSpec((tm, d), lambda i, j: (i, 0)),
        out_shape=jax.ShapeDtypeStruct((n_tiles * tm, d), F32),
        scratch_shapes=[pltpu.VMEM((tm, d), BF16)],
        compiler_params=_cparams(("parallel", "arbitrary")),
        name="ffn",
    )(x, mod3, w1, w2, ln_g.reshape(1, d), ln_b.reshape(1, d))


def _prep_kernel(z_ref, za_ref, zb_ref, nbw_ref, mu_ref, w2_ref, a2_ref, g2_ref, w0_ref, a0_ref, kk_ref_, ka_ref, rk_ref,
                 seg_ref, r_o, v_o, kk_o, lw0_o, kd0_o, bb0_o, lw1_o, kd1_o, bb1_o, bonus_o, g_o, zsh):
    z = z_ref[...]
    stacked = jnp.concatenate([za_ref[...], z, zb_ref[...]], axis=0)
    nbw = nbw_ref[0]
    t0 = stacked.astype(BF16)
    t1 = (stacked - t0.astype(F32)).astype(BF16)
    nb = _dot(nbw, t0) + _dot(nbw, t1)
    zsh[...] = z + mu_ref[...] * (nb - z)

    r = zsh[:, 0:D_A]
    k = zsh[:, D_A:2 * D_A]
    v = zsh[:, 2 * D_A:3 * D_A]
    xwa = zsh[:, 3 * D_A:3 * D_A + LANES]
    xg = zsh[:, 3 * D_A + LANES:3 * D_A + LANES + LORA_G_PAD]
    seg = seg_ref[...]
    r_o[...] = r
    v_o[...] = v.astype(v_o.dtype)

    g_o[...] = _dot(jax.nn.sigmoid(xg).astype(BF16), g2_ref[...])

    kk = k * kk_ref_[...]
    ksq = kk * kk
    ss = jnp.concatenate([_dot2(ksq[:, p * LANES:(p + 1) * LANES], seg) for p in range(N_PAIR)], axis=1)
    kk = kk * lax.rsqrt(jnp.maximum(ss, NRM_EPS * NRM_EPS))
    kk_o[...] = kk

    tw = jnp.tanh(xwa).astype(BF16)
    xa = xwa.astype(BF16)
    k_a = ka_ref[...]
    r_k = rk_ref[...]
    outs = ((lw0_o, kd0_o, bb0_o), (lw1_o, kd1_o, bb1_o))
    rkd = jnp.zeros_like(r)
    for d in range(2):
        lw_o, kd_o, bb_o = outs[d]
        wl = w0_ref[d:d + 1, :] + _dot(tw, w2_ref[d])
        lw_o[...] = NEG_EXP_HALF * _act_sigmoid(wl)
        a = _act_sigmoid(a0_ref[d:d + 1, :] + _dot(xa, a2_ref[d]))
        kd = k * (1.0 + (a - 1.0) * k_a)
        kd_o[...] = kd
        bb_o[...] = kk * a
        rkd = rkd + r * kd * r_k
    rs = jnp.concatenate([_dot2(rkd[:, p * LANES:(p + 1) * LANES], seg) for p in range(N_PAIR)], axis=1)
    bonus_o[...] = rs * v


def _neighbour_weights(tm, lat_tiles):
    assert lat_tiles >= 2
    w = np.zeros((4, tm, tm + 2 * GRID_W), np.float32)
    for t in range(tm):
        me = GRID_W + t
        if t > 0:
            w[0, t, me - 1] = 0.5
        if t < tm - 1:
            w[0, t, me + 1] = 0.5
        for variant, (has_above, has_below) in ((1, (False, True)), (2, (True, True)), (3, (True, False))):
            if t % GRID_W > 0:
                w[variant, t, me - 1] = 0.25
            if t % GRID_W < GRID_W - 1:
                w[variant, t, me + 1] = 0.25
            if t >= GRID_W or has_above:
                w[variant, t, me - GRID_W] = 0.25
            if t < tm - GRID_W or has_below:
                w[variant, t, me + GRID_W] = 0.25
    return jnp.asarray(w, BF16)


def _rwkv_prep(zs, mu, w2p, a2p, g2p, w0, a0, k_k, k_a, r_k, *, n_ctx, t_ctx, t_lat):
    n, ncol = zs.shape
    tm = t_ctx
    assert t_lat % tm == 0 and tm % GRID_W == 0 and n_ctx % tm == 0
    halo_per_tile = tm // GRID_W
    n_halo = n // GRID_W
    n_ctx_tiles, lat_tiles = n_ctx // tm, t_lat // tm

    def variant(i):
        jj = lax.rem(jnp.maximum(i - n_ctx_tiles, 0), lat_tiles)
        return jnp.where(i < n_ctx_tiles, 0, jnp.where(jj == 0, 1, jnp.where(jj == lat_tiles - 1, 3, 2)))
    head_of_lane = jnp.arange(LANES) // HEAD
    seg = (head_of_lane[:, None] == head_of_lane[None, :]).astype(BF16)
    full = lambda shape: pl.BlockSpec(shape, lambda i: (0,) * len(shape))
    row = pl.BlockSpec((tm, D_A), lambda i: (i, 0))
    out = jax.ShapeDtypeStruct((n, D_A), F32)
    return pl.pallas_call(
        _prep_kernel,
        grid=(n // tm,),
        in_specs=[pl.BlockSpec((tm, ncol), lambda i: (i, 0)),
                  pl.BlockSpec((GRID_W, ncol), lambda i: (jnp.maximum(i * halo_per_tile - 1, 0), 0)),
                  pl.BlockSpec((GRID_W, ncol), lambda i: (jnp.minimum((i + 1) * halo_per_tile, n_halo - 1), 0)),
                  pl.BlockSpec((1, tm, tm + 2 * GRID_W), lambda i: (variant(i), 0, 0)),
                  full((1, ncol)),
                  full((2, LANES, D_A)), full((2, LANES, D_A)), full((LORA_G_PAD, D_A)),
                  full((2, D_A)), full((2, D_A)), full((1, D_A)), full((1, D_A)), full((1, D_A)),
                  full((LANES, LANES))],
        out_specs=[row] * 11,
        out_shape=[out, jax.ShapeDtypeStruct((n, D_A), BF16)] + [out] * 9,
        scratch_shapes=[pltpu.VMEM((tm, ncol), F32)],
        compiler_params=_cparams(("parallel",)),
        name="rwkv_prep",
    )(zs, zs, zs, _neighbour_weights(tm, lat_tiles), mu.reshape(1, ncol), w2p, a2p, g2p, w0, a0,
      k_k.reshape(1, D_A), k_a.reshape(1, D_A), r_k.reshape(1, D_A), seg)


def _blockdiag(x, lane_a):
    return jnp.concatenate([jnp.where(lane_a, x, 0.0), jnp.where(lane_a, 0.0, x)], axis=0)


def _pair_mm(a, b, lane_a):
    return _dot(a.astype(BF16), _blockdiag(b, lane_a).astype(BF16))


def _scan_chunks(chains):
    cs = SCAN_C
    ids = range(len(chains))
    lane_a = chains[0][6][0]
    bd_mask = chains[0][6][6]
    rows = [pl.ds(pl.multiple_of(ch[0] * cs, cs), cs) for ch in chains]
    lanes = [slice(ch[2] * PAIR, (ch[2] + 1) * PAIR) for ch in chains]
    load = lambda k: [chains[i][3][k][rows[i], lanes[i]] for i in ids]
    r, v, kk, lw, kd, bb = (load(k) for k in range(6))
    v = [x.astype(F32) for x in v]
    tri2, strict, incl, eye2, level_masks = ([chains[i][6][k] for i in ids] for k in range(1, 6))

    def split2(x):
        hi = x.astype(BF16)
        return jnp.concatenate([hi, (x - hi.astype(F32)).astype(BF16)], axis=0)

    lp_dir = {}
    for i in ids:
        if chains[i][1] not in lp_dir:
            lp_dir[chains[i][1]] = _dot(tri2[i], split2(chains[i][3][3][rows[i], :]))
    lp = [lp_dir[chains[i][1]][:, lanes[i]] for i in ids]
    total = [lp[i][0:1, :] if chains[i][1] else lp[i][cs - 1:cs, :] for i in ids]
    p_inv = [jnp.exp(-lp[i]) for i in ids]
    lhs = [jnp.concatenate([kk[i] * jnp.exp(lp[i] - lw[i]), r[i] * jnp.exp(lp[i])], axis=0).astype(BF16)
           for i in ids]
    rhs = [jnp.concatenate([_blockdiag(bb[i] * p_inv[i], lane_a), _blockdiag(kd[i] * p_inv[i], lane_a)],
                           axis=0).astype(BF16) for i in ids]
    gram = [_dot_nt(lhs[i], rhs[i]) for i in ids]
    l_beta = [jnp.where(strict[i], gram[i][0:cs, 0:PAIR], 0.0) for i in ids]
    l_kappa = [jnp.where(strict[i], gram[i][0:cs, PAIR:2 * PAIR], 0.0).astype(BF16) for i in ids]
    m_both = [jnp.concatenate([jnp.where(incl[i], gram[i][cs:2 * cs, PAIR:2 * PAIR], 0.0),
                               jnp.where(incl[i], -gram[i][cs:2 * cs, 0:PAIR], 0.0)], axis=1).astype(BF16)
              for i in ids]

    dinv = [eye2[i] - jnp.where(level_masks[i][0], l_beta[i], 0.0) for i in ids]
    for lvl in range(1, len(level_masks[0])):
        x = [_pair_mm(jnp.where(level_masks[i][lvl], l_beta[i], 0.0), dinv[i], lane_a) for i in ids]
        dinv = [dinv[i] - _pair_mm(dinv[i], x[i], lane_a) for i in ids]

    s0 = [chains[i][4][chains[i][2]] for i in ids]
    from_state = [_dot_nt(lhs[i], s0[i].astype(BF16)) for i in ids]
    v_bd = [_blockdiag(v[i], lane_a).astype(BF16) for i in ids]
    w_rhs = [from_state[i][0:cs] + _dot(l_kappa[i], v_bd[i]) for i in ids]
    u = [_pair_mm(dinv[i], w_rhs[i], lane_a) for i in ids]
    y = [from_state[i][cs:2 * cs]
         + _dot(m_both[i], jnp.concatenate([v_bd[i], _blockdiag(u[i], lane_a).astype(BF16)], axis=0)) for i in ids]
    for i in ids:
        chains[i][5][rows[i], lanes[i]] = y[i]

    to_end = [jnp.exp(total[i] - lp[i]) for i in ids]
    zt = [jnp.concatenate([v[i], -u[i]], axis=0).T.astype(BF16) for i in ids]
    kb_end = [jnp.concatenate([kd[i] * to_end[i], bb[i] * to_end[i]], axis=0).astype(BF16) for i in ids]
    upd = [_dot(zt[i], kb_end[i]) for i in ids]
    for i in ids:
        chains[i][4][chains[i][2]] = s0[i] * jnp.exp(total[i]) + jnp.where(bd_mask, upd[i], 0.0)


def _scan_consts():
    cs = SCAN_C
    lane = lax.broadcasted_iota(jnp.int32, (cs, PAIR), 1)
    t = lax.broadcasted_iota(jnp.int32, (cs, PAIR), 0)
    i = jnp.bitwise_and(lane, HEAD - 1)
    lane_a = lane < HEAD
    out = {}
    for reverse in (False, True):
        before = (i > t) if reverse else (i < t)
        strict = before
        incl = jnp.logical_or(before, i == t)
        tri2 = incl.astype(BF16)
        eye2 = (i == t).astype(F32)
        level_masks = []
        s = 1
        while s < cs:
            sh_s = s.bit_length() - 1
            same_2s = lax.shift_right_logical(t, sh_s + 1) == lax.shift_right_logical(i, sh_s + 1)
            diff_s = lax.shift_right_logical(t, sh_s) != lax.shift_right_logical(i, sh_s)
            level_masks.append(jnp.logical_and(jnp.logical_and(same_2s, diff_s), strict))
            s *= 2
        out[reverse] = (tri2, strict, incl, eye2, tuple(level_masks))
    row = lax.broadcasted_iota(jnp.int32, (PAIR, PAIR), 0)
    col = lax.broadcasted_iota(jnp.int32, (PAIR, PAIR), 1)
    bd_mask = (row < HEAD) == (col < HEAD)
    return lane_a, out, bd_mask


def _scan_kernel(rf, vf, kkf, lw0, kd0, bb0, rb, vb, kkb, lw1, kd1, bb1, s0f_ref, s0b_ref,
                 yf_ref, yb_ref, sf_ref, sb_ref, sf_scr, sb_scr, *, n_ctx_seq, lat_tb):
    g = pl.program_id(0)
    is_ctx = g < n_ctx_seq
    j = lax.rem(jnp.maximum(g - n_ctx_seq, 0), lat_tb)
    n_chunk = SCAN_TB // SCAN_C
    lane_a, per_dir, bd_mask = _scan_consts()
    consts_f = (lane_a,) + per_dir[False] + (bd_mask,)
    consts_b = (lane_a,) + per_dir[True] + (bd_mask,)

    @pl.when(is_ctx)
    def _():
        sf_scr[...] = jnp.zeros_like(sf_scr)
        sb_scr[...] = jnp.zeros_like(sb_scr)

    @pl.when(jnp.logical_and(jnp.logical_not(is_ctx), j == 0))
    def _():
        sf_scr[...] = s0f_ref[0]
        sb_scr[...] = s0b_ref[0]

    def body(c, carry):
        chains = []
        for p in range(N_PAIR):
            chains.append((c, False, p, (rf, vf, kkf, lw0, kd0, bb0), sf_scr, yf_ref, consts_f))
            chains.append((n_chunk - 1 - c, True, p, (rb, vb, kkb, lw1, kd1, bb1), sb_scr, yb_ref, consts_b))
        _scan_chunks(chains)
        return carry

    lax.fori_loop(0, n_chunk, body, 0)

    @pl.when(jnp.logical_or(is_ctx, j == lat_tb - 1))
    def _():
        sf_ref[0] = sf_scr[...]
        sb_ref[0] = sb_scr[...]


def _rwkv_scan(prep, s0f, s0b, *, n_ctx_seq, n_lat_seq, t_lat):
    lat_tb = t_lat // SCAN_TB
    n_blocks = n_ctx_seq + n_lat_seq * lat_tb
    n_seq = n_ctx_seq + n_lat_seq

    def lat_idx(g):
        q = jnp.maximum(g - n_ctx_seq, 0)
        return q // lat_tb, lax.rem(q, lat_tb)

    def bwd_block(g):
        s, j = lat_idx(g)
        return jnp.where(g < n_ctx_seq, g, n_ctx_seq + s * lat_tb + (lat_tb - 1 - j))

    def seq_of(g):
        return jnp.where(g < n_ctx_seq, g, n_ctx_seq + lat_idx(g)[0])

    fwd = pl.BlockSpec((SCAN_TB, D_A), lambda g: (g, 0))
    bwd = pl.BlockSpec((SCAN_TB, D_A), lambda g: (bwd_block(g), 0))
    st_in = pl.BlockSpec((1, N_PAIR, PAIR, PAIR), lambda g: (lat_idx(g)[0], 0, 0, 0))
    st_out = pl.BlockSpec((1, N_PAIR, PAIR, PAIR), lambda g: (seq_of(g), 0, 0, 0))
    st_shape = jax.ShapeDtypeStruct((n_seq, N_PAIR, PAIR, PAIR), F32)
    y_shape = jax.ShapeDtypeStruct((n_blocks * SCAN_TB, D_A), F32)
    r, v, kk, lw0, kd0, bb0, lw1, kd1, bb1 = prep
    return pl.pallas_call(
        functools.partial(_scan_kernel, n_ctx_seq=n_ctx_seq, lat_tb=lat_tb),
        grid=(n_blocks,),
        in_specs=[fwd] * 6 + [bwd] * 6 + [st_in, st_in],
        out_specs=[fwd, bwd, st_out, st_out],
        out_shape=[y_shape, y_shape, st_shape, st_shape],
        scratch_shapes=[pltpu.VMEM((N_PAIR, PAIR, PAIR), F32), pltpu.VMEM((N_PAIR, PAIR, PAIR), F32)],
        compiler_params=_cparams(("arbitrary",)),
        name="rwkv_scan",
    )(r, v, kk, lw0, kd0, bb0, r, v, kk, lw1, kd1, bb1, s0f, s0b)


def _gmlp_kernel(uv_ref, lg_ref, lb_ref, ws_ref, bs_ref, o_ref, *, n_chunk):
    u = uv_ref[:, 0:D_B].astype(F32)
    v = uv_ref[:, D_B:2 * D_B].astype(F32)
    mu = jnp.mean(v, axis=-1, keepdims=True)
    vc = v - mu
    var = jnp.mean(vc * vc, axis=-1, keepdims=True)
    vn = ((vc * lax.rsqrt(var + LN_EPS)) * lg_ref[...] + lb_ref[...]).astype(BF16)
    for c in range(n_chunk):
        rows = slice(c * CHUNK, (c + 1) * CHUNK)
        for g in range(G_B):
            cols = slice(g * LANES, (g + 1) * LANES)
            s = _dot(ws_ref[g], vn[rows, cols]) + bs_ref[g]
            o_ref[rows, cols] = (u[rows, cols] * s).astype(o_ref.dtype)


def _gmlp(uv, ln_g, ln_b, ws, bs_b, *, n_chunk=GMLP_CHUNKS):
    n = uv.shape[0]
    tm = n_chunk * CHUNK
    full = lambda shape: pl.BlockSpec(shape, lambda i: (0,) * len(shape))
    return pl.pallas_call(
        functools.partial(_gmlp_kernel, n_chunk=n_chunk),
        grid=(n // tm,),
        in_specs=[pl.BlockSpec((tm, 2 * D_B), lambda i: (i, 0)),
                  full((1, D_B)), full((1, D_B)), full((G_B, CHUNK, CHUNK)), full((G_B, CHUNK, LANES))],
        out_specs=pl.BlockSpec((tm, D_B), lambda i: (i, 0)),
        out_shape=jax.ShapeDtypeStruct((n, D_B), BF16),
        compiler_params=_cparams(("parallel",)),
        name="gmlp",
    )(uv, ln_g.reshape(1, D_B), ln_b.reshape(1, D_B), ws, bs_b)


def _merge_kernel(yf_ref, yr_ref, bonus_ref, g_ref, gng_ref, gnb_ref, avg_ref, yb_ref, ga_ref, gb_ref, wpa_ref, wpb_ref,
                  o_ref):
    avg = avg_ref[...]
    parts = []
    for p in range(N_PAIR):
        cols = slice(p * LANES, (p + 1) * LANES)
        yp = yf_ref[:, cols] + yr_ref[:, cols]
        yc = yp - _dot2(yp, avg)
        yn = yc * lax.rsqrt(_dot2(yc * yc, avg) + GN_EPS)
        parts.append(((yn * gng_ref[:, cols] + gnb_ref[:, cols] + bonus_ref[:, cols]) * g_ref[:, cols]).astype(BF16))
    oa = jnp.concatenate(parts, axis=1)
    yb = yb_ref[...]
    for n0 in range(0, o_ref.shape[1], FFN_OUT_CHUNK):
        cols = slice(n0, n0 + FFN_OUT_CHUNK)
        o_ref[:, cols] = (ga_ref[:, cols].astype(F32) * _dot(oa, wpa_ref[:, cols])
                          + gb_ref[:, cols].astype(F32) * _dot(yb, wpb_ref[:, cols])).astype(o_ref.dtype)


def _merge(y_fwd, y_bwd, bonus, g, gn_g, gn_b, yb, act, w_pa, w_pb, *, tm=HALF_ROW_TILE):
    n = y_fwd.shape[0]
    head_of_lane = jnp.arange(LANES) // HEAD
    avg = ((head_of_lane[:, None] == head_of_lane[None, :]).astype(F32) / HEAD).astype(BF16)
    full = lambda shape: pl.BlockSpec(shape, lambda i: (0,) * len(shape))
    row = lambda w, c=0: pl.BlockSpec((tm, w), lambda i: (i, c))
    return pl.pallas_call(
        _merge_kernel,
        grid=(n // tm,),
        in_specs=[row(D_A), row(D_A), row(D_A), row(D_A), full((1, D_A)), full((1, D_A)), full((LANES, LANES)),
                  row(D_B), row(D_MODEL, 1), row(D_MODEL, 2), full((D_A, D_MODEL)), full((D_B, D_MODEL))],
        out_specs=row(D_MODEL),
        out_shape=jax.ShapeDtypeStruct((n, D_MODEL), BF16),
        compiler_params=_cparams(("parallel",)),
        name="merge",
    )(y_fwd, y_bwd, bonus, g, gn_g.reshape(1, D_A), gn_b.reshape(1, D_A), avg, yb, act, act, w_pa, w_pb)


def _w1_layout_kernel(a_ref, *refs):
    b_refs, o_ref = refs[:-1], refs[-1]
    valid = D_FF - pl.program_id(1) * D_FF_TILE
    keep = lax.broadcasted_iota(jnp.int32, (1, D_FF_TILE), 1) < valid
    a = jnp.where(keep, a_ref[0], 0.0)
    b = jnp.where(keep, jnp.concatenate([r[0] for r in b_refs], axis=1), 0.0)
    o_ref[0] = jnp.concatenate([a, b], axis=1).astype(BF16)


def _w2_layout_kernel(*refs):
    w_refs, o_ref = refs[:-1], refs[-1]
    valid = D_FF - pl.program_id(1) * D_FF_TILE
    keep = lax.broadcasted_iota(jnp.int32, (D_FF_TILE, 1), 0) < valid
    o_ref[0] = jnp.where(keep, jnp.concatenate([r[0] for r in w_refs], axis=0), 0.0).astype(BF16)


def _ffn_weight_layouts(w_in, w_out):
    s, d, _ = w_in.shape
    nj = D_FF_PAD // D_FF_TILE
    sub = D_FF_TILE // LANES
    first_b = D_FF // LANES
    last_blk = 2 * D_FF // LANES - 1
    b_spec = lambda k: pl.BlockSpec((1, d, LANES), lambda i, j: (i, 0, jnp.minimum(first_b + sub * j + k, last_blk)))
    w1 = pl.pallas_call(
        _w1_layout_kernel,
        grid=(s, nj),
        in_specs=[pl.BlockSpec((1, d, D_FF_TILE), lambda i, j: (i, 0, j))] + [b_spec(k) for k in range(sub)],
        out_specs=pl.BlockSpec((1, d, 2 * D_FF_TILE), lambda i, j: (i, 0, j)),
        out_shape=jax.ShapeDtypeStruct((s, d, 2 * D_FF_PAD), BF16),
        compiler_params=_cparams(("parallel", "parallel")),
        name="w1_layout",
    )(w_in, *([w_in] * sub))
    last_row_blk = D_FF // LANES - 1
    r_spec = lambda k: pl.BlockSpec((1, LANES, d), lambda i, j: (i, jnp.minimum(sub * j + k, last_row_blk), 0))
    w2 = pl.pallas_call(
        _w2_layout_kernel,
        grid=(s, nj),
        in_specs=[r_spec(k) for k in range(sub)],
        out_specs=pl.BlockSpec((1, D_FF_TILE, d), lambda i, j: (i, j, 0)),
        out_shape=jax.ShapeDtypeStruct((s, D_FF_PAD, d), BF16),
        compiler_params=_cparams(("parallel", "parallel")),
        name="w2_layout",
    )(*([w_out] * sub))
    return w1, w2


def _cast_rows_t_kernel(*refs, row_off, row0, n_valid):
    in_refs, o_ref = refs[:-1], refs[-1]
    tn = o_ref.shape[1]
    rows = in_refs[0][...]
    if row_off:
        rows = jnp.concatenate([rows[row_off:tn, :], in_refs[1][0:row_off, :]], axis=0)
    col = pl.program_id(0) * tn + lax.broadcasted_iota(jnp.int32, (tn, 1), 0)
    o_ref[...] = jnp.where(col < n_valid, rows, 0.0).T.astype(BF16)


def _cast_rows_t(wt, row0, n_valid, n_out, *, tn):
    nrow, d = wt.shape
    blk0, row_off = divmod(row0, tn)
    assert n_out % tn == 0 and row_off % 8 == 0 and row0 + n_out <= pl.cdiv(nrow, tn) * tn
    last_blk = pl.cdiv(nrow, tn) - 1
    spec = lambda k: pl.BlockSpec((tn, d), lambda j: (jnp.minimum(blk0 + j + k, last_blk), 0))
    n_in = 2 if row_off else 1
    return pl.pallas_call(
        functools.partial(_cast_rows_t_kernel, row_off=row_off, row0=row0, n_valid=n_valid),
        grid=(n_out // tn,),
        in_specs=[spec(k) for k in range(n_in)],
        out_specs=pl.BlockSpec((d, tn), lambda j: (0, j)),
        out_shape=jax.ShapeDtypeStruct((d, n_out), BF16),
        compiler_params=_cparams(("parallel",)),
        name="cast_rows_t",
    )(*([wt] * n_in))


def _pair_states(s):
    b = s.shape[0]
    s = s.reshape(b, N_PAIR, 2, HEAD, HEAD)
    z = jnp.zeros_like(s[:, :, 0])
    top = jnp.concatenate([s[:, :, 0], z], axis=-1)
    bot = jnp.concatenate([z, s[:, :, 1]], axis=-1)
    return jnp.concatenate([top, bot], axis=-2)


def _unpair_states(sp):
    b = sp.shape[0]
    return jnp.stack([sp[:, :, :HEAD, :HEAD], sp[:, :, HEAD:, HEAD:]], axis=2).reshape(b, H_A, HEAD, HEAD)


def kernel(x_prompt, x_sample, c, state_rwkv_fwd, state_rwkv_bwd, c_ctx, w_ada, b_ada, ln_g, ln_b,
           ffn_w_in, ffn_w_out, w_in, shift_mu, rw_w0, rw_w2, rw_a0, rw_a2, rw_g2, rw_k_k, rw_k_a,
           rw_r_k, rw_gn_g, rw_gn_b, sg_ln_g, sg_ln_b, sg_w, sg_b, w_pa, w_pb, w_o):
    n_ctx_seq, t_ctx, d = x_prompt.shape
    n_lat_seq, t_lat, _ = x_sample.shape
    n_ctx = n_ctx_seq * t_ctx
    n_lat = n_lat_seq * t_lat
    tm = ROW_TILE
    ctx_tiles = n_ctx // tm

    def group_of_tile(rows):
        assert n_ctx % rows == 0 and t_lat % rows == 0
        return lambda i: jnp.where(i < n_ctx // rows, 0, 1 + (i - n_ctx // rows) // (t_lat // rows))

    group_of = group_of_tile(tm)

    x = jnp.concatenate([x_prompt.reshape(n_ctx, d), x_sample.reshape(n_lat, d)], axis=0)
    cond8 = jnp.zeros((16, d), F32).at[0].set(c_ctx).at[1:1 + n_lat_seq].set(c)
    new_f, new_b = [], []
    for l in range(DEPTH):
        w1, w2 = _ffn_weight_layouts(ffn_w_in[l], ffn_w_out[l])
        wt = w_in[l].T
        w_zs = _cast_rows_t(wt, 0, N_SHIFT, N_SHIFT_PAD, tn=CAST_HEAD_TN)
        w_act = _cast_rows_t(wt, N_SHIFT, wt.shape[0] - N_SHIFT, wt.shape[0] - N_SHIFT, tn=CAST_SHIFT_TN)
        mu_p = jnp.pad(shift_mu[l], (0, N_SHIFT_PAD - N_SHIFT))
        zpad = jnp.zeros((2, LANES - LORA_W, D_A), F32)
        w2p = jnp.concatenate([rw_w2[l], zpad], axis=1).astype(BF16)
        a2p = jnp.concatenate([zpad, rw_a2[l]], axis=1).astype(BF16)
        g2p = jnp.pad(rw_g2[l], ((0, LORA_G_PAD - LORA_G), (0, 0))).astype(BF16)
        bs_b = jnp.broadcast_to(sg_b[l][:, :, None], (G_B, CHUNK, LANES))

        mod = _ada(cond8, w_ada[l], b_ada[l]).reshape(16, 9, d)

        x = _ffn(x, mod, w1, w2, ln_g[l, 0], ln_b[l, 0], group_of, which=0, tm=tm)

        zs = _mm_mod(x, mod, w_zs, group_of_tile(HALF_ROW_TILE), which=1, act=_act_none, tm=HALF_ROW_TILE, tn=N_SHIFT_PAD,
                     tn_out=N_SHIFT_PAD, out_dtype=F32)
        act = _mm_mod(x, mod, w_act, group_of, which=1, act=_act_gelu, act_tail=_act_sigmoid,
                      tail_from=2 * D_B // ACT_TN, tm=tm, tn=ACT_TN, tn_out=ACT_TN, out_dtype=BF16)
        prep = _rwkv_prep(zs, mu_p, w2p, a2p, g2p, rw_w0[l], rw_a0[l], rw_k_k[l], rw_k_a[l], rw_r_k[l].reshape(D_A),
                          n_ctx=n_ctx, t_ctx=t_ctx, t_lat=t_lat)
        bonus, g_out = prep[9], prep[10]
        assert t_ctx == SCAN_TB
        y_fwd, y_bwd, sf, sb = _rwkv_scan(prep[:9], _pair_states(state_rwkv_fwd[:, l]),
                                          _pair_states(state_rwkv_bwd[:, l]),
                                          n_ctx_seq=n_ctx_seq, n_lat_seq=n_lat_seq, t_lat=t_lat)
        new_f.append(_unpair_states(sf[:n_ctx_seq]))
        new_b.append(_unpair_states(sb[:n_ctx_seq]))
        assert 2 * D_B == D_MODEL
        yb = _gmlp(act, sg_ln_g[l], sg_ln_b[l], sg_w[l].astype(BF16), bs_b)
        merged = _merge(y_fwd, y_bwd, bonus, g_out, rw_gn_g[l], rw_gn_b[l], yb, act,
                        w_pa[l].astype(BF16), w_pb[l].astype(BF16))
        x = _mm_res_ln(merged, w_o[l].astype(BF16), x, mod, ln_g[l, 1], ln_b[l, 1], group_of, gate_row=5, coef=1.0,
                       tm=tm, tk=WO_TK)

        ffn2 = functools.partial(_ffn, x, mod, w1, w2, ln_g[l, 2], ln_b[l, 2], group_of, which=2, tm=tm)
        x_ctx = ffn2(tile0=0, n_tiles=ctx_tiles)
        x_lat = ffn2(tile0=ctx_tiles, n_tiles=n_lat // tm)
        if l + 1 < DEPTH:
            x = jnp.concatenate([x_ctx, x_lat], axis=0)

    y_prompt = x_ctx.reshape(n_ctx_seq, t_ctx, d)
    y_sample = x_lat.reshape(n_lat_seq, t_lat, d)
    new_state_fwd = jnp.stack(new_f, axis=1).astype(x_prompt.dtype)
    new_state_bwd = jnp.stack(new_b, axis=1).astype(x_prompt.dtype)
    return (y_prompt, y_sample, new_state_fwd, new_state_bwd)
```

```python
import functools

import numpy as np
import jax
import jax.numpy as jnp
from jax import lax
from jax.experimental import pallas as pl
from jax.experimental.pallas import tpu as pltpu

F32 = jnp.float32
BF16 = jnp.bfloat16

D_MODEL = 2048
DEPTH = 1
D_A = 1024
HEAD = 64
H_A = D_A // HEAD
LORA_W = 64
LORA_A = 64
LORA_G = 160
D_B = 1024
CHUNK = 128
G_B = 8
D_FF = 5504
N_SHIFT = 3 * D_A + LORA_W + LORA_A + LORA_G
GRID_W = 64
ALPHA = (2.0 * DEPTH) ** 0.25
LN_EPS = 1e-5
GN_EPS = 64e-5
NRM_EPS = 1e-12
NEG_EXP_HALF = -float(np.exp(-0.5))

LANES = 128
VMEM_LIMIT = 56 * 1024 * 1024

LORA_G_PAD = 256
N_SHIFT_PAD = 3 * D_A + LORA_W + LORA_A + LORA_G_PAD
D_FF_TILE = 512
D_FF_PAD = ((D_FF + D_FF_TILE - 1) // D_FF_TILE) * D_FF_TILE
FFN_OUT_CHUNK = 512
LN_ROWS = 256
MM_ACT_CHUNK = 256
MM_COPY_CHUNK = 1152
ROW_TILE = 1024
HALF_ROW_TILE = ROW_TILE // 2
ADA_TN = 1024
ACT_TN = 1024
WO_TK = 1024
GMLP_CHUNKS = 4
CAST_HEAD_TN = 384
CAST_SHIFT_TN = 512
SCAN_C = 64
SCAN_TB = 256
PAIR = 2 * HEAD
N_PAIR = H_A // 2


def _cparams(sem):
    return pltpu.CompilerParams(dimension_semantics=sem, vmem_limit_bytes=VMEM_LIMIT)


def _dot(a, b):
    return jnp.dot(a, b, preferred_element_type=F32)


def _dot_nt(a, b):
    return lax.dot_general(a, b, (((1,), (1,)), ((), ())), preferred_element_type=F32)


def _dot2(x, m):
    hi = x.astype(BF16)
    lo = (x - hi.astype(F32)).astype(BF16)
    return _dot(hi, m) + _dot(lo, m)


def _ada_kernel(c_ref, w_ref, b_ref, o_ref):
    c = c_ref[...]
    s = (c * jax.nn.sigmoid(c)).astype(BF16)
    o_ref[...] = _dot(s, w_ref[...].astype(BF16)) + b_ref[...]


def _ada(cond8, w_ada, b_ada, tn=ADA_TN):
    rows, d = cond8.shape
    n = w_ada.shape[1]
    return pl.pallas_call(
        _ada_kernel,
        grid=(n // tn,),
        in_specs=[pl.BlockSpec((rows, d), lambda j: (0, 0)),
                  pl.BlockSpec((d, tn), lambda j: (0, j)),
                  pl.BlockSpec((1, tn), lambda j: (0, j))],
        out_specs=pl.BlockSpec((rows, tn), lambda j: (0, j)),
        out_shape=jax.ShapeDtypeStruct((rows, n), F32),
        compiler_params=_cparams(("parallel",)),
        name="ada",
    )(cond8, w_ada, b_ada.reshape(1, n))


def _act_none(acc):
    return acc


def _act_gelu(acc):
    return 0.5 * acc * (1.0 + lax.erf(acc * (2.0 ** -0.5)))


def _act_sigmoid(acc):
    return 0.5 * jnp.tanh(0.5 * acc) + 0.5


def _mm_mod_kernel(x_ref, mod_ref, w_ref, o_ref, h_ref, *, which, act, act_tail, tail_from):
    j = pl.program_id(1)

    @pl.when(j == 0)
    def _():
        shift = mod_ref[0, 3 * which:3 * which + 1, :]
        scale = mod_ref[0, 3 * which + 1:3 * which + 2, :]
        h_ref[...] = (x_ref[...] * (1.0 + scale) + shift).astype(BF16)

    def tile(fn):
        step = MM_COPY_CHUNK if fn is _act_none else MM_ACT_CHUNK
        for n0 in range(0, o_ref.shape[1], step):
            o_ref[:, n0:n0 + step] = fn(_dot(h_ref[...], w_ref[:, n0:n0 + step])).astype(o_ref.dtype)

    if act_tail is None:
        tile(act)
    else:
        pl.when(j < tail_from)(lambda: tile(act))
        pl.when(j >= tail_from)(lambda: tile(act_tail))


def _mm_mod(x, mod3, w, group_of, *, which, act, tm, tn, tn_out, out_dtype, act_tail=None, tail_from=0):
    n, d = x.shape
    nj = w.shape[1] // tn
    w_mode = dict(pipeline_mode=pl.Buffered(1)) if nj == 1 else {}
    return pl.pallas_call(
        functools.partial(_mm_mod_kernel, which=which, act=act, act_tail=act_tail, tail_from=tail_from),
        grid=(n // tm, nj),
        in_specs=[pl.BlockSpec((tm, d), lambda i, j: (i, 0)),
                  pl.BlockSpec((1, 9, d), lambda i, j: (group_of(i), 0, 0)),
                  pl.BlockSpec((d, tn), lambda i, j: (0, j), **w_mode)],
        out_specs=pl.BlockSpec((tm, tn_out), lambda i, j: (i, j)),
        out_shape=jax.ShapeDtypeStruct((n, nj * tn_out), out_dtype),
        scratch_shapes=[pltpu.VMEM((tm, d), BF16)],
        compiler_params=_cparams(("parallel", "arbitrary")),
        name="mm_mod",
    )(x, mod3, w)


def _mm_res_ln_kernel(a_ref, w_ref, x_ref, mod_ref, g_ref, b_ref, o_ref, *, gate_row, coef, nk):
    k = pl.program_id(1)
    assert nk >= 2
    half = o_ref.shape[0] // 2

    def step(first, last):
        for r0 in (0, half):
            rows = slice(r0, r0 + half)
            for n0 in range(0, o_ref.shape[1], FFN_OUT_CHUNK):
                cols = slice(n0, n0 + FFN_OUT_CHUNK)
                part = _dot(a_ref[rows, :], w_ref[:, cols])
                if first:
                    o_ref[rows, cols] = part
                else:
                    o_ref[rows, cols] += part
            if last:
                gate = coef * mod_ref[0, gate_row:gate_row + 1, :]
                for q0 in range(r0, r0 + half, LN_ROWS):
                    q = slice(q0, q0 + LN_ROWS)
                    o_ref[q, :] = _res_ln(x_ref[q, :], gate * o_ref[q, :], g_ref[...], b_ref[...])

    pl.when(k == 0)(lambda: step(True, False))
    if nk > 2:
        pl.when(jnp.logical_and(k > 0, k < nk - 1))(lambda: step(False, False))
    pl.when(k == nk - 1)(lambda: step(False, True))


def _mm_res_ln(a, w, x, mod3, ln_g, ln_b, group_of, *, gate_row, coef, tm, tk):
    n, kdim = a.shape
    d = w.shape[1]
    nk = kdim // tk
    return pl.pallas_call(
        functools.partial(_mm_res_ln_kernel, gate_row=gate_row, coef=coef, nk=nk),
        grid=(n // tm, nk),
        in_specs=[pl.BlockSpec((tm, tk), lambda i, k: (i, k)),
                  pl.BlockSpec((tk, d), lambda i, k: (k, 0)),
                  pl.BlockSpec((tm, d), lambda i, k: (i, 0)),
                  pl.BlockSpec((1, 9, d), lambda i, k: (group_of(i), 0, 0)),
                  pl.BlockSpec((1, d), lambda i, k: (0, 0)),
                  pl.BlockSpec((1, d), lambda i, k: (0, 0))],
        out_specs=pl.BlockSpec((tm, d), lambda i, k: (i, 0)),
        out_shape=jax.ShapeDtypeStruct((n, d), F32),
        compiler_params=_cparams(("parallel", "arbitrary")),
        name="mm_res_ln",
    )(a, w, x, mod3, ln_g.reshape(1, d), ln_b.reshape(1, d))


def _ffn_kernel(x_ref, mod_ref, w1_ref, w2_ref, g_ref, b_ref, *rest, which, nj):
    o_ref, h_ref = rest[-2:]
    j = pl.program_id(1)
    assert nj >= 2
    half = h_ref.shape[0] // 2

    def step(first, last):
        for r0 in (0, half):
            rows = slice(r0, r0 + half)
            if first:
                shift = mod_ref[0, 3 * which:3 * which + 1, :]
                scale = mod_ref[0, 3 * which + 1:3 * which + 2, :]
                h_ref[rows, :] = (x_ref[rows, :] * (1.0 + scale) + shift).astype(BF16)
            h = h_ref[rows, :]
            a = _dot(h, w1_ref[0, :, 0:D_FF_TILE])
            hid = ((a * jax.nn.sigmoid(a)) * _dot(h, w1_ref[0, :, D_FF_TILE:2 * D_FF_TILE])).astype(BF16)
            for n0 in range(0, o_ref.shape[1], FFN_OUT_CHUNK):
                cols = slice(n0, n0 + FFN_OUT_CHUNK)
                part = _dot(hid, w2_ref[0, :, cols])
                if first:
                    o_ref[rows, cols] = part
                else:
                    o_ref[rows, cols] += part
            if last:
                gate = 0.5 * mod_ref[0, 3 * which + 2:3 * which + 3, :]
                for q0 in range(r0, r0 + half, LN_ROWS):
                    q = slice(q0, q0 + LN_ROWS)
                    o_ref[q, :] = _res_ln(x_ref[q, :], gate * o_ref[q, :], g_ref[...], b_ref[...])

    pl.when(j == 0)(lambda: step(True, False))
    pl.when(jnp.logical_and(j > 0, j < nj - 1))(lambda: step(False, False))
    pl.when(j == nj - 1)(lambda: step(False, True))


def _res_ln(x, f, g, b):
    y = ALPHA * x + f
    mu = jnp.mean(y, axis=-1, keepdims=True)
    yc = y - mu
    var = jnp.mean(yc * yc, axis=-1, keepdims=True)
    return yc * lax.rsqrt(var + LN_EPS) * g + b


def _ffn(x, mod3, w1, w2, ln_g, ln_b, group_of, *, which, tm, tile0=0, n_tiles=None, group_tile0=None,
         out_tile0=0, out_rows=None, into=None):
    n, d = x.shape
    n_tiles = n // tm if n_tiles is None else n_tiles
    group_tile0 = tile0 if group_tile0 is None else group_tile0
    out_rows = n_tiles * tm if out_rows is None else out_rows
    nj = w2.shape[1] // D_FF_TILE
    s = which // 2
    in_specs = [pl.BlockSpec((tm, d), lambda i, j: (tile0 + i, 0)),
                pl.BlockSpec((1, 9, d), lambda i, j: (group_of(group_tile0 + i), 0, 0)),
                pl.BlockSpec((1, d, 2 * D_FF_TILE), lambda i, j: (s, 0, j)),
                pl.BlockSpec((1, D_FF_TILE, d), lambda i, j: (s, j, 0)),
                pl.BlockSpec((1, d), lambda i, j: (0, 0)),
                pl.BlockSpec((1, d), lambda i, j: (0, 0))]
    args = [x, mod3, w1, w2, ln_g.reshape(1, d), ln_b.reshape(1, d)]
    aliases = {}
    if into is not None:
        assert into.shape == (out_rows, d)
        in_specs.append(pl.BlockSpec(memory_space=pl.ANY))
        args.append(into)
        aliases = {len(args) - 1: 0}
    return pl.pallas_call(
        functools.partial(_ffn_kernel, which=which, nj=nj),
        grid=(n_tiles, nj),
        in_specs=in_specs,
        out_specs=pl.BlockSpec((tm, d), lambda i, j: (out_tile0 + i, 0)),
        out_shape=jax.ShapeDtypeStruct((out_rows, d), F32),
        scratch_shapes=[pltpu.VMEM((tm, d), BF16)],
        input_output_aliases=aliases,
        compiler_params=_cparams(("parallel", "arbitrary")),
        name="ffn",
    )(*args)


def _prep_kernel(z_ref, za_ref, zb_ref, nbw_ref, mu_ref, w2_ref, a2_ref, g2_ref, w0_ref, a0_ref, kk_ref_, ka_ref, rk_ref,
                 seg_ref, r_o, v_o, kk_o, lw0_o, kd0_o, bb0_o, lw1_o, kd1_o, bb1_o, bonus_o, g_o, zsh):
    z = z_ref[...]
    stacked = jnp.concatenate([za_ref[...], z, zb_ref[...]], axis=0)
    nbw = nbw_ref[0]
    t0 = stacked.astype(BF16)
    t1 = (stacked - t0.astype(F32)).astype(BF16)
    nb = _dot(nbw, t0) + _dot(nbw, t1)
    zsh[...] = z + mu_ref[...] * (nb - z)

    r = zsh[:, 0:D_A]
    k = zsh[:, D_A:2 * D_A]
    v = zsh[:, 2 * D_A:3 * D_A]
    xwa = zsh[:, 3 * D_A:3 * D_A + LANES]
    xg = zsh[:, 3 * D_A + LANES:3 * D_A + LANES + LORA_G_PAD]
    seg = seg_ref[...]
    r_o[...] = r
    v_o[...] = v.astype(v_o.dtype)

    g_o[...] = _dot(jax.nn.sigmoid(xg).astype(BF16), g2_ref[...])

    kk = k * kk_ref_[...]
    ksq = kk * kk
    ss = jnp.concatenate([_dot2(ksq[:, p * LANES:(p + 1) * LANES], seg) for p in range(N_PAIR)], axis=1)
    kk = kk * lax.rsqrt(jnp.maximum(ss, NRM_EPS * NRM_EPS))
    kk_o[...] = kk

    tw = jnp.tanh(xwa).astype(BF16)
    xa = xwa.astype(BF16)
    k_a = ka_ref[...]
    r_k = rk_ref[...]
    outs = ((lw0_o, kd0_o, bb0_o), (lw1_o, kd1_o, bb1_o))
    rkd = jnp.zeros_like(r)
    for d in range(2):
        lw_o, kd_o, bb_o = outs[d]
        wl = w0_ref[d:d + 1, :] + _dot(tw, w2_ref[d])
        lw_o[...] = NEG_EXP_HALF * _act_sigmoid(wl)
        a = _act_sigmoid(a0_ref[d:d + 1, :] + _dot(xa, a2_ref[d]))
        kd = k * (1.0 + (a - 1.0) * k_a)
        kd_o[...] = kd
        bb_o[...] = kk * a
        rkd = rkd + r * kd * r_k
    rs = jnp.concatenate([_dot2(rkd[:, p * LANES:(p + 1) * LANES], seg) for p in range(N_PAIR)], axis=1)
    bonus_o[...] = rs * v


def _neighbour_weights(tm, lat_tiles):
    assert lat_tiles >= 2
    w = np.zeros((4, tm, tm + 2 * GRID_W), np.float32)
    for t in range(tm):
        me = GRID_W + t
        if t > 0:
            w[0, t, me - 1] = 0.5
        if t < tm - 1:
            w[0, t, me + 1] = 0.5
        for variant, (has_above, has_below) in ((1, (False, True)), (2, (True, True)), (3, (True, False))):
            if t % GRID_W > 0:
                w[variant, t, me - 1] = 0.25
            if t % GRID_W < GRID_W - 1:
                w[variant, t, me + 1] = 0.25
            if t >= GRID_W or has_above:
                w[variant, t, me - GRID_W] = 0.25
            if t < tm - GRID_W or has_below:
                w[variant, t, me + GRID_W] = 0.25
    return jnp.asarray(w, BF16)


def _rwkv_prep(zs, mu, w2p, a2p, g2p, w0, a0, k_k, k_a, r_k, *, n_ctx, t_ctx, t_lat):
    n, ncol = zs.shape
    tm = t_ctx
    assert t_lat % tm == 0 and tm % GRID_W == 0 and n_ctx % tm == 0
    halo_per_tile = tm // GRID_W
    n_halo = n // GRID_W
    n_ctx_tiles, lat_tiles = n_ctx // tm, t_lat // tm

    def variant(i):
        jj = lax.rem(jnp.maximum(i - n_ctx_tiles, 0), lat_tiles)
        return jnp.where(i < n_ctx_tiles, 0, jnp.where(jj == 0, 1, jnp.where(jj == lat_tiles - 1, 3, 2)))
    head_of_lane = jnp.arange(LANES) // HEAD
    seg = (head_of_lane[:, None] == head_of_lane[None, :]).astype(BF16)
    full = lambda shape: pl.BlockSpec(shape, lambda i: (0,) * len(shape))
    row = pl.BlockSpec((tm, D_A), lambda i: (i, 0))
    out = jax.ShapeDtypeStruct((n, D_A), F32)
    return pl.pallas_call(
        _prep_kernel,
        grid=(n // tm,),
        in_specs=[pl.BlockSpec((tm, ncol), lambda i: (i, 0)),
                  pl.BlockSpec((GRID_W, ncol), lambda i: (jnp.maximum(i * halo_per_tile - 1, 0), 0)),
                  pl.BlockSpec((GRID_W, ncol), lambda i: (jnp.minimum((i + 1) * halo_per_tile, n_halo - 1), 0)),
                  pl.BlockSpec((1, tm, tm + 2 * GRID_W), lambda i: (variant(i), 0, 0)),
                  full((1, ncol)),
                  full((2, LANES, D_A)), full((2, LANES, D_A)), full((LORA_G_PAD, D_A)),
                  full((2, D_A)), full((2, D_A)), full((1, D_A)), full((1, D_A)), full((1, D_A)),
                  full((LANES, LANES))],
        out_specs=[row] * 11,
        out_shape=[out, jax.ShapeDtypeStruct((n, D_A), BF16)] + [out] * 9,
        scratch_shapes=[pltpu.VMEM((tm, ncol), F32)],
        compiler_params=_cparams(("parallel",)),
        name="rwkv_prep",
    )(zs, zs, zs, _neighbour_weights(tm, lat_tiles), mu.reshape(1, ncol), w2p, a2p, g2p, w0, a0,
      k_k.reshape(1, D_A), k_a.reshape(1, D_A), r_k.reshape(1, D_A), seg)


def _blockdiag(x, lane_a):
    return jnp.concatenate([jnp.where(lane_a, x, 0.0), jnp.where(lane_a, 0.0, x)], axis=0)


def _pair_mm(a, b, lane_a):
    return _dot(a.astype(BF16), _blockdiag(b, lane_a).astype(BF16))


def _scan_chunks(chains):
    cs = SCAN_C
    ids = range(len(chains))
    lane_a = chains[0][6][0]
    bd_mask = chains[0][6][6]
    rows = [pl.ds(pl.multiple_of(ch[0] * cs, cs), cs) for ch in chains]
    lanes = [slice(ch[2] * PAIR, (ch[2] + 1) * PAIR) for ch in chains]
    load = lambda k: [chains[i][3][k][rows[i], lanes[i]] for i in ids]
    r, v, kk, lw, kd, bb = (load(k) for k in range(6))
    v = [x.astype(F32) for x in v]
    tri2, strict, incl, eye2, level_masks = ([chains[i][6][k] for i in ids] for k in range(1, 6))

    def split2(x):
        hi = x.astype(BF16)
        return jnp.concatenate([hi, (x - hi.astype(F32)).astype(BF16)], axis=0)

    lp_dir = {}
    for i in ids:
        if chains[i][1] not in lp_dir:
            lp_dir[chains[i][1]] = _dot(tri2[i], split2(chains[i][3][3][rows[i], :]))
    lp = [lp_dir[chains[i][1]][:, lanes[i]] for i in ids]
    total = [lp[i][0:1, :] if chains[i][1] else lp[i][cs - 1:cs, :] for i in ids]
    p_inv = [jnp.exp(-lp[i]) for i in ids]
    lhs = [jnp.concatenate([kk[i] * jnp.exp(lp[i] - lw[i]), r[i] * jnp.exp(lp[i])], axis=0).astype(BF16)
           for i in ids]
    rhs = [jnp.concatenate([_blockdiag(bb[i] * p_inv[i], lane_a), _blockdiag(kd[i] * p_inv[i], lane_a)],
                           axis=0).astype(BF16) for i in ids]
    gram = [_dot_nt(lhs[i], rhs[i]) for i in ids]
    l_beta = [jnp.where(strict[i], gram[i][0:cs, 0:PAIR], 0.0) for i in ids]
    l_kappa = [jnp.where(strict[i], gram[i][0:cs, PAIR:2 * PAIR], 0.0).astype(BF16) for i in ids]
    m_both = [jnp.concatenate([jnp.where(incl[i], gram[i][cs:2 * cs, PAIR:2 * PAIR], 0.0),
                               jnp.where(incl[i], -gram[i][cs:2 * cs, 0:PAIR], 0.0)], axis=1).astype(BF16)
              for i in ids]

    dinv = [eye2[i] - jnp.where(level_masks[i][0], l_beta[i], 0.0) for i in ids]
    for lvl in range(1, len(level_masks[0])):
        x = [_pair_mm(jnp.where(level_masks[i][lvl], l_beta[i], 0.0), dinv[i], lane_a) for i in ids]
        dinv = [dinv[i] - _pair_mm(dinv[i], x[i], lane_a) for i in ids]

    s0 = [chains[i][4][chains[i][2]] for i in ids]
    from_state = [_dot_nt(lhs[i], s0[i].astype(BF16)) for i in ids]
    v_bd = [_blockdiag(v[i], lane_a).astype(BF16) for i in ids]
    w_rhs = [from_state[i][0:cs] + _dot(l_kappa[i], v_bd[i]) for i in ids]
    u = [_pair_mm(dinv[i], w_rhs[i], lane_a) for i in ids]
    y = [from_state[i][cs:2 * cs]
         + _dot(m_both[i], jnp.concatenate([v_bd[i], _blockdiag(u[i], lane_a).astype(BF16)], axis=0)) for i in ids]
    for i in ids:
        chains[i][5][rows[i], lanes[i]] = y[i]

    to_end = [jnp.exp(total[i] - lp[i]) for i in ids]
    zt = [jnp.concatenate([v[i], -u[i]], axis=0).T.astype(BF16) for i in ids]
    kb_end = [jnp.concatenate([kd[i] * to_end[i], bb[i] * to_end[i]], axis=0).astype(BF16) for i in ids]
    upd = [_dot(zt[i], kb_end[i]) for i in ids]
    for i in ids:
        chains[i][4][chains[i][2]] = s0[i] * jnp.exp(total[i]) + jnp.where(bd_mask, upd[i], 0.0)


def _scan_consts():
    cs = SCAN_C
    lane = lax.broadcasted_iota(jnp.int32, (cs, PAIR), 1)
    t = lax.broadcasted_iota(jnp.int32, (cs, PAIR), 0)
    i = jnp.bitwise_and(lane, HEAD - 1)
    lane_a = lane < HEAD
    out = {}
    for reverse in (False, True):
        before = (i > t) if reverse else (i < t)
        strict = before
        incl = jnp.logical_or(before, i == t)
        tri2 = incl.astype(BF16)
        eye2 = (i == t).astype(F32)
        level_masks = []
        s = 1
        while s < cs:
            sh_s = s.bit_length() - 1
            same_2s = lax.shift_right_logical(t, sh_s + 1) == lax.shift_right_logical(i, sh_s + 1)
            diff_s = lax.shift_right_logical(t, sh_s) != lax.shift_right_logical(i, sh_s)
            level_masks.append(jnp.logical_and(jnp.logical_and(same_2s, diff_s), strict))
            s *= 2
        out[reverse] = (tri2, strict, incl, eye2, tuple(level_masks))
    row = lax.broadcasted_iota(jnp.int32, (PAIR, PAIR), 0)
    col = lax.broadcasted_iota(jnp.int32, (PAIR, PAIR), 1)
    bd_mask = (row < HEAD) == (col < HEAD)
    return lane_a, out, bd_mask


def _scan_kernel(rf, vf, kkf, lw0, kd0, bb0, rb, vb, kkb, lw1, kd1, bb1, s0f_ref, s0b_ref,
                 yf_ref, yb_ref, sf_ref, sb_ref, sf_scr, sb_scr, *, n_ctx_seq, lat_tb):
    g = pl.program_id(0)
    is_ctx = g < n_ctx_seq
    j = lax.rem(jnp.maximum(g - n_ctx_seq, 0), lat_tb)
    n_chunk = SCAN_TB // SCAN_C
    lane_a, per_dir, bd_mask = _scan_consts()
    consts_f = (lane_a,) + per_dir[False] + (bd_mask,)
    consts_b = (lane_a,) + per_dir[True] + (bd_mask,)

    @pl.when(is_ctx)
    def _():
        sf_scr[...] = jnp.zeros_like(sf_scr)
        sb_scr[...] = jnp.zeros_like(sb_scr)

    @pl.when(jnp.logical_and(jnp.logical_not(is_ctx), j == 0))
    def _():
        sf_scr[...] = s0f_ref[0]
        sb_scr[...] = s0b_ref[0]

    def body(c, carry):
        chains = []
        for p in range(N_PAIR):
            chains.append((c, False, p, (rf, vf, kkf, lw0, kd0, bb0), sf_scr, yf_ref, consts_f))
            chains.append((n_chunk - 1 - c, True, p, (rb, vb, kkb, lw1, kd1, bb1), sb_scr, yb_ref, consts_b))
        _scan_chunks(chains)
        return carry

    lax.fori_loop(0, n_chunk, body, 0)

    @pl.when(jnp.logical_or(is_ctx, j == lat_tb - 1))
    def _():
        sf_ref[0] = sf_scr[...]
        sb_ref[0] = sb_scr[...]


def _rwkv_scan(prep, s0f, s0b, *, n_ctx_seq, n_lat_seq, t_lat):
    lat_tb = t_lat // SCAN_TB
    n_blocks = n_ctx_seq + n_lat_seq * lat_tb
    n_seq = n_ctx_seq + n_lat_seq

    def lat_idx(g):
        q = jnp.maximum(g - n_ctx_seq, 0)
        return q // lat_tb, lax.rem(q, lat_tb)

    def bwd_block(g):
        s, j = lat_idx(g)
        return jnp.where(g < n_ctx_seq, g, n_ctx_seq + s * lat_tb + (lat_tb - 1 - j))

    def seq_of(g):
        return jnp.where(g < n_ctx_seq, g, n_ctx_seq + lat_idx(g)[0])

    fwd = pl.BlockSpec((SCAN_TB, D_A), lambda g: (g, 0))
    bwd = pl.BlockSpec((SCAN_TB, D_A), lambda g: (bwd_block(g), 0))
    st_in = pl.BlockSpec((1, N_PAIR, PAIR, PAIR), lambda g: (lat_idx(g)[0], 0, 0, 0))
    st_out = pl.BlockSpec((1, N_PAIR, PAIR, PAIR), lambda g: (seq_of(g), 0, 0, 0))
    st_shape = jax.ShapeDtypeStruct((n_seq, N_PAIR, PAIR, PAIR), F32)
    y_shape = jax.ShapeDtypeStruct((n_blocks * SCAN_TB, D_A), F32)
    r, v, kk, lw0, kd0, bb0, lw1, kd1, bb1 = prep
    return pl.pallas_call(
        functools.partial(_scan_kernel, n_ctx_seq=n_ctx_seq, lat_tb=lat_tb),
        grid=(n_blocks,),
        in_specs=[fwd] * 6 + [bwd] * 6 + [st_in, st_in],
        out_specs=[fwd, bwd, st_out, st_out],
        out_shape=[y_shape, y_shape, st_shape, st_shape],
        scratch_shapes=[pltpu.VMEM((N_PAIR, PAIR, PAIR), F32), pltpu.VMEM((N_PAIR, PAIR, PAIR), F32)],
        compiler_params=_cparams(("arbitrary",)),
        name="rwkv_scan",
    )(r, v, kk, lw0, kd0, bb0, r, v, kk, lw1, kd1, bb1, s0f, s0b)


def _gmlp_kernel(uv_ref, lg_ref, lb_ref, ws_ref, bs_ref, o_ref, *, n_chunk):
    u = uv_ref[:, 0:D_B].astype(F32)
    v = uv_ref[:, D_B:2 * D_B].astype(F32)
    mu = jnp.mean(v, axis=-1, keepdims=True)
    vc = v - mu
    var = jnp.mean(vc * vc, axis=-1, keepdims=True)
    vn = ((vc * lax.rsqrt(var + LN_EPS)) * lg_ref[...] + lb_ref[...]).astype(BF16)
    for c in range(n_chunk):
        rows = slice(c * CHUNK, (c + 1) * CHUNK)
        for g in range(G_B):
            cols = slice(g * LANES, (g + 1) * LANES)
            s = _dot(ws_ref[g], vn[rows, cols]) + bs_ref[g]
            o_ref[rows, cols] = (u[rows, cols] * s).astype(o_ref.dtype)


def _gmlp(uv, ln_g, ln_b, ws, bs_b, *, n_chunk=GMLP_CHUNKS):
    n = uv.shape[0]
    tm = n_chunk * CHUNK
    full = lambda shape: pl.BlockSpec(shape, lambda i: (0,) * len(shape))
    return pl.pallas_call(
        functools.partial(_gmlp_kernel, n_chunk=n_chunk),
        grid=(n // tm,),
        in_specs=[pl.BlockSpec((tm, 2 * D_B), lambda i: (i, 0)),
                  full((1, D_B)), full((1, D_B)), full((G_B, CHUNK, CHUNK)), full((G_B, CHUNK, LANES))],
        out_specs=pl.BlockSpec((tm, D_B), lambda i: (i, 0)),
        out_shape=jax.ShapeDtypeStruct((n, D_B), BF16),
        compiler_params=_cparams(("parallel",)),
        name="gmlp",
    )(uv, ln_g.reshape(1, D_B), ln_b.reshape(1, D_B), ws, bs_b)


def _merge_kernel(yf_ref, yr_ref, bonus_ref, g_ref, gng_ref, gnb_ref, avg_ref, yb_ref, ga_ref, gb_ref, wpa_ref, wpb_ref,
                  o_ref):
    avg = avg_ref[...]
    parts = []
    for p in range(N_PAIR):
        cols = slice(p * LANES, (p + 1) * LANES)
        yp = yf_ref[:, cols] + yr_ref[:, cols]
        yc = yp - _dot2(yp, avg)
        yn = yc * lax.rsqrt(_dot2(yc * yc, avg) + GN_EPS)
        parts.append(((yn * gng_ref[:, cols] + gnb_ref[:, cols] + bonus_ref[:, cols]) * g_ref[:, cols]).astype(BF16))
    oa = jnp.concatenate(parts, axis=1)
    yb = yb_ref[...]
    for n0 in range(0, o_ref.shape[1], FFN_OUT_CHUNK):
        cols = slice(n0, n0 + FFN_OUT_CHUNK)
        o_ref[:, cols] = (ga_ref[:, cols].astype(F32) * _dot(oa, wpa_ref[:, cols])
                          + gb_ref[:, cols].astype(F32) * _dot(yb, wpb_ref[:, cols])).astype(o_ref.dtype)


def _merge(y_fwd, y_bwd, bonus, g, gn_g, gn_b, yb, act, w_pa, w_pb, *, tm=HALF_ROW_TILE):
    n = y_fwd.shape[0]
    head_of_lane = jnp.arange(LANES) // HEAD
    avg = ((head_of_lane[:, None] == head_of_lane[None, :]).astype(F32) / HEAD).astype(BF16)
    full = lambda shape: pl.BlockSpec(shape, lambda i: (0,) * len(shape))
    row = lambda w, c=0: pl.BlockSpec((tm, w), lambda i: (i, c))
    return pl.pallas_call(
        _merge_kernel,
        grid=(n // tm,),
        in_specs=[row(D_A), row(D_A), row(D_A), row(D_A), full((1, D_A)), full((1, D_A)), full((LANES, LANES)),
                  row(D_B), row(D_MODEL, 1), row(D_MODEL, 2), full((D_A, D_MODEL)), full((D_B, D_MODEL))],
        out_specs=row(D_MODEL),
        out_shape=jax.ShapeDtypeStruct((n, D_MODEL), BF16),
        compiler_params=_cparams(("parallel",)),
        name="merge",
    )(y_fwd, y_bwd, bonus, g, gn_g.reshape(1, D_A), gn_b.reshape(1, D_A), avg, yb, act, act, w_pa, w_pb)


def _w1_layout_kernel(a_ref, *refs):
    b_refs, o_ref = refs[:-1], refs[-1]
    valid = D_FF - pl.program_id(1) * D_FF_TILE
    keep = lax.broadcasted_iota(jnp.int32, (1, D_FF_TILE), 1) < valid
    a = jnp.where(keep, a_ref[0], 0.0)
    b = jnp.where(keep, jnp.concatenate([r[0] for r in b_refs], axis=1), 0.0)
    o_ref[0] = jnp.concatenate([a, b], axis=1).astype(BF16)


def _w2_layout_kernel(*refs):
    w_refs, o_ref = refs[:-1], refs[-1]
    valid = D_FF - pl.program_id(1) * D_FF_TILE
    keep = lax.broadcasted_iota(jnp.int32, (D_FF_TILE, 1), 0) < valid
    o_ref[0] = jnp.where(keep, jnp.concatenate([r[0] for r in w_refs], axis=0), 0.0).astype(BF16)


def _ffn_weight_layouts(w_in, w_out):
    s, d, _ = w_in.shape
    nj = D_FF_PAD // D_FF_TILE
    sub = D_FF_TILE // LANES
    first_b = D_FF // LANES
    last_blk = 2 * D_FF // LANES - 1
    b_spec = lambda k: pl.BlockSpec((1, d, LANES), lambda i, j: (i, 0, jnp.minimum(first_b + sub * j + k, last_blk)))
    w1 = pl.pallas_call(
        _w1_layout_kernel,
        grid=(s, nj),
        in_specs=[pl.BlockSpec((1, d, D_FF_TILE), lambda i, j: (i, 0, j))] + [b_spec(k) for k in range(sub)],
        out_specs=pl.BlockSpec((1, d, 2 * D_FF_TILE), lambda i, j: (i, 0, j)),
        out_shape=jax.ShapeDtypeStruct((s, d, 2 * D_FF_PAD), BF16),
        compiler_params=_cparams(("parallel", "parallel")),
        name="w1_layout",
    )(w_in, *([w_in] * sub))
    last_row_blk = D_FF // LANES - 1
    r_spec = lambda k: pl.BlockSpec((1, LANES, d), lambda i, j: (i, jnp.minimum(sub * j + k, last_row_blk), 0))
    w2 = pl.pallas_call(
        _w2_layout_kernel,
        grid=(s, nj),
        in_specs=[r_spec(k) for k in range(sub)],
        out_specs=pl.BlockSpec((1, D_FF_TILE, d), lambda i, j: (i, j, 0)),
        out_shape=jax.ShapeDtypeStruct((s, D_FF_PAD, d), BF16),
        compiler_params=_cparams(("parallel", "parallel")),
        name="w2_layout",
    )(*([w_out] * sub))
    return w1, w2


def _cast_rows_t_kernel(*refs, row_off, row0, n_valid):
    in_refs, o_ref = refs[:-1], refs[-1]
    tn = o_ref.shape[1]
    rows = in_refs[0][...]
    if row_off:
        rows = jnp.concatenate([rows[row_off:tn, :], in_refs[1][0:row_off, :]], axis=0)
    col = pl.program_id(0) * tn + lax.broadcasted_iota(jnp.int32, (tn, 1), 0)
    o_ref[...] = jnp.where(col < n_valid, rows, 0.0).T.astype(BF16)


def _cast_rows_t(wt, row0, n_valid, n_out, *, tn):
    nrow, d = wt.shape
    blk0, row_off = divmod(row0, tn)
    assert n_out % tn == 0 and row_off % 8 == 0 and row0 + n_out <= pl.cdiv(nrow, tn) * tn
    last_blk = pl.cdiv(nrow, tn) - 1
    spec = lambda k: pl.BlockSpec((tn, d), lambda j: (jnp.minimum(blk0 + j + k, last_blk), 0))
    n_in = 2 if row_off else 1
    return pl.pallas_call(
        functools.partial(_cast_rows_t_kernel, row_off=row_off, row0=row0, n_valid=n_valid),
        grid=(n_out // tn,),
        in_specs=[spec(k) for k in range(n_in)],
        out_specs=pl.BlockSpec((d, tn), lambda j: (0, j)),
        out_shape=jax.ShapeDtypeStruct((d, n_out), BF16),
        compiler_params=_cparams(("parallel",)),
        name="cast_rows_t",
    )(*([wt] * n_in))


def _pair_states(s):
    b = s.shape[0]
    s = s.reshape(b, N_PAIR, 2, HEAD, HEAD)
    z = jnp.zeros_like(s[:, :, 0])
    top = jnp.concatenate([s[:, :, 0], z], axis=-1)
    bot = jnp.concatenate([z, s[:, :, 1]], axis=-1)
    return jnp.concatenate([top, bot], axis=-2)


def _unpair_states(sp):
    b = sp.shape[0]
    return jnp.stack([sp[:, :, :HEAD, :HEAD], sp[:, :, HEAD:, HEAD:]], axis=2).reshape(b, H_A, HEAD, HEAD)


def kernel(x_prompt, x_sample, c, state_rwkv_fwd, state_rwkv_bwd, c_ctx, w_ada, b_ada, ln_g, ln_b,
           ffn_w_in, ffn_w_out, w_in, shift_mu, rw_w0, rw_w2, rw_a0, rw_a2, rw_g2, rw_k_k, rw_k_a,
           rw_r_k, rw_gn_g, rw_gn_b, sg_ln_g, sg_ln_b, sg_w, sg_b, w_pa, w_pb, w_o):
    n_ctx_seq, t_ctx, d = x_prompt.shape
    n_lat_seq, t_lat, _ = x_sample.shape
    n_ctx = n_ctx_seq * t_ctx
    n_lat = n_lat_seq * t_lat
    tm = ROW_TILE
    ctx_tiles = n_ctx // tm

    def group_of_tile(rows):
        assert n_ctx % rows == 0 and t_lat % rows == 0
        return lambda i: jnp.where(i < n_ctx // rows, 0, 1 + (i - n_ctx // rows) // (t_lat // rows))

    group_of = group_of_tile(tm)

    x = None
    cond8 = jnp.zeros((16, d), F32).at[0].set(c_ctx).at[1:1 + n_lat_seq].set(c)
    new_f, new_b = [], []
    for l in range(DEPTH):
        w1, w2 = _ffn_weight_layouts(ffn_w_in[l], ffn_w_out[l])
        wt = w_in[l].T
        w_zs = _cast_rows_t(wt, 0, N_SHIFT, N_SHIFT_PAD, tn=CAST_HEAD_TN)
        w_act = _cast_rows_t(wt, N_SHIFT, wt.shape[0] - N_SHIFT, wt.shape[0] - N_SHIFT, tn=CAST_SHIFT_TN)
        mu_p = jnp.pad(shift_mu[l], (0, N_SHIFT_PAD - N_SHIFT))
        zpad = jnp.zeros((2, LANES - LORA_W, D_A), F32)
        w2p = jnp.concatenate([rw_w2[l], zpad], axis=1).astype(BF16)
        a2p = jnp.concatenate([zpad, rw_a2[l]], axis=1).astype(BF16)
        g2p = jnp.pad(rw_g2[l], ((0, LORA_G_PAD - LORA_G), (0, 0))).astype(BF16)
        bs_b = jnp.broadcast_to(sg_b[l][:, :, None], (G_B, CHUNK, LANES))

        mod = _ada(cond8, w_ada[l], b_ada[l]).reshape(16, 9, d)

        ffn1 = functools.partial(_ffn, mod3=mod, w1=w1, w2=w2, ln_g=ln_g[l, 0], ln_b=ln_b[l, 0], group_of=group_of,
                                 which=0, tm=tm, out_rows=n_ctx + n_lat)
        if x is None:
            x = ffn1(x_prompt.reshape(n_ctx, d), into=jnp.zeros((n_ctx + n_lat, d), F32))
            x = ffn1(x_sample.reshape(n_lat, d), group_tile0=ctx_tiles, out_tile0=ctx_tiles, into=x)
        else:
            x = ffn1(x)

        zs = _mm_mod(x, mod, w_zs, group_of_tile(HALF_ROW_TILE), which=1, act=_act_none, tm=HALF_ROW_TILE, tn=N_SHIFT_PAD,
                     tn_out=N_SHIFT_PAD, out_dtype=F32)
        act = _mm_mod(x, mod, w_act, group_of, which=1, act=_act_gelu, act_tail=_act_sigmoid,
                      tail_from=2 * D_B // ACT_TN, tm=tm, tn=ACT_TN, tn_out=ACT_TN, out_dtype=BF16)
        prep = _rwkv_prep(zs, mu_p, w2p, a2p, g2p, rw_w0[l], rw_a0[l], rw_k_k[l], rw_k_a[l], rw_r_k[l].reshape(D_A),
                          n_ctx=n_ctx, t_ctx=t_ctx, t_lat=t_lat)
        bonus, g_out = prep[9], prep[10]
        assert t_ctx == SCAN_TB
        y_fwd, y_bwd, sf, sb = _rwkv_scan(prep[:9], _pair_states(state_rwkv_fwd[:, l]),
                                          _pair_states(state_rwkv_bwd[:, l]),
                                          n_ctx_seq=n_ctx_seq, n_lat_seq=n_lat_seq, t_lat=t_lat)
        new_f.append(_unpair_states(sf[:n_ctx_seq]))
        new_b.append(_unpair_states(sb[:n_ctx_seq]))
        assert 2 * D_B == D_MODEL
        yb = _gmlp(act, sg_ln_g[l], sg_ln_b[l], sg_w[l].astype(BF16), bs_b)
        merged = _merge(y_fwd, y_bwd, bonus, g_out, rw_gn_g[l], rw_gn_b[l], yb, act,
                        w_pa[l].astype(BF16), w_pb[l].astype(BF16))
        x = _mm_res_ln(merged, w_o[l].astype(BF16), x, mod, ln_g[l, 1], ln_b[l, 1], group_of, gate_row=5, coef=1.0,
                       tm=tm, tk=WO_TK)

        ffn2 = functools.partial(_ffn, x, mod, w1, w2, ln_g[l, 2], ln_b[l, 2], group_of, which=2, tm=tm)
        x_ctx = ffn2(tile0=0, n_tiles=ctx_tiles)
        x_lat = ffn2(tile0=ctx_tiles, n_tiles=n_lat // tm)
        if l + 1 < DEPTH:
            x = jnp.concatenate([x_ctx, x_lat], axis=0)

    y_prompt = x_ctx.reshape(n_ctx_seq, t_ctx, d)
    y_sample = x_lat.reshape(n_lat_seq, t_lat, d)
    new_state_fwd = jnp.stack(new_f, axis=1).astype(x_prompt.dtype)
    new_state_bwd = jnp.stack(new_b, axis=1).astype(x_prompt.dtype)
    return (y_prompt, y_sample, new_state_fwd, new_state_bwd)
```
